```python
import math
import jax, jax.numpy as jnp
from jax import lax
import numpy as np

D_MODEL = 1024
BATCH = 16
SEQ = 2048
DEPTH = 4
DEC_BATCH = 128
DEC_SEQ = 4
PAST_LEN = 8192
PAGE_SIZE = 128

N_A_LAYERS = DEPTH // 2
N_B_LAYERS = DEPTH - N_A_LAYERS
SSM_EXPAND = 2
SSM_WIDTH = SSM_EXPAND * D_MODEL
GROUP_CH = 16
N_GROUPS = SSM_WIDTH // GROUP_CH
STATE_DIM = 64
SCAN_CHUNK = 128
N_HEADS = D_MODEL // 128
QK_NOPE = 128
QK_ROPE = 64
V_HEAD = 128
KV_LORA = D_MODEL // 4
Q_LORA = 3 * D_MODEL // 8
ATTN_WIDTH = N_HEADS * V_HEAD
Q_BLOCK = 128
ROPE_THETA = 10000.0
RMS_EPS = 1e-6
SOFTMAX_SCALE = 1.0 / math.sqrt(QK_NOPE + QK_ROPE)
NEG_INF = -1e30

kernel_name = 'yoco_s5_mla_hybrid_step'


def rms_norm(x, g):
    x32 = x.astype(jnp.float32)
    y = x32 * lax.rsqrt(jnp.mean(x32 * x32, axis=-1, keepdims=True) + RMS_EPS) * g.astype(jnp.float32)
    return y.astype(x.dtype)


def rope(x, pos):
    half = x.shape[-1] // 2
    inv = ROPE_THETA ** (-jnp.arange(half, dtype=jnp.float32) / half)
    ang = pos.astype(jnp.float32)[:, None] * inv[None, :]
    shape = (1, pos.shape[0]) + (1,) * (x.ndim - 3) + (half,)
    cos = jnp.cos(ang).reshape(shape)
    sin = jnp.sin(ang).reshape(shape)
    x1 = x[..., :half].astype(jnp.float32)
    x2 = x[..., half:].astype(jnp.float32)
    return jnp.concatenate([x1 * cos - x2 * sin, x1 * sin + x2 * cos], axis=-1).astype(x.dtype)


def s5_scan(u, h0_re, h0_im, a_re, a_im, log_dt, b_re, b_im, c_re, c_im, d_skip):
    bsz, L, _ = u.shape
    u32 = u.astype(jnp.float32).reshape(bsz, L, N_GROUPS, GROUP_CH)
    lam_re = a_re.astype(jnp.float32)
    lam_im = a_im.astype(jnp.float32)
    dt = jnp.exp(log_dt.astype(jnp.float32))[:, None]
    mag = jnp.exp(lam_re * dt)
    lb_re = mag * jnp.cos(lam_im * dt)
    lb_im = mag * jnp.sin(lam_im * dt)
    den = lam_re * lam_re + lam_im * lam_im
    nr = lb_re - 1.0
    f_re = (nr * lam_re + lb_im * lam_im) / den
    f_im = (lb_im * lam_re - nr * lam_im) / den
    br = b_re.astype(jnp.float32)
    bi = b_im.astype(jnp.float32)
    bb_re = f_re[..., None] * br - f_im[..., None] * bi
    bb_im = f_re[..., None] * bi + f_im[..., None] * br
    cr = c_re.astype(jnp.float32)
    ci = c_im.astype(jnp.float32)
    ch = SCAN_CHUNK if L % SCAN_CHUNK == 0 else L
    nc = L // ch
    u_chunks = u32.reshape(bsz, nc, ch, N_GROUPS, GROUP_CH).swapaxes(0, 1)

    def combine(e1, e2):
        a1r, a1i, b1r, b1i = e1
        a2r, a2i, b2r, b2i = e2
        return (a1r * a2r - a1i * a2i, a1r * a2i + a1i * a2r,
                a2r * b1r - a2i * b1i + b2r, a2r * b1i + a2i * b1r + b2i)

    def step(carry, uc):
        h_re, h_im = carry
        bu_re = jnp.einsum('gpc,blgc->blgp', bb_re, uc)
        bu_im = jnp.einsum('gpc,blgc->blgp', bb_im, uc)
        a_r = jnp.broadcast_to(lb_re, bu_re.shape)
        a_i = jnp.broadcast_to(lb_im, bu_im.shape)
        cum_r, cum_i, loc_r, loc_i = lax.associative_scan(combine, (a_r, a_i, bu_re, bu_im), axis=1)
        xr = cum_r * h_re[:, None] - cum_i * h_im[:, None] + loc_r
        xi = cum_r * h_im[:, None] + cum_i * h_re[:, None] + loc_i
        y = jnp.einsum('gcp,blgp->blgc', cr, xr) - jnp.einsum('gcp,blgp->blgc', ci, xi)
        return (xr[:, -1], xi[:, -1]), y

    (h_re, h_im), ys = lax.scan(step, (h0_re.astype(jnp.float32), h0_im.astype(jnp.float32)), u_chunks)
    y = ys.swapaxes(0, 1).reshape(bsz, L, SSM_WIDTH) + d_skip.astype(jnp.float32) * u32.reshape(bsz, L, SSM_WIDTH)
    return y.astype(u.dtype), h_re, h_im


def s5_block(x, h0_re, h0_im, g, w_in, a_re, a_im, log_dt, b_re, b_im, c_re, c_im, d_skip, w_glu, b_glu, w_out):
    h = rms_norm(x, g)
    u, z = jnp.split(h @ w_in, 2, axis=-1)
    y, h_re, h_im = s5_scan(u, h0_re, h0_im, a_re, a_im, log_dt, b_re, b_im, c_re, c_im, d_skip)
    ga, gb = jnp.split(jax.nn.gelu(y) @ w_glu + b_glu, 2, axis=-1)
    v = ga * jax.nn.sigmoid(gb) * jax.nn.silu(z)
    return x + v @ w_out, h_re, h_im


def shared_latent(x, pos, g_in, w_dkv, g_lat):
    h = rms_norm(x, g_in)
    ckv = h @ w_dkv
    return rms_norm(ckv[..., :KV_LORA], g_lat), rope(ckv[..., KV_LORA:], pos)


def latent_attention(q_lat, q_rope, keys_lat, keys_kr, q_pos, k_pos):
    bsz, L, H, C = q_lat.shape
    qb = Q_BLOCK if L % Q_BLOCK == 0 else L
    nb = L // qb

    def to_blocks(t):
        return t.reshape((bsz, nb, qb) + t.shape[2:]).swapaxes(0, 1)

    def attend_block(args):
        ql, qr, qp = args
        s = (jnp.einsum('bqhc,bkc->bhqk', ql, keys_lat, preferred_element_type=jnp.float32)
             + jnp.einsum('bqhr,bkr->bhqk', qr, keys_kr, preferred_element_type=jnp.float32))
        s = jnp.where(k_pos[None, :] <= qp[:, None], s * SOFTMAX_SCALE, NEG_INF)
        p = jax.nn.softmax(s, axis=-1).astype(keys_lat.dtype)
        return jnp.einsum('bhqk,bkc->bqhc', p, keys_lat)

    o = lax.map(attend_block, (to_blocks(q_lat), to_blocks(q_rope), q_pos.reshape(nb, qb)))
    return o.swapaxes(0, 1).reshape(bsz, L, H, C)


def mla_block(x, q_pos, keys_lat, keys_kr, k_pos, w_uk, w_uv, g, w_in, g_q, w_uq, w_out):
    bsz, L, _ = x.shape
    h = rms_norm(x, g)
    cq_gate = h @ w_in
    cq = rms_norm(cq_gate[..., :Q_LORA], g_q)
    gate = cq_gate[..., Q_LORA:]
    q = (cq @ w_uq).reshape(bsz, L, N_HEADS, QK_NOPE + QK_ROPE)
    q_nope = q[..., :QK_NOPE]
    q_rope = rope(q[..., QK_NOPE:], q_pos)
    q_lat = jnp.einsum('blhn,chn->blhc', q_nope, w_uk.reshape(KV_LORA, N_HEADS, QK_NOPE))
    o_lat = latent_attention(q_lat, q_rope, keys_lat, keys_kr, q_pos, k_pos)
    o = jnp.einsum('blhc,chv->blhv', o_lat, w_uv.reshape(KV_LORA, N_HEADS, V_HEAD)).reshape(bsz, L, ATTN_WIDTH)
    return x + (o * jax.nn.silu(gate)) @ w_out


def setup_inputs(seed: int = 0) -> dict:
    key = jax.random.key(seed)
    ks = iter(jax.random.split(key, 48))

    def nrm(shape, scale):
        return jax.random.normal(next(ks), shape, jnp.float32) * scale

    n_pages = PAST_LEN // PAGE_SIZE
    n_used = DEC_BATCH * n_pages
    n_pool = n_used + n_used // 4
    na, nb = N_A_LAYERS, N_B_LAYERS
    x_prompt = nrm((BATCH, SEQ, D_MODEL), 1.0)
    x_sample = nrm((DEC_BATCH, DEC_SEQ, D_MODEL), 1.0)
    cache_latent = nrm((n_pool, PAGE_SIZE, KV_LORA), 1.0)
    cache_krope = nrm((n_pool, PAGE_SIZE, QK_ROPE), 1.0)
    page_table = jax.random.permutation(next(ks), n_pool)[:n_used].reshape(DEC_BATCH, n_pages).astype(jnp.int32)
    state_ssm_re = nrm((na, DEC_BATCH, N_GROUPS, STATE_DIM), 0.5)
    state_ssm_im = nrm((na, DEC_BATCH, N_GROUPS, STATE_DIM), 0.5)
    norm_a = 1.0 + nrm((na, D_MODEL), 0.02)
    w_in_a = nrm((na, D_MODEL, 2 * SSM_WIDTH), D_MODEL ** -0.5)
    a_re = -0.5 + nrm((na, N_GROUPS, STATE_DIM), 0.01)
    a_im = jnp.pi * jnp.arange(STATE_DIM, dtype=jnp.float32) + nrm((na, N_GROUPS, STATE_DIM), 0.01)
    log_dt = jax.random.uniform(next(ks), (na, N_GROUPS), jnp.float32, math.log(0.001), math.log(0.1))
    b_re = nrm((na, N_GROUPS, STATE_DIM, GROUP_CH), (2 * GROUP_CH) ** -0.5)
    b_im = nrm((na, N_GROUPS, STATE_DIM, GROUP_CH), (2 * GROUP_CH) ** -0.5)
    c_re = nrm((na, N_GROUPS, GROUP_CH, STATE_DIM), (2 * STATE_DIM) ** -0.5)
    c_im = nrm((na, N_GROUPS, GROUP_CH, STATE_DIM), (2 * STATE_DIM) ** -0.5)
    d_skip = nrm((na, SSM_WIDTH), 0.5)
    w_glu = nrm((na, SSM_WIDTH, 2 * SSM_WIDTH), SSM_WIDTH ** -0.5)
    b_glu = nrm((na, 2 * SSM_WIDTH), 0.01)
    w_out_a = nrm((na, SSM_WIDTH, D_MODEL), SSM_WIDTH ** -0.5)
    norm_kv = 1.0 + nrm((D_MODEL,), 0.02)
    w_dkv = nrm((D_MODEL, KV_LORA + QK_ROPE), D_MODEL ** -0.5)
    norm_latent = 1.0 + nrm((KV_LORA,), 0.02)
    w_uk = nrm((KV_LORA, N_HEADS * QK_NOPE), KV_LORA ** -0.5)
    w_uv = nrm((KV_LORA, N_HEADS * V_HEAD), KV_LORA ** -0.5)
    norm_b = 1.0 + nrm((nb, D_MODEL), 0.02)
    w_in_b = nrm((nb, D_MODEL, Q_LORA + ATTN_WIDTH), D_MODEL ** -0.5)
    norm_q = 1.0 + nrm((nb, Q_LORA), 0.02)
    w_uq = nrm((nb, Q_LORA, N_HEADS * (QK_NOPE + QK_ROPE)), Q_LORA ** -0.5)
    w_out_b = nrm((nb, ATTN_WIDTH, D_MODEL), ATTN_WIDTH ** -0.5)
    norm_f = 1.0 + nrm((D_MODEL,), 0.02)
    return {'x_prompt': x_prompt, 'x_sample': x_sample, 'cache_latent': cache_latent, 'cache_krope': cache_krope,
            'page_table': page_table, 'state_ssm_re': state_ssm_re, 'state_ssm_im': state_ssm_im,
            'norm_a': norm_a, 'w_in_a': w_in_a, 'a_re': a_re, 'a_im': a_im, 'log_dt': log_dt,
            'b_re': b_re, 'b_im': b_im, 'c_re': c_re, 'c_im': c_im, 'd_skip': d_skip,
            'w_glu': w_glu, 'b_glu': b_glu, 'w_out_a': w_out_a,
            'norm_kv': norm_kv, 'w_dkv': w_dkv, 'norm_latent': norm_latent, 'w_uk': w_uk, 'w_uv': w_uv,
            'norm_b': norm_b, 'w_in_b': w_in_b, 'norm_q': norm_q, 'w_uq': w_uq, 'w_out_b': w_out_b,
            'norm_f': norm_f}


def reference(x_prompt, x_sample, cache_latent, cache_krope, page_table, state_ssm_re, state_ssm_im,
              norm_a, w_in_a, a_re, a_im, log_dt, b_re, b_im, c_re, c_im, d_skip, w_glu, b_glu, w_out_a,
              norm_kv, w_dkv, norm_latent, w_uk, w_uv,
              norm_b, w_in_b, norm_q, w_uq, w_out_b, norm_f):
    n_seq, n_pages = page_table.shape
    past_len = n_pages * PAGE_SIZE
    pos_p = jnp.arange(x_prompt.shape[1], dtype=jnp.int32)
    pos_s = past_len + jnp.arange(x_sample.shape[1], dtype=jnp.int32)
    kpos_s = jnp.arange(past_len + x_sample.shape[1], dtype=jnp.int32)
    zeros_p = jnp.zeros((x_prompt.shape[0], N_GROUPS, STATE_DIM), jnp.float32)
    xp, xs = x_prompt, x_sample
    hp_re, hp_im, hs_re, hs_im = [], [], [], []
    for i in range(DEPTH):
        if i < N_A_LAYERS:
            a_par = (norm_a[i], w_in_a[i], a_re[i], a_im[i], log_dt[i], b_re[i], b_im[i],
                     c_re[i], c_im[i], d_skip[i], w_glu[i], b_glu[i], w_out_a[i])
            xp, r, m = s5_block(xp, zeros_p, zeros_p, *a_par)
            hp_re.append(r)
            hp_im.append(m)
            xs, r, m = s5_block(xs, state_ssm_re[i], state_ssm_im[i], *a_par)
            hs_re.append(r)
            hs_im.append(m)
        else:
            if i == N_A_LAYERS:
                lat_p, kr_p = shared_latent(xp, pos_p, norm_kv, w_dkv, norm_latent)
                lat_s, kr_s = shared_latent(xs, pos_s, norm_kv, w_dkv, norm_latent)
                past_lat = cache_latent[page_table].reshape(n_seq, past_len, KV_LORA)
                past_kr = cache_krope[page_table].reshape(n_seq, past_len, QK_ROPE)
                keys_lat_s = jnp.concatenate([past_lat, lat_s.astype(past_lat.dtype)], axis=1)
                keys_kr_s = jnp.concatenate([past_kr, kr_s.astype(past_kr.dtype)], axis=1)
            j = i - N_A_LAYERS
            b_par = (norm_b[j], w_in_b[j], norm_q[j], w_uq[j], w_out_b[j])
            xp = mla_block(xp, pos_p, lat_p, kr_p, pos_p, w_uk, w_uv, *b_par)
            xs = mla_block(xs, pos_s, keys_lat_s, keys_kr_s, kpos_s, w_uk, w_uv, *b_par)
    y_prompt = rms_norm(xp, norm_f)
    y_sample = rms_norm(xs, norm_f)
    return (y_prompt, y_sample, lat_p, kr_p, lat_s, kr_s,
            jnp.stack(hp_re), jnp.stack(hp_im), jnp.stack(hs_re), jnp.stack(hs_im))
```

```python
import functools
import math

import jax
import jax.numpy as jnp
from jax import lax
from jax.experimental import pallas as pl
from jax.experimental.pallas import tpu as pltpu

F32 = jnp.float32
BF16 = jnp.bfloat16

GROUP_CH = 16
STATE_DIM = 64
N_HEADS = 8
QK_NOPE = 128
QK_ROPE = 64
V_HEAD = 128
KV_LORA = 256
Q_LORA = 384
PAGE_SIZE = 128
ROPE_THETA = 10000.0
RMS_EPS = 1e-6
SOFTMAX_SCALE = 1.0 / math.sqrt(QK_NOPE + QK_ROPE)
NEG_INF = -1e30
LANES = 128
QK_PAD = KV_LORA + LANES
PROMPT_CHUNK = 16
VMEM_LIMIT = 56 * 1024 * 1024


def _dot(a, b):
    return jnp.dot(a, b, preferred_element_type=F32)


def _dot_nt(a, b):
    return lax.dot_general(a, b, (((1,), (1,)), ((), ())), preferred_element_type=F32)


def _dot_f32(a, b):
    return jnp.dot(a, b, preferred_element_type=F32, precision=lax.Precision.HIGHEST)


def _rms(x, g):
    return x * lax.rsqrt(jnp.mean(x * x, axis=-1, keepdims=True) + RMS_EPS) * g


def _params(*sem):
    return pltpu.CompilerParams(dimension_semantics=sem, vmem_limit_bytes=VMEM_LIMIT)


def _norm_matmul_kernel(x_ref, g_ref, w_ref, o_ref, h_s):
    @pl.when((pl.program_id(1) == 0) & (pl.program_id(2) == 0))
    def _():
        h_s[...] = _rms(x_ref[...], g_ref[...]).astype(BF16)

    o_ref[0] = _dot(h_s[...], w_ref[0])


def _norm_matmul(x, g, w3, *, tm, tn, name):
    m, d = x.shape
    parts, _, n = w3.shape
    tm = min(tm, m)
    return pl.pallas_call(
        _norm_matmul_kernel,
        grid=(m // tm, parts, n // tn),
        in_specs=[
            pl.BlockSpec((tm, d), lambda i, p, j: (i, 0)),
            pl.BlockSpec((1, d), lambda i, p, j: (0, 0)),
            pl.BlockSpec((1, d, tn), lambda i, p, j: (p, 0, j)),
        ],
        out_specs=pl.BlockSpec((1, tm, tn), lambda i, p, j: (p, i, j)),
        out_shape=jax.ShapeDtypeStruct((parts, m, n), F32),
        scratch_shapes=[pltpu.VMEM((tm, d), BF16)],
        compiler_params=_params("parallel", "arbitrary", "arbitrary"),
        name=name,
    )(x, g, w3)


def _discretize(lam_re, lam_im, log_dt):
    dt = jnp.exp(log_dt)
    mag = jnp.exp(lam_re * dt)
    lb_re = mag * jnp.cos(lam_im * dt)
    lb_im = mag * jnp.sin(lam_im * dt)
    den = lam_re * lam_re + lam_im * lam_im
    nr = lb_re - 1.0
    f_re = (nr * lam_re + lb_im * lam_im) / den
    f_im = (lb_im * lam_re - nr * lam_im) / den
    return lb_re, lb_im, f_re, f_im


def _cpow(br, bi, e, nbits, shape):
    br = jnp.broadcast_to(br, shape)
    bi = jnp.broadcast_to(bi, shape)
    rr = jnp.ones(shape, F32)
    ri = jnp.zeros(shape, F32)
    for j in range(nbits):
        bit = ((e >> j) & 1) == 1
        nr = rr * br - ri * bi
        ni = rr * bi + ri * br
        rr = jnp.where(bit, nr, rr)
        ri = jnp.where(bit, ni, ri)
        if j + 1 < nbits:
            br, bi = br * br - bi * bi, 2.0 * br * bi
    return rr, ri


def _s5_prep_kernel(pcol_ref, prow_ref, btr_ref, bti_ref, ctr_ref, cti_ref,
                    toep_ref, wsr_ref, wsi_ref, wor_ref, woi_ref, at_ref, *, chunk):
    tc = chunk * GROUP_CH
    tl = ctr_ref.shape[2]
    nbits = chunk.bit_length()
    shift = GROUP_CH.bit_length() - 1

    pc = pcol_ref[0]
    lbr_c, lbi_c, _, _ = _discretize(pc[:, 0:1], pc[:, 1:2], pc[:, 2:3])
    pr = prow_ref[0]
    lbr_r, lbi_r, f_re, f_im = _discretize(pr[0:1], pr[1:2], pr[2:3])

    cr = ctr_ref[0]
    ci = cti_ref[0]
    shape = cr.shape
    lane_t = lax.broadcasted_iota(jnp.int32, shape, 1) >> shift
    p0r, p0i = _cpow(lbr_c, lbi_c, lane_t, nbits, shape)
    p1r = p0r * lbr_c - p0i * lbi_c
    p1i = p0r * lbi_c + p0i * lbr_c
    rr = cr * p0r - ci * p0i
    ri = cr * p0i + ci * p0r
    wor_ref[0] = (cr * p1r - ci * p1i)[:, :tc].astype(BF16)
    woi_ref[0] = (-(cr * p1i + ci * p1r))[:, :tc].astype(BF16)

    btr = btr_ref[0]
    bti = bti_ref[0]
    bbr = f_re * btr - f_im * bti
    bbi = f_re * bti + f_im * btr

    krow = _dot_f32(bbr[:GROUP_CH], rr) - _dot_f32(bbi[:GROUP_CH], ri)
    lane = lax.broadcasted_iota(jnp.int32, (GROUP_CH, tl), 1)
    for s in range(chunk):
        if s == 0:
            blk = krow
        else:
            blk = jnp.where(lane >= GROUP_CH * s, pltpu.roll(krow, GROUP_CH * s, 1), 0.0)
        toep_ref[0, GROUP_CH * s:GROUP_CH * (s + 1), :] = blk[:, :tc].astype(BF16)

    shape_s = btr.shape
    row_e = (chunk - 1) - (lax.broadcasted_iota(jnp.int32, shape_s, 0) >> shift)
    qr, qi = _cpow(lbr_r, lbi_r, row_e, nbits, shape_s)
    wsr_ref[0] = (bbr * qr - bbi * qi).astype(BF16)
    wsi_ref[0] = (bbr * qi + bbi * qr).astype(BF16)

    ar, ai = lbr_r, lbi_r
    for _ in range(chunk.bit_length() - 1):
        ar, ai = ar * ar - ai * ai, 2.0 * ar * ai
    at_ref[0, 0:1, :] = ar
    at_ref[0, 1:2, :] = ai


def _s5_prep(a_re, a_im, log_dt, b_re, b_im, c_re, c_im, *, chunk):
    g, p = a_re.shape
    tc = chunk * GROUP_CH
    tl = max(tc, LANES)
    ldt = jnp.broadcast_to(log_dt[:, None], (g, p))
    pcol = jnp.stack([a_re, a_im, ldt], axis=-1)
    prow = jnp.stack([a_re, a_im, ldt], axis=1)
    bt_re = jnp.tile(jnp.swapaxes(b_re, 1, 2), (1, chunk, 1))
    bt_im = jnp.tile(jnp.swapaxes(b_im, 1, 2), (1, chunk, 1))
    ct_re = jnp.tile(jnp.swapaxes(c_re, 1, 2), (1, 1, tl // GROUP_CH))
    ct_im = jnp.tile(jnp.swapaxes(c_im, 1, 2), (1, 1, tl // GROUP_CH))

    def spec(a, b):
        return pl.BlockSpec((1, a, b), lambda i: (i, 0, 0))

    return pl.pallas_call(
        functools.partial(_s5_prep_kernel, chunk=chunk),
        grid=(g,),
        in_specs=[spec(p, 3), spec(3, p), spec(tc, p), spec(tc, p), spec(p, tl), spec(p, tl)],
        out_specs=[spec(tc, tc), spec(tc, p), spec(tc, p), spec(p, tc), spec(p, tc), spec(2, p)],
        out_shape=[
            jax.ShapeDtypeStruct((g, tc, tc), BF16),
            jax.ShapeDtypeStruct((g, tc, p), BF16),
            jax.ShapeDtypeStruct((g, tc, p), BF16),
            jax.ShapeDtypeStruct((g, p, tc), BF16),
            jax.ShapeDtypeStruct((g, p, tc), BF16),
            jax.ShapeDtypeStruct((g, 2, p), F32),
        ],
        compiler_params=_params("parallel"),
        name=f"s5_prep_t{chunk}",
    )(pcol, prow, bt_re, bt_im, ct_re, ct_im)


def _s5_scan_kernel(u_ref, toep_ref, wsr_ref, wsi_ref, wor_ref, woi_ref, at_ref, h0r_ref, h0i_ref,
                    y_ref, hr_ref, hi_ref, sr_s, si_s, pr_s, pi_s, *, n_chunks, bsz):
    u = u_ref[0]
    y_intra = _dot(u, toep_ref[0])
    sr_s[...] = _dot(u, wsr_ref[0])
    si_s[...] = _dot(u, wsi_ref[0])
    ar = at_ref[0, 0:1, :]
    ai = at_ref[0, 1:2, :]

    def body(k, carry):
        hr, hi = carry
        rows = pl.ds(pl.multiple_of(k * bsz, bsz), bsz)
        pr_s[rows, :] = hr
        pi_s[rows, :] = hi
        return (ar * hr - ai * hi + sr_s[rows, :], ar * hi + ai * hr + si_s[rows, :])

    hr, hi = lax.fori_loop(0, n_chunks, body, (h0r_ref[0], h0i_ref[0]))
    hr_ref[0] = hr
    hi_ref[0] = hi
    y_ref[0] = (y_intra + _dot(pr_s[...].astype(BF16), wor_ref[0])
                + _dot(pi_s[...].astype(BF16), woi_ref[0]))


def _s5_scan(u_l, ops, h0r, h0i, *, n_chunks, bsz):
    toep, wsr, wsi, wor, woi, at = ops
    g, n, tc = u_l.shape
    p = STATE_DIM

    def spec(a, b):
        return pl.BlockSpec((1, a, b), lambda i: (i, 0, 0))

    return pl.pallas_call(
        functools.partial(_s5_scan_kernel, n_chunks=n_chunks, bsz=bsz),
        grid=(g,),
        in_specs=[spec(n, tc), spec(tc, tc), spec(tc, p), spec(tc, p), spec(p, tc), spec(p, tc),
                  spec(2, p), spec(bsz, p), spec(bsz, p)],
        out_specs=[spec(n, tc), spec(bsz, p), spec(bsz, p)],
        out_shape=[
            jax.ShapeDtypeStruct((g, n, tc), F32),
            jax.ShapeDtypeStruct((g, bsz, p), F32),
            jax.ShapeDtypeStruct((g, bsz, p), F32),
        ],
        scratch_shapes=[pltpu.VMEM((n, p), F32)] * 4,
        compiler_params=_params("parallel"),
        name=f"s5_scan_n{n}",
    )(u_l, toep, wsr, wsi, wor, woi, at, h0r, h0i)


def _s5_glu_kernel(y_ref, u_ref, z_ref, d_ref, wa_ref, wb_ref, ba_ref, bb_ref, v_ref, a_s):
    @pl.when(pl.program_id(1) == 0)
    def _():
        a_s[...] = jax.nn.gelu(y_ref[...] + d_ref[...] * u_ref[0]).astype(BF16)

    a = a_s[...]
    ga = _dot(a, wa_ref[0]) + ba_ref[0]
    gb = _dot(a, wb_ref[0]) + bb_ref[0]
    z = z_ref[0]
    v_ref[...] = (ga * jax.nn.sigmoid(gb) * (z * jax.nn.sigmoid(z))).astype(BF16)


def _s5_glu(y, uz, d_skip, w_glu2, b_glu2, *, tm, tn):
    m, w = y.shape
    tm = min(tm, m)
    return pl.pallas_call(
        _s5_glu_kernel,
        grid=(m // tm, w // tn),
        in_specs=[
            pl.BlockSpec((tm, w), lambda i, j: (i, 0)),
            pl.BlockSpec((1, tm, w), lambda i, j: (0, i, 0)),
            pl.BlockSpec((1, tm, tn), lambda i, j: (1, i, j)),
            pl.BlockSpec((1, w), lambda i, j: (0, 0)),
            pl.BlockSpec((1, w, tn), lambda i, j: (0, 0, j)),
            pl.BlockSpec((1, w, tn), lambda i, j: (1, 0, j)),
            pl.BlockSpec((1, 1, tn), lambda i, j: (0, 0, j)),
            pl.BlockSpec((1, 1, tn), lambda i, j: (1, 0, j)),
        ],
        out_specs=pl.BlockSpec((tm, tn), lambda i, j: (i, j)),
        out_shape=jax.ShapeDtypeStruct((m, w), BF16),
        scratch_shapes=[pltpu.VMEM((tm, w), BF16)],
        compiler_params=_params("parallel", "arbitrary"),
        name=f"s5_glu_m{m}",
    )(y, uz, uz, d_skip, w_glu2, w_glu2, b_glu2, b_glu2)


def _residual_matmul_kernel(v_ref, x_ref, w_ref, o_ref):
    o_ref[...] = x_ref[...] + _dot(v_ref[...], w_ref[...])


def _residual_matmul(v, x, w, *, tm, name):
    m, k = v.shape
    n = w.shape[1]
    tm = min(tm, m)
    return pl.pallas_call(
        _residual_matmul_kernel,
        grid=(m // tm,),
        in_specs=[
            pl.BlockSpec((tm, k), lambda i: (i, 0)),
            pl.BlockSpec((tm, n), lambda i: (i, 0)),
            pl.BlockSpec((k, n), lambda i: (0, 0)),
        ],
        out_specs=pl.BlockSpec((tm, n), lambda i: (i, 0)),
        out_shape=jax.ShapeDtypeStruct((m, n), F32),
        compiler_params=_params("parallel"),
        name=name,
    )(v, x, w)


def _latent_kernel(x_ref, g_ref, wl_ref, wr_ref, ws_ref, gl_ref, cc_ref, ss_ref, lat_ref, kr_ref, kcat_ref):
    h = _rms(x_ref[...], g_ref[...]).astype(BF16)
    lat = _rms(_dot(h, wl_ref[...]), gl_ref[...])
    kr = _dot(h, wr_ref[...]) * cc_ref[...] + _dot(h, ws_ref[...]) * ss_ref[...]
    lat_ref[...] = lat
    kr_ref[...] = kr[:, :QK_ROPE]
    kcat_ref[:, :KV_LORA] = lat.astype(BF16)
    kcat_ref[:, KV_LORA:] = kr.astype(BF16)


def _latent(x, g, wl, wr, ws, gl, cc, ss, *, tm, name):
    m, d = x.shape
    tm = min(tm, m)
    nt = cc.shape[0] // tm

    def full(a):
        return pl.BlockSpec(a.shape, lambda i: (0,) * a.ndim)

    return pl.pallas_call(
        _latent_kernel,
        grid=(m // tm,),
        in_specs=[
            pl.BlockSpec((tm, d), lambda i: (i, 0)),
            full(g), full(wl), full(wr), full(ws), full(gl),
            pl.BlockSpec((tm, LANES), lambda i: (i % nt, 0)),
            pl.BlockSpec((tm, LANES), lambda i: (i % nt, 0)),
        ],
        out_specs=[
            pl.BlockSpec((tm, KV_LORA), lambda i: (i, 0)),
            pl.BlockSpec((tm, QK_ROPE), lambda i: (i, 0)),
            pl.BlockSpec((tm, QK_PAD), lambda i: (i, 0)),
        ],
        out_shape=[
            jax.ShapeDtypeStruct((m, KV_LORA), F32),
            jax.ShapeDtypeStruct((m, QK_ROPE), F32),
            jax.ShapeDtypeStruct((m, QK_PAD), BF16),
        ],
        compiler_params=_params("parallel"),
        name=name,
    )(x, g, wl, wr, ws, gl, cc, ss)


def _mla_front_kernel(x_ref, g_ref, wcq_ref, wg_ref, gq_ref, wqn_ref, wqr_ref, wqs_ref, wuk_ref,
                      cc_ref, ss_ref, q_ref, sg_ref):
    h = _rms(x_ref[...], g_ref[...]).astype(BF16)
    cq = _rms(_dot(h, wcq_ref[...]), gq_ref[...]).astype(BF16)
    gate = _dot(h, wg_ref[...])
    sg_ref[...] = gate * jax.nn.sigmoid(gate)
    qn = _dot(cq, wqn_ref[...]).astype(BF16)
    qr = _dot(cq, wqr_ref[...])
    qs = _dot(cq, wqs_ref[...])
    cc = cc_ref[...]
    ss = ss_ref[...]
    for hd in range(N_HEADS):
        sl = slice(hd * LANES, (hd + 1) * LANES)
        base = hd * QK_PAD
        q_ref[:, base:base + KV_LORA] = _dot(qn[:, sl], wuk_ref[hd]).astype(BF16)
        q_ref[:, base + KV_LORA:base + QK_PAD] = (qr[:, sl] * cc + qs[:, sl] * ss).astype(BF16)


def _mla_front(x, g, wcq, wg, gq, wqn, wqr, wqs, wuk, cc, ss, *, tm, name):
    m, d = x.shape
    tm = min(tm, m)
    nt = cc.shape[0] // tm

    def full(a):
        return pl.BlockSpec(a.shape, lambda i: (0,) * a.ndim)

    return pl.pallas_call(
        _mla_front_kernel,
        grid=(m // tm,),
        in_specs=[
            pl.BlockSpec((tm, d), lambda i: (i, 0)),
            full(g), full(wcq), full(wg), full(gq), full(wqn), full(wqr), full(wqs), full(wuk),
            pl.BlockSpec((tm, LANES), lambda i: (i % nt, 0)),
            pl.BlockSpec((tm, LANES), lambda i: (i % nt, 0)),
        ],
        out_specs=[
            pl.BlockSpec((tm, N_HEADS * QK_PAD), lambda i: (i, 0)),
            pl.BlockSpec((tm, N_HEADS * V_HEAD), lambda i: (i, 0)),
        ],
        out_shape=[
            jax.ShapeDtypeStruct((m, N_HEADS * QK_PAD), BF16),
            jax.ShapeDtypeStruct((m, N_HEADS * V_HEAD), F32),
        ],
        compiler_params=_params("parallel"),
        name=name,
    )(x, g, wcq, wg, gq, wqn, wqr, wqs, wuk, cc, ss)


def _attn_prompt_kernel(q_ref, k_ref, o_ref, m_s, l_s, acc_s, *, tq, tk):
    qi = pl.program_id(1)
    ki = pl.program_id(2)

    @pl.when(ki == 0)
    def _():
        m_s[...] = jnp.full(m_s.shape, NEG_INF, F32)
        l_s[...] = jnp.zeros(l_s.shape, F32)
        acc_s[...] = jnp.zeros(acc_s.shape, F32)

    @pl.when(ki <= qi)
    def _():
        k = k_ref[...]
        kl = k[:, :KV_LORA]
        row = qi * tq + lax.broadcasted_iota(jnp.int32, (tq, tk), 0)
        col = ki * tk + lax.broadcasted_iota(jnp.int32, (tq, tk), 1)
        mask = col <= row
        for hd in range(N_HEADS):
            s = _dot_nt(q_ref[:, hd * QK_PAD:(hd + 1) * QK_PAD], k)
            s = jnp.where(mask, s * SOFTMAX_SCALE, NEG_INF)
            m_prev = m_s[hd]
            m_new = jnp.maximum(m_prev, jnp.max(s, axis=-1, keepdims=True))
            alpha = jnp.exp(m_prev - m_new)
            p = jnp.exp(s - m_new)
            l_s[hd] = alpha * l_s[hd] + jnp.sum(p, axis=-1, keepdims=True)
            acc_s[hd] = alpha * acc_s[hd] + _dot(p.astype(BF16), kl)
            m_s[hd] = m_new

    @pl.when(ki == qi)
    def _():
        for hd in range(N_HEADS):
            o_ref[:, hd * KV_LORA:(hd + 1) * KV_LORA] = (acc_s[hd] / l_s[hd]).astype(BF16)


def _attn_prompt(q, kcat, *, bsz, seq, tq):
    nq = seq // tq
    return pl.pallas_call(
        functools.partial(_attn_prompt_kernel, tq=tq, tk=tq),
        grid=(bsz, nq, nq),
        in_specs=[
            pl.BlockSpec((tq, N_HEADS * QK_PAD), lambda b, i, j: (b * nq + i, 0)),
            pl.BlockSpec((tq, QK_PAD), lambda b, i, j: (b * nq + jnp.minimum(i, j), 0)),
        ],
        out_specs=pl.BlockSpec((tq, N_HEADS * KV_LORA), lambda b, i, j: (b * nq + i, 0)),
        out_shape=jax.ShapeDtypeStruct((bsz * seq, N_HEADS * KV_LORA), BF16),
        scratch_shapes=[
            pltpu.VMEM((N_HEADS, tq, 1), F32),
            pltpu.VMEM((N_HEADS, tq, 1), F32),
            pltpu.VMEM((N_HEADS, tq, KV_LORA), F32),
        ],
        compiler_params=_params("parallel", "parallel", "arbitrary"),
        name="attn_prompt",
    )(q, kcat)


def _attn_sample_kernel(pt_ref, q_ref, *refs, n_pg, dec_seq):
    lat_refs = refs[:n_pg]
    kr_refs = refs[n_pg:2 * n_pg]
    nl_ref, nk_ref, o_ref, m_s, l_s, acc_s = refs[2 * n_pg:]
    step = pl.program_id(1)

    @pl.when(step == 0)
    def _():
        m_s[...] = jnp.full(m_s.shape, NEG_INF, F32)
        l_s[...] = jnp.zeros(l_s.shape, F32)
        acc_s[...] = jnp.zeros(acc_s.shape, F32)

    q = q_ref[0]
    ql = q[:, :KV_LORA]
    qr = q[:, KV_LORA:KV_LORA + QK_ROPE]
    m = m_s[...]
    l = l_s[...]
    acc = acc_s[...]
    for i in range(n_pg):
        kl = lat_refs[i][0].astype(BF16)
        kk = kr_refs[i][0].astype(BF16)
        s = (_dot_nt(ql, kl) + _dot_nt(qr, kk)) * SOFTMAX_SCALE
        m_new = jnp.maximum(m, jnp.max(s, axis=-1, keepdims=True))
        alpha = jnp.exp(m - m_new)
        p = jnp.exp(s - m_new)
        l = alpha * l + jnp.sum(p, axis=-1, keepdims=True)
        acc = alpha * acc + _dot(p.astype(BF16), kl)
        m = m_new
    m_s[...] = m
    l_s[...] = l
    acc_s[...] = acc

    @pl.when(step == pl.num_programs(1) - 1)
    def _():
        nl = nl_ref[0].astype(BF16).astype(F32)
        nk = nk_ref[0].astype(BF16).astype(F32)
        qlf = ql.astype(F32)
        qrf = qr.astype(F32)
        tok = lax.broadcasted_iota(jnp.int32, (q.shape[0], 1), 0) >> (N_HEADS.bit_length() - 1)
        m2, l2, acc2 = m, l, acc
        for j in range(dec_seq):
            sj = (jnp.sum(qlf * nl[j:j + 1], axis=-1, keepdims=True)
                  + jnp.sum(qrf * nk[j:j + 1], axis=-1, keepdims=True)) * SOFTMAX_SCALE
            sj = jnp.where(tok >= j, sj, NEG_INF)
            m_new = jnp.maximum(m2, sj)
            alpha = jnp.exp(m2 - m_new)
            pj = jnp.exp(sj - m_new)
            l2 = alpha * l2 + pj
            acc2 = alpha * acc2 + pj.astype(BF16).astype(F32) * nl[j:j + 1]
            m2 = m_new
        o_ref[0] = (acc2 / l2).astype(BF16)


def _attn_sample(q, cache_latent, cache_krope, page_table, new_lat, new_kr, *, n_pg):
    n_seq, n_pages = page_table.shape
    rows = q.shape[1]
    dec_seq = new_lat.shape[1]
    steps = n_pages // n_pg
    pt = page_table.reshape(-1)

    def page_spec(width, i):
        return pl.BlockSpec((1, PAGE_SIZE, width),
                            lambda n, s, pt_ref: (pt_ref[n * n_pages + s * n_pg + i], 0, 0))

    def seq_spec(a, b):
        return pl.BlockSpec((1, a, b), lambda n, s, pt_ref: (n, 0, 0))

    grid_spec = pltpu.PrefetchScalarGridSpec(
        num_scalar_prefetch=1,
        grid=(n_seq, steps),
        in_specs=([seq_spec(rows, QK_PAD)]
                  + [page_spec(KV_LORA, i) for i in range(n_pg)]
                  + [page_spec(QK_ROPE, i) for i in range(n_pg)]
                  + [seq_spec(dec_seq, KV_LORA), seq_spec(dec_seq, QK_ROPE)]),
        out_specs=seq_spec(rows, KV_LORA),
        scratch_shapes=[
            pltpu.VMEM((rows, 1), F32),
            pltpu.VMEM((rows, 1), F32),
            pltpu.VMEM((rows, KV_LORA), F32),
        ],
    )
    return pl.pallas_call(
        functools.partial(_attn_sample_kernel, n_pg=n_pg, dec_seq=dec_seq),
        grid_spec=grid_spec,
        out_shape=jax.ShapeDtypeStruct((n_seq, rows, KV_LORA), BF16),
        compiler_params=_params("parallel", "arbitrary"),
        name="attn_sample",
    )(pt, q, *([cache_latent] * n_pg), *([cache_krope] * n_pg), new_lat, new_kr)


def _mla_back_kernel(o_ref, sg_ref, x_ref, wuv_ref, wo_ref, out_ref):
    parts = []
    for hd in range(N_HEADS):
        oh = _dot(o_ref[:, hd * KV_LORA:(hd + 1) * KV_LORA], wuv_ref[hd])
        parts.append((oh * sg_ref[:, hd * V_HEAD:(hd + 1) * V_HEAD]).astype(BF16))
    out_ref[...] = x_ref[...] + _dot(jnp.concatenate(parts, axis=1), wo_ref[...])


def _mla_back(o, sg, x, wuv, wo, *, tm, name):
    m, d = x.shape
    tm = min(tm, m)

    def full(a):
        return pl.BlockSpec(a.shape, lambda i: (0,) * a.ndim)

    def rows(a):
        return pl.BlockSpec((tm, a.shape[1]), lambda i: (i, 0))

    return pl.pallas_call(
        _mla_back_kernel,
        grid=(m // tm,),
        in_specs=[rows(o), rows(sg), rows(x), full(wuv), full(wo)],
        out_specs=pl.BlockSpec((tm, d), lambda i: (i, 0)),
        out_shape=jax.ShapeDtypeStruct((m, d), F32),
        compiler_params=_params("parallel"),
        name=name,
    )(o, sg, x, wuv, wo)


def _final_norm_kernel(x_ref, g_ref, o_ref):
    o_ref[...] = _rms(x_ref[...], g_ref[...])


def _final_norm(x, g, *, tm, name):
    m, d = x.shape
    tm = min(tm, m)
    return pl.pallas_call(
        _final_norm_kernel,
        grid=(m // tm,),
        in_specs=[pl.BlockSpec((tm, d), lambda i: (i, 0)), pl.BlockSpec((1, d), lambda i: (0, 0))],
        out_specs=pl.BlockSpec((tm, d), lambda i: (i, 0)),
        out_shape=jax.ShapeDtypeStruct((m, d), F32),
        compiler_params=_params("parallel"),
        name=name,
    )(x, g)


def _to_scan_layout(u, bsz, seq, chunk):
    k = seq // chunk
    g = u.shape[1] // GROUP_CH
    u5 = u.astype(BF16).reshape(bsz, k, chunk, g, GROUP_CH)
    return u5.transpose(3, 1, 0, 2, 4).reshape(g, k * bsz, chunk * GROUP_CH)


def _from_scan_layout(y, bsz, seq, chunk):
    k = seq // chunk
    g = y.shape[0]
    y5 = y.reshape(g, k, bsz, chunk, GROUP_CH)
    return y5.transpose(2, 1, 3, 0, 4).reshape(bsz * seq, g * GROUP_CH)


def _rope_tables(pos):
    half = QK_ROPE // 2
    inv = ROPE_THETA ** (-jnp.arange(half, dtype=F32) / half)
    ang = pos.astype(F32)[:, None] * inv[None, :]
    cos = jnp.cos(ang)
    sin = jnp.sin(ang)
    pad = jnp.zeros((pos.shape[0], LANES - QK_ROPE), F32)
    cc = jnp.concatenate([cos, cos, pad], axis=1)
    ss = jnp.concatenate([-sin, sin, pad], axis=1)
    return cc, ss


def _swap_halves(w):
    half = w.shape[-1] // 2
    return jnp.concatenate([w[..., half:], w[..., :half]], axis=-1)


def _pad_lanes(w):
    pad = [(0, 0)] * (w.ndim - 1) + [(0, LANES - w.shape[-1])]
    return jnp.pad(w, pad)


def kernel(x_prompt, x_sample, cache_latent, cache_krope, page_table, state_ssm_re, state_ssm_im, norm_a, w_in_a, a_re, a_im, log_dt, b_re, b_im, c_re, c_im, d_skip, w_glu, b_glu, w_out_a, norm_kv, w_dkv, norm_latent, w_uk, w_uv, norm_b, w_in_b, norm_q, w_uq, w_out_b, norm_f):
    bsz, seq, d = x_prompt.shape
    dbs, dseq, _ = x_sample.shape
    n_a = norm_a.shape[0]
    n_b = norm_b.shape[0]
    width = d_skip.shape[1]
    n_groups = width // GROUP_CH
    past_len = page_table.shape[1] * PAGE_SIZE

    xp = x_prompt.reshape(bsz * seq, d)
    xs = x_sample.reshape(dbs * dseq, d)

    hp_re, hp_im, hs_re, hs_im = [], [], [], []
    zeros_p = jnp.zeros((n_groups, bsz, STATE_DIM), F32)
    for i in range(n_a):
        w_in3 = w_in_a[i].astype(BF16).reshape(d, 2, width).transpose(1, 0, 2)
        w_glu2 = w_glu[i].astype(BF16).reshape(width, 2, width).transpose(1, 0, 2)
        b_glu2 = b_glu[i].reshape(2, 1, width)
        w_out = w_out_a[i].astype(BF16)
        g = norm_a[i][None, :]
        dsk = d_skip[i][None, :]
        ssm = (a_re[i], a_im[i], log_dt[i], b_re[i], b_im[i], c_re[i], c_im[i])

        uz = _norm_matmul(xp, g, w_in3, tm=512, tn=1024, name="s5_in_prompt")
        ops = _s5_prep(*ssm, chunk=PROMPT_CHUNK)
        u_l = _to_scan_layout(uz[0], bsz, seq, PROMPT_CHUNK)
        y_l, hr, hi = _s5_scan(u_l, ops, zeros_p, zeros_p, n_chunks=seq // PROMPT_CHUNK, bsz=bsz)
        y = _from_scan_layout(y_l, bsz, seq, PROMPT_CHUNK)
        v = _s5_glu(y, uz, dsk, w_glu2, b_glu2, tm=512, tn=512)
        xp = _residual_matmul(v, xp, w_out, tm=512, name="s5_out_prompt")
        hp_re.append(hr.transpose(1, 0, 2))
        hp_im.append(hi.transpose(1, 0, 2))

        uz = _norm_matmul(xs, g, w_in3, tm=512, tn=1024, name="s5_in_sample")
        ops = _s5_prep(*ssm, chunk=dseq)
        u_l = _to_scan_layout(uz[0], dbs, dseq, dseq)
        y_l, hr, hi = _s5_scan(u_l, ops, state_ssm_re[i].transpose(1, 0, 2), state_ssm_im[i].transpose(1, 0, 2),
                               n_chunks=1, bsz=dbs)
        y = _from_scan_layout(y_l, dbs, dseq, dseq)
        v = _s5_glu(y, uz, dsk, w_glu2, b_glu2, tm=512, tn=512)
        xs = _residual_matmul(v, xs, w_out, tm=512, name="s5_out_sample")
        hs_re.append(hr.transpose(1, 0, 2))
        hs_im.append(hi.transpose(1, 0, 2))

    cc_p, ss_p = _rope_tables(jnp.arange(seq, dtype=jnp.int32))
    pos_s = past_len + jnp.arange(dseq, dtype=jnp.int32)
    cc_s, ss_s = _rope_tables(jnp.tile(pos_s, dbs))
    w_lat = w_dkv[:, :KV_LORA].astype(BF16)
    w_kr = _pad_lanes(w_dkv[:, KV_LORA:]).astype(BF16)
    w_ks = _pad_lanes(_swap_halves(w_dkv[:, KV_LORA:])).astype(BF16)
    g_kv = norm_kv[None, :]
    g_lat = norm_latent[None, :]
    lat_p, kr_p, kcat_p = _latent(xp, g_kv, w_lat, w_kr, w_ks, g_lat, cc_p, ss_p, tm=512, name="latent_prompt")
    lat_s, kr_s, _ = _latent(xs, g_kv, w_lat, w_kr, w_ks, g_lat, cc_s, ss_s, tm=512, name="latent_sample")
    new_lat = lat_s.reshape(dbs, dseq, KV_LORA)
    new_kr = kr_s.reshape(dbs, dseq, QK_ROPE)

    w_ukt = w_uk.astype(BF16).reshape(KV_LORA, N_HEADS, QK_NOPE).transpose(1, 2, 0)
    w_uvh = w_uv.astype(BF16).reshape(KV_LORA, N_HEADS, V_HEAD).transpose(1, 0, 2)
    for j in range(n_b):
        g = norm_b[j][None, :]
        w_cq = w_in_b[j][:, :Q_LORA].astype(BF16)
        w_gate = w_in_b[j][:, Q_LORA:].astype(BF16)
        g_q = norm_q[j][None, :]
        wq3 = w_uq[j].reshape(Q_LORA, N_HEADS, QK_NOPE + QK_ROPE)
        w_qn = wq3[:, :, :QK_NOPE].reshape(Q_LORA, N_HEADS * QK_NOPE).astype(BF16)
        w_qr = _pad_lanes(wq3[:, :, QK_NOPE:]).reshape(Q_LORA, N_HEADS * LANES).astype(BF16)
        w_qs = _pad_lanes(_swap_halves(wq3[:, :, QK_NOPE:])).reshape(Q_LORA, N_HEADS * LANES).astype(BF16)
        w_o = w_out_b[j].astype(BF16)

        q, sg = _mla_front(xp, g, w_cq, w_gate, g_q, w_qn, w_qr, w_qs, w_ukt, cc_p, ss_p, tm=256,
                           name="mla_front_prompt")
        o = _attn_prompt(q, kcat_p, bsz=bsz, seq=seq, tq=256)
        xp = _mla_back(o, sg, xp, w_uvh, w_o, tm=512, name="mla_back_prompt")

        q, sg = _mla_front(xs, g, w_cq, w_gate, g_q, w_qn, w_qr, w_qs, w_ukt, cc_s, ss_s, tm=256,
                           name="mla_front_sample")
        o = _attn_sample(q.reshape(dbs, dseq * N_HEADS, QK_PAD), cache_latent, cache_krope, page_table,
                         new_lat, new_kr, n_pg=8)
        xs = _mla_back(o.reshape(dbs * dseq, N_HEADS * KV_LORA), sg, xs, w_uvh, w_o, tm=512,
                       name="mla_back_sample")

    y_prompt = _final_norm(xp, norm_f[None, :], tm=512, name="final_norm_prompt").reshape(bsz, seq, d)
    y_sample = _final_norm(xs, norm_f[None, :], tm=512, name="final_norm_sample").reshape(dbs, dseq, d)
    return (y_prompt, y_sample,
            lat_p.reshape(bsz, seq, KV_LORA), kr_p.reshape(bsz, seq, QK_ROPE),
            new_lat, new_kr,
            jnp.stack(hp_re), jnp.stack(hp_im), jnp.stack(hs_re), jnp.stack(hs_im))
```

```python
import functools
import math

import jax
import jax.numpy as jnp
from jax import lax
from jax.experimental import pallas as pl
from jax.experimental.pallas import tpu as pltpu

F32 = jnp.float32
BF16 = jnp.bfloat16

GROUP_CH = 16
STATE_DIM = 64
N_HEADS = 8
QK_NOPE = 128
QK_ROPE = 64
V_HEAD = 128
KV_LORA = 256
Q_LORA = 384
PAGE_SIZE = 128
ROPE_THETA = 10000.0
RMS_EPS = 1e-6
SOFTMAX_SCALE = 1.0 / math.sqrt(QK_NOPE + QK_ROPE)
NEG_INF = -1e30
LANES = 128
QK_PAD = KV_LORA + LANES
PROMPT_CHUNK = 16
ATTN_TQ = 256
ATTN_TK = 512
SAMPLE_PAGES_PER_STEP = 16
VMEM_LIMIT = 56 * 1024 * 1024


def _dot(a, b):
    return jnp.dot(a, b, preferred_element_type=F32)


def _dot_nt(a, b):
    return lax.dot_general(a, b, (((1,), (1,)), ((), ())), preferred_element_type=F32)


def _dot_f32(a, b):
    return jnp.dot(a, b, preferred_element_type=F32, precision=lax.Precision.HIGHEST)


def _rms(x, g):
    return x * lax.rsqrt(jnp.mean(x * x, axis=-1, keepdims=True) + RMS_EPS) * g


def _params(*sem):
    return pltpu.CompilerParams(dimension_semantics=sem, vmem_limit_bytes=VMEM_LIMIT)


def _norm_matmul_kernel(x_ref, g_ref, w_ref, o_ref, h_s):
    @pl.when((pl.program_id(1) == 0) & (pl.program_id(2) == 0))
    def _():
        h_s[...] = _rms(x_ref[...], g_ref[...]).astype(BF16)

    o_ref[0] = _dot(h_s[...], w_ref[0])


def _norm_matmul(x, g, w3, *, tm, tn, name):
    m, d = x.shape
    parts, _, n = w3.shape
    tm = min(tm, m)
    return pl.pallas_call(
        _norm_matmul_kernel,
        grid=(m // tm, parts, n // tn),
        in_specs=[
            pl.BlockSpec((tm, d), lambda i, p, j: (i, 0)),
            pl.BlockSpec((1, d), lambda i, p, j: (0, 0)),
            pl.BlockSpec((1, d, tn), lambda i, p, j: (p, 0, j)),
        ],
        out_specs=pl.BlockSpec((1, tm, tn), lambda i, p, j: (p, i, j)),
        out_shape=jax.ShapeDtypeStruct((parts, m, n), F32),
        scratch_shapes=[pltpu.VMEM((tm, d), BF16)],
        compiler_params=_params("parallel", "arbitrary", "arbitrary"),
        name=name,
    )(x, g, w3)


def _discretize(lam_re, lam_im, log_dt):
    dt = jnp.exp(log_dt)
    mag = jnp.exp(lam_re * dt)
    lb_re = mag * jnp.cos(lam_im * dt)
    lb_im = mag * jnp.sin(lam_im * dt)
    den = lam_re * lam_re + lam_im * lam_im
    nr = lb_re - 1.0
    f_re = (nr * lam_re + lb_im * lam_im) / den
    f_im = (lb_im * lam_re - nr * lam_im) / den
    return lb_re, lb_im, f_re, f_im


def _cpow(br, bi, e, nbits, shape):
    br = jnp.broadcast_to(br, shape)
    bi = jnp.broadcast_to(bi, shape)
    rr = jnp.ones(shape, F32)
    ri = jnp.zeros(shape, F32)
    for j in range(nbits):
        bit = ((e >> j) & 1) == 1
        nr = rr * br - ri * bi
        ni = rr * bi + ri * br
        rr = jnp.where(bit, nr, rr)
        ri = jnp.where(bit, ni, ri)
        if j + 1 < nbits:
            br, bi = br * br - bi * bi, 2.0 * br * bi
    return rr, ri


def _s5_prep_kernel(pcol_ref, prow_ref, btr_ref, bti_ref, ctr_ref, cti_ref,
                    toep_ref, wsr_ref, wsi_ref, wor_ref, woi_ref, at_ref, *, chunk):
    tc = chunk * GROUP_CH
    tl = ctr_ref.shape[2]
    nbits = chunk.bit_length()
    shift = GROUP_CH.bit_length() - 1

    pc = pcol_ref[0]
    lbr_c, lbi_c, _, _ = _discretize(pc[:, 0:1], pc[:, 1:2], pc[:, 2:3])
    pr = prow_ref[0]
    lbr_r, lbi_r, f_re, f_im = _discretize(pr[0:1], pr[1:2], pr[2:3])

    cr = ctr_ref[0]
    ci = cti_ref[0]
    shape = cr.shape
    lane_t = lax.broadcasted_iota(jnp.int32, shape, 1) >> shift
    p0r, p0i = _cpow(lbr_c, lbi_c, lane_t, nbits, shape)
    p1r = p0r * lbr_c - p0i * lbi_c
    p1i = p0r * lbi_c + p0i * lbr_c
    rr = cr * p0r - ci * p0i
    ri = cr * p0i + ci * p0r
    wor_ref[0] = (cr * p1r - ci * p1i)[:, :tc].astype(BF16)
    woi_ref[0] = (-(cr * p1i + ci * p1r))[:, :tc].astype(BF16)

    btr = btr_ref[0]
    bti = bti_ref[0]
    bbr = f_re * btr - f_im * bti
    bbi = f_re * bti + f_im * btr

    krow = _dot_f32(bbr[:GROUP_CH], rr) - _dot_f32(bbi[:GROUP_CH], ri)
    lane = lax.broadcasted_iota(jnp.int32, (GROUP_CH, tl), 1)
    for s in range(chunk):
        if s == 0:
            blk = krow
        else:
            blk = jnp.where(lane >= GROUP_CH * s, pltpu.roll(krow, GROUP_CH * s, 1), 0.0)
        toep_ref[0, GROUP_CH * s:GROUP_CH * (s + 1), :] = blk[:, :tc].astype(BF16)

    shape_s = btr.shape
    row_e = (chunk - 1) - (lax.broadcasted_iota(jnp.int32, shape_s, 0) >> shift)
    qr, qi = _cpow(lbr_r, lbi_r, row_e, nbits, shape_s)
    wsr_ref[0] = (bbr * qr - bbi * qi).astype(BF16)
    wsi_ref[0] = (bbr * qi + bbi * qr).astype(BF16)

    ar, ai = lbr_r, lbi_r
    for _ in range(chunk.bit_length() - 1):
        ar, ai = ar * ar - ai * ai, 2.0 * ar * ai
    at_ref[0, 0:1, :] = ar
    at_ref[0, 1:2, :] = ai


def _s5_prep(a_re, a_im, log_dt, b_re, b_im, c_re, c_im, *, chunk):
    g, p = a_re.shape
    tc = chunk * GROUP_CH
    tl = max(tc, LANES)
    ldt = jnp.broadcast_to(log_dt[:, None], (g, p))
    pcol = jnp.stack([a_re, a_im, ldt], axis=-1)
    prow = jnp.stack([a_re, a_im, ldt], axis=1)
    bt_re = jnp.tile(jnp.swapaxes(b_re, 1, 2), (1, chunk, 1))
    bt_im = jnp.tile(jnp.swapaxes(b_im, 1, 2), (1, chunk, 1))
    ct_re = jnp.tile(jnp.swapaxes(c_re, 1, 2), (1, 1, tl // GROUP_CH))
    ct_im = jnp.tile(jnp.swapaxes(c_im, 1, 2), (1, 1, tl // GROUP_CH))

    def spec(a, b):
        return pl.BlockSpec((1, a, b), lambda i: (i, 0, 0))

    return pl.pallas_call(
        functools.partial(_s5_prep_kernel, chunk=chunk),
        grid=(g,),
        in_specs=[spec(p, 3), spec(3, p), spec(tc, p), spec(tc, p), spec(p, tl), spec(p, tl)],
        out_specs=[spec(tc, tc), spec(tc, p), spec(tc, p), spec(p, tc), spec(p, tc), spec(2, p)],
        out_shape=[
            jax.ShapeDtypeStruct((g, tc, tc), BF16),
            jax.ShapeDtypeStruct((g, tc, p), BF16),
            jax.ShapeDtypeStruct((g, tc, p), BF16),
            jax.ShapeDtypeStruct((g, p, tc), BF16),
            jax.ShapeDtypeStruct((g, p, tc), BF16),
            jax.ShapeDtypeStruct((g, 2, p), F32),
        ],
        compiler_params=_params("parallel"),
        name=f"s5_prep_t{chunk}",
    )(pcol, prow, bt_re, bt_im, ct_re, ct_im)


def _s5_scan_kernel(u_ref, toep_ref, wsr_ref, wsi_ref, wor_ref, woi_ref, at_ref, h0r_ref, h0i_ref,
                    y_ref, hr_ref, hi_ref, sr_s, si_s, pr_s, pi_s, *, n_chunks, bsz):
    u = u_ref[0]
    y_intra = _dot(u, toep_ref[0])
    sr_s[...] = _dot(u, wsr_ref[0])
    si_s[...] = _dot(u, wsi_ref[0])
    ar = at_ref[0, 0:1, :]
    ai = at_ref[0, 1:2, :]

    def body(k, carry):
        hr, hi = carry
        rows = pl.ds(pl.multiple_of(k * bsz, bsz), bsz)
        pr_s[rows, :] = hr
        pi_s[rows, :] = hi
        return (ar * hr - ai * hi + sr_s[rows, :], ar * hi + ai * hr + si_s[rows, :])

    hr, hi = lax.fori_loop(0, n_chunks, body, (h0r_ref[0], h0i_ref[0]))
    hr_ref[0] = hr
    hi_ref[0] = hi
    y_ref[0] = (y_intra + _dot(pr_s[...].astype(BF16), wor_ref[0])
                + _dot(pi_s[...].astype(BF16), woi_ref[0]))


def _s5_scan(u_l, ops, h0r, h0i, *, n_chunks, bsz):
    toep, wsr, wsi, wor, woi, at = ops
    g, n, tc = u_l.shape
    p = STATE_DIM

    def spec(a, b):
        return pl.BlockSpec((1, a, b), lambda i: (i, 0, 0))

    return pl.pallas_call(
        functools.partial(_s5_scan_kernel, n_chunks=n_chunks, bsz=bsz),
        grid=(g,),
        in_specs=[spec(n, tc), spec(tc, tc), spec(tc, p), spec(tc, p), spec(p, tc), spec(p, tc),
                  spec(2, p), spec(bsz, p), spec(bsz, p)],
        out_specs=[spec(n, tc), spec(bsz, p), spec(bsz, p)],
        out_shape=[
            jax.ShapeDtypeStruct((g, n, tc), F32),
            jax.ShapeDtypeStruct((g, bsz, p), F32),
            jax.ShapeDtypeStruct((g, bsz, p), F32),
        ],
        scratch_shapes=[pltpu.VMEM((n, p), F32)] * 4,
        compiler_params=_params("parallel"),
        name=f"s5_scan_n{n}",
    )(u_l, toep, wsr, wsi, wor, woi, at, h0r, h0i)


def _s5_glu_kernel(y_ref, u_ref, z_ref, d_ref, wa_ref, wb_ref, ba_ref, bb_ref, v_ref, a_s):
    @pl.when(pl.program_id(1) == 0)
    def _():
        a_s[...] = jax.nn.gelu(y_ref[...] + d_ref[...] * u_ref[0]).astype(BF16)

    a = a_s[...]
    ga = _dot(a, wa_ref[0]) + ba_ref[0]
    gb = _dot(a, wb_ref[0]) + bb_ref[0]
    z = z_ref[0]
    v_ref[...] = (ga * jax.nn.sigmoid(gb) * (z * jax.nn.sigmoid(z))).astype(BF16)


def _s5_glu(y, uz, d_skip, w_glu2, b_glu2, *, tm, tn):
    m, w = y.shape
    tm = min(tm, m)
    return pl.pallas_call(
        _s5_glu_kernel,
        grid=(m // tm, w // tn),
        in_specs=[
            pl.BlockSpec((tm, w), lambda i, j: (i, 0)),
            pl.BlockSpec((1, tm, w), lambda i, j: (0, i, 0)),
            pl.BlockSpec((1, tm, tn), lambda i, j: (1, i, j)),
            pl.BlockSpec((1, w), lambda i, j: (0, 0)),
            pl.BlockSpec((1, w, tn), lambda i, j: (0, 0, j)),
            pl.BlockSpec((1, w, tn), lambda i, j: (1, 0, j)),
            pl.BlockSpec((1, 1, tn), lambda i, j: (0, 0, j)),
            pl.BlockSpec((1, 1, tn), lambda i, j: (1, 0, j)),
        ],
        out_specs=pl.BlockSpec((tm, tn), lambda i, j: (i, j)),
        out_shape=jax.ShapeDtypeStruct((m, w), BF16),
        scratch_shapes=[pltpu.VMEM((tm, w), BF16)],
        compiler_params=_params("parallel", "arbitrary"),
        name=f"s5_glu_m{m}",
    )(y, uz, uz, d_skip, w_glu2, w_glu2, b_glu2, b_glu2)


def _residual_matmul_kernel(v_ref, x_ref, w_ref, o_ref):
    o_ref[...] = x_ref[...] + _dot(v_ref[...], w_ref[...])


def _residual_matmul(v, x, w, *, tm, name):
    m, k = v.shape
    n = w.shape[1]
    tm = min(tm, m)
    return pl.pallas_call(
        _residual_matmul_kernel,
        grid=(m // tm,),
        in_specs=[
            pl.BlockSpec((tm, k), lambda i: (i, 0)),
            pl.BlockSpec((tm, n), lambda i: (i, 0)),
            pl.BlockSpec((k, n), lambda i: (0, 0)),
        ],
        out_specs=pl.BlockSpec((tm, n), lambda i: (i, 0)),
        out_shape=jax.ShapeDtypeStruct((m, n), F32),
        compiler_params=_params("parallel"),
        name=name,
    )(v, x, w)


def _latent_kernel(x_ref, g_ref, wl_ref, wr_ref, ws_ref, gl_ref, cc_ref, ss_ref, lat_ref, kr_ref, kcat_ref):
    h = _rms(x_ref[...], g_ref[...]).astype(BF16)
    lat = _rms(_dot(h, wl_ref[...]), gl_ref[...])
    kr = _dot(h, wr_ref[...]) * cc_ref[...] + _dot(h, ws_ref[...]) * ss_ref[...]
    lat_ref[...] = lat
    kr_ref[...] = kr[:, :QK_ROPE]
    kcat_ref[:, :KV_LORA] = lat.astype(BF16)
    kcat_ref[:, KV_LORA:] = kr.astype(BF16)


def _latent(x, g, wl, wr, ws, gl, cc, ss, *, tm, name):
    m, d = x.shape
    tm = min(tm, m)
    nt = cc.shape[0] // tm

    def full(a):
        return pl.BlockSpec(a.shape, lambda i: (0,) * a.ndim)

    return pl.pallas_call(
        _latent_kernel,
        grid=(m // tm,),
        in_specs=[
            pl.BlockSpec((tm, d), lambda i: (i, 0)),
            full(g), full(wl), full(wr), full(ws), full(gl),
            pl.BlockSpec((tm, LANES), lambda i: (i % nt, 0)),
            pl.BlockSpec((tm, LANES), lambda i: (i % nt, 0)),
        ],
        out_specs=[
            pl.BlockSpec((tm, KV_LORA), lambda i: (i, 0)),
            pl.BlockSpec((tm, QK_ROPE), lambda i: (i, 0)),
            pl.BlockSpec((tm, QK_PAD), lambda i: (i, 0)),
        ],
        out_shape=[
            jax.ShapeDtypeStruct((m, KV_LORA), F32),
            jax.ShapeDtypeStruct((m, QK_ROPE), F32),
            jax.ShapeDtypeStruct((m, QK_PAD), BF16),
        ],
        compiler_params=_params("parallel"),
        name=name,
    )(x, g, wl, wr, ws, gl, cc, ss)


def _mla_front_kernel(x_ref, g_ref, wcq_ref, wg_ref, gq_ref, wqn_ref, wqr_ref, wqs_ref, wuk_ref,
                      cc_ref, ss_ref, q_ref, sg_ref):
    h = _rms(x_ref[...], g_ref[...]).astype(BF16)
    cq = _rms(_dot(h, wcq_ref[...]), gq_ref[...]).astype(BF16)
    gate = _dot(h, wg_ref[...])
    sg_ref[...] = gate * jax.nn.sigmoid(gate)
    qn = _dot(cq, wqn_ref[...]).astype(BF16)
    qr = _dot(cq, wqr_ref[...])
    qs = _dot(cq, wqs_ref[...])
    cc = cc_ref[...]
    ss = ss_ref[...]
    for hd in range(N_HEADS):
        sl = slice(hd * LANES, (hd + 1) * LANES)
        q_ref[0, hd, :, :KV_LORA] = _dot(qn[:, sl], wuk_ref[hd]).astype(BF16)
        q_ref[0, hd, :, KV_LORA:] = (qr[:, sl] * cc + qs[:, sl] * ss).astype(BF16)


def _mla_front(x, g, wcq, wg, gq, wqn, wqr, wqs, wuk, cc, ss, *, tm, name):
    m, d = x.shape
    tm = min(tm, m)
    nt = cc.shape[0] // tm

    def full(a):
        return pl.BlockSpec(a.shape, lambda i: (0,) * a.ndim)

    return pl.pallas_call(
        _mla_front_kernel,
        grid=(m // tm,),
        in_specs=[
            pl.BlockSpec((tm, d), lambda i: (i, 0)),
            full(g), full(wcq), full(wg), full(gq), full(wqn), full(wqr), full(wqs), full(wuk),
            pl.BlockSpec((tm, LANES), lambda i: (i % nt, 0)),
            pl.BlockSpec((tm, LANES), lambda i: (i % nt, 0)),
        ],
        out_specs=[
            pl.BlockSpec((1, N_HEADS, tm, QK_PAD), lambda i: (i, 0, 0, 0)),
            pl.BlockSpec((tm, N_HEADS * V_HEAD), lambda i: (i, 0)),
        ],
        out_shape=[
            jax.ShapeDtypeStruct((m // tm, N_HEADS, tm, QK_PAD), BF16),
            jax.ShapeDtypeStruct((m, N_HEADS * V_HEAD), F32),
        ],
        compiler_params=_params("parallel"),
        name=name,
    )(x, g, wcq, wg, gq, wqn, wqr, wqs, wuk, cc, ss)


def _lane_tile(x, width):
    return x if width == LANES else jnp.concatenate([x] * (width // LANES), axis=1)


def _softmax_update(s, kl, m_ref, l_ref, acc_ref, rows):
    m_prev = m_ref[rows, :]
    m_new = jnp.maximum(m_prev, jnp.max(s, axis=1, keepdims=True))
    alpha = jnp.exp(m_prev - m_new)
    p = jnp.exp(s - _lane_tile(m_new, s.shape[1]))
    l_ref[rows, :] = alpha * l_ref[rows, :] + jnp.sum(p, axis=1, keepdims=True)
    acc_ref[rows, :] = _lane_tile(alpha, kl.shape[1]) * acc_ref[rows, :] + _dot(p.astype(BF16), kl)
    m_ref[rows, :] = m_new


def _attn_prompt_kernel(q_ref, k_ref, o_ref, m_s, l_s, acc_s, *, tq, tk, heads_per_chunk):
    qi = pl.program_id(1)
    ki = pl.program_id(2)
    diag = (qi * tq + (tq - 1)) // tk
    rc = heads_per_chunk * tq

    @pl.when(ki == 0)
    def _():
        m_s[...] = jnp.full(m_s.shape, NEG_INF, F32)
        l_s[...] = jnp.zeros(l_s.shape, F32)
        acc_s[...] = jnp.zeros(acc_s.shape, F32)

    def update(masked):
        k = k_ref[...]
        kl = k[:, :KV_LORA]
        for c in range(N_HEADS // heads_per_chunk):
            q = q_ref[0, c * heads_per_chunk:(c + 1) * heads_per_chunk].reshape(rc, QK_PAD)
            s = _dot_nt(q, k) * SOFTMAX_SCALE
            if masked:
                tok = qi * tq + (lax.broadcasted_iota(jnp.int32, (rc, tk), 0) & (tq - 1))
                col = ki * tk + lax.broadcasted_iota(jnp.int32, (rc, tk), 1)
                s = jnp.where(col <= tok, s, NEG_INF)
            _softmax_update(s, kl, m_s, l_s, acc_s, pl.ds(c * rc, rc))

    @pl.when(ki < diag)
    def _():
        update(False)

    @pl.when(ki == diag)
    def _():
        update(True)
        for hd in range(N_HEADS):
            rows = pl.ds(hd * tq, tq)
            o_ref[0, hd] = (acc_s[rows, :] / _lane_tile(l_s[rows, :], KV_LORA)).astype(BF16)


def _attn_prompt(q, kcat, *, bsz, seq, tq, tk):
    assert tq & (tq - 1) == 0 and seq % tq == 0 and seq % tk == 0
    nq = seq // tq
    nk = seq // tk
    rows = N_HEADS * tq
    return pl.pallas_call(
        functools.partial(_attn_prompt_kernel, tq=tq, tk=tk, heads_per_chunk=2),
        grid=(bsz, nq, nk),
        in_specs=[
            pl.BlockSpec((1, N_HEADS, tq, QK_PAD), lambda b, i, j: (b * nq + i, 0, 0, 0)),
            pl.BlockSpec((tk, QK_PAD), lambda b, i, j: (b * nk + jnp.minimum(j, (i * tq + (tq - 1)) // tk), 0)),
        ],
        out_specs=pl.BlockSpec((1, N_HEADS, tq, KV_LORA), lambda b, i, j: (b * nq + i, 0, 0, 0)),
        out_shape=jax.ShapeDtypeStruct((bsz * nq, N_HEADS, tq, KV_LORA), BF16),
        scratch_shapes=[
            pltpu.VMEM((rows, LANES), F32),
            pltpu.VMEM((rows, LANES), F32),
            pltpu.VMEM((rows, KV_LORA), F32),
        ],
        compiler_params=_params("parallel", "parallel", "arbitrary"),
        name="attn_prompt",
    )(q, kcat)


def _attn_sample_kernel(pt_ref, q_ref, *refs, n_pg, dec_seq):
    lat_refs = refs[:n_pg]
    krt_refs = refs[n_pg:2 * n_pg]
    nl_ref, nk_ref, o_ref, kl_s, kr_s, m_s, l_s, acc_s = refs[2 * n_pg:]
    step = pl.program_id(1)

    @pl.when(step == 0)
    def _():
        m_s[...] = jnp.full(m_s.shape, NEG_INF, F32)
        l_s[...] = jnp.zeros(l_s.shape, F32)
        acc_s[...] = jnp.zeros(acc_s.shape, F32)

    for i in range(n_pg):
        kl_s[i * PAGE_SIZE:(i + 1) * PAGE_SIZE, :] = lat_refs[i][0].astype(BF16)
        kr_s[:, i * PAGE_SIZE:(i + 1) * PAGE_SIZE] = krt_refs[i][0].astype(BF16)

    q = q_ref[0]
    ql = q[:, :KV_LORA]
    qr = q[:, KV_LORA:KV_LORA + QK_ROPE]
    kl = kl_s[...]
    s = (_dot_nt(ql, kl) + _dot(qr, kr_s[...])) * SOFTMAX_SCALE
    _softmax_update(s, kl, m_s, l_s, acc_s, slice(None))

    @pl.when(step == pl.num_programs(1) - 1)
    def _():
        nl = nl_ref[0].astype(BF16).astype(F32)
        nk = nk_ref[0].astype(BF16).astype(F32)
        qlf = ql.astype(F32)
        qrf = qr.astype(F32)
        tok = lax.broadcasted_iota(jnp.int32, (q.shape[0], 1), 0) & (dec_seq - 1)
        m = m_s[...]
        l = l_s[...]
        acc = acc_s[...]
        for j in range(dec_seq):
            sj = (jnp.sum(qlf * nl[j:j + 1], axis=-1, keepdims=True)
                  + jnp.sum(qrf * nk[j:j + 1], axis=-1, keepdims=True)) * SOFTMAX_SCALE
            sj = jnp.where(tok >= j, sj, NEG_INF)
            m_new = jnp.maximum(m, sj)
            alpha = jnp.exp(m - m_new)
            pj = jnp.exp(sj - m_new)
            l = alpha * l + pj
            pv = _lane_tile(pj, KV_LORA).astype(BF16).astype(F32) * nl[j:j + 1]
            acc = _lane_tile(alpha, KV_LORA) * acc + pv
            m = m_new
        o_ref[0] = (acc / _lane_tile(l, KV_LORA)).astype(BF16)


def _attn_sample(q, cache_latent, cache_krope_t, page_table, new_lat, new_kr, *, n_pg):
    n_seq, n_pages = page_table.shape
    rows = q.shape[1]
    dec_seq = new_lat.shape[1]
    assert dec_seq & (dec_seq - 1) == 0 and n_pages % n_pg == 0
    steps = n_pages // n_pg
    pt = page_table.reshape(-1)

    def page_spec(a, b, i):
        return pl.BlockSpec((1, a, b), lambda n, s, pt_ref: (pt_ref[n * n_pages + s * n_pg + i], 0, 0))

    def seq_spec(a, b):
        return pl.BlockSpec((1, a, b), lambda n, s, pt_ref: (n, 0, 0))

    grid_spec = pltpu.PrefetchScalarGridSpec(
        num_scalar_prefetch=1,
        grid=(n_seq, steps),
        in_specs=([seq_spec(rows, QK_PAD)]
                  + [page_spec(PAGE_SIZE, KV_LORA, i) for i in range(n_pg)]
                  + [page_spec(QK_ROPE, PAGE_SIZE, i) for i in range(n_pg)]
                  + [seq_spec(dec_seq, KV_LORA), seq_spec(dec_seq, QK_ROPE)]),
        out_specs=seq_spec(rows, KV_LORA),
        scratch_shapes=[
            pltpu.VMEM((n_pg * PAGE_SIZE, KV_LORA), BF16),
            pltpu.VMEM((QK_ROPE, n_pg * PAGE_SIZE), BF16),
            pltpu.VMEM((rows, LANES), F32),
            pltpu.VMEM((rows, LANES), F32),
            pltpu.VMEM((rows, KV_LORA), F32),
        ],
    )
    return pl.pallas_call(
        functools.partial(_attn_sample_kernel, n_pg=n_pg, dec_seq=dec_seq),
        grid_spec=grid_spec,
        out_shape=jax.ShapeDtypeStruct((n_seq, rows, KV_LORA), BF16),
        compiler_params=_params("parallel", "arbitrary"),
        name="attn_sample",
    )(pt, q, *([cache_latent] * n_pg), *([cache_krope_t] * n_pg), new_lat, new_kr)


def _mla_back_kernel(o_ref, sg_ref, x_ref, wuv_ref, wo_ref, out_ref):
    parts = []
    for hd in range(N_HEADS):
        oh = _dot(o_ref[0, hd], wuv_ref[hd])
        parts.append((oh * sg_ref[:, hd * V_HEAD:(hd + 1) * V_HEAD]).astype(BF16))
    out_ref[...] = x_ref[...] + _dot(jnp.concatenate(parts, axis=1), wo_ref[...])


def _mla_back(o, sg, x, wuv, wo, *, name):
    m, d = x.shape
    tm = o.shape[2]

    def full(a):
        return pl.BlockSpec(a.shape, lambda i: (0,) * a.ndim)

    def rows(a):
        return pl.BlockSpec((tm, a.shape[1]), lambda i: (i, 0))

    return pl.pallas_call(
        _mla_back_kernel,
        grid=(m // tm,),
        in_specs=[pl.BlockSpec((1,) + o.shape[1:], lambda i: (i, 0, 0, 0)), rows(sg), rows(x), full(wuv), full(wo)],
        out_specs=pl.BlockSpec((tm, d), lambda i: (i, 0)),
        out_shape=jax.ShapeDtypeStruct((m, d), F32),
        compiler_params=_params("parallel"),
        name=name,
    )(o, sg, x, wuv, wo)


def _final_norm_kernel(x_ref, g_ref, o_ref):
    o_ref[...] = _rms(x_ref[...], g_ref[...])


def _final_norm(x, g, *, tm, name):
    m, d = x.shape
    tm = min(tm, m)
    return pl.pallas_call(
        _final_norm_kernel,
        grid=(m // tm,),
        in_specs=[pl.BlockSpec((tm, d), lambda i: (i, 0)), pl.BlockSpec((1, d), lambda i: (0, 0))],
        out_specs=pl.BlockSpec((tm, d), lambda i: (i, 0)),
        out_shape=jax.ShapeDtypeStruct((m, d), F32),
        compiler_params=_params("parallel"),
        name=name,
    )(x, g)


def _to_scan_layout(u, bsz, seq, chunk):
    k = seq // chunk
    g = u.shape[1] // GROUP_CH
    u5 = u.astype(BF16).reshape(bsz, k, chunk, g, GROUP_CH)
    return u5.transpose(3, 1, 0, 2, 4).reshape(g, k * bsz, chunk * GROUP_CH)


def _from_scan_layout(y, bsz, seq, chunk):
    k = seq // chunk
    g = y.shape[0]
    y5 = y.reshape(g, k, bsz, chunk, GROUP_CH)
    return y5.transpose(2, 1, 3, 0, 4).reshape(bsz * seq, g * GROUP_CH)


def _rope_tables(pos):
    half = QK_ROPE // 2
    inv = ROPE_THETA ** (-jnp.arange(half, dtype=F32) / half)
    ang = pos.astype(F32)[:, None] * inv[None, :]
    cos = jnp.cos(ang)
    sin = jnp.sin(ang)
    pad = jnp.zeros((pos.shape[0], LANES - QK_ROPE), F32)
    cc = jnp.concatenate([cos, cos, pad], axis=1)
    ss = jnp.concatenate([-sin, sin, pad], axis=1)
    return cc, ss


def _swap_halves(w):
    half = w.shape[-1] // 2
    return jnp.concatenate([w[..., half:], w[..., :half]], axis=-1)


def _pad_lanes(w):
    pad = [(0, 0)] * (w.ndim - 1) + [(0, LANES - w.shape[-1])]
    return jnp.pad(w, pad)


def kernel(x_prompt, x_sample, cache_latent, cache_krope, page_table, state_ssm_re, state_ssm_im, norm_a, w_in_a, a_re, a_im, log_dt, b_re, b_im, c_re, c_im, d_skip, w_glu, b_glu, w_out_a, norm_kv, w_dkv, norm_latent, w_uk, w_uv, norm_b, w_in_b, norm_q, w_uq, w_out_b, norm_f):
    bsz, seq, d = x_prompt.shape
    dbs, dseq, _ = x_sample.shape
    n_a = norm_a.shape[0]
    n_b = norm_b.shape[0]
    width = d_skip.shape[1]
    n_groups = width // GROUP_CH
    past_len = page_table.shape[1] * PAGE_SIZE

    xp = x_prompt.reshape(bsz * seq, d)
    xs = x_sample.reshape(dbs * dseq, d)

    hp_re, hp_im, hs_re, hs_im = [], [], [], []
    zeros_p = jnp.zeros((n_groups, bsz, STATE_DIM), F32)
    for i in range(n_a):
        w_in3 = w_in_a[i].astype(BF16).reshape(d, 2, width).transpose(1, 0, 2)
        w_glu2 = w_glu[i].astype(BF16).reshape(width, 2, width).transpose(1, 0, 2)
        b_glu2 = b_glu[i].reshape(2, 1, width)
        w_out = w_out_a[i].astype(BF16)
        g = norm_a[i][None, :]
        dsk = d_skip[i][None, :]
        ssm = (a_re[i], a_im[i], log_dt[i], b_re[i], b_im[i], c_re[i], c_im[i])

        uz = _norm_matmul(xp, g, w_in3, tm=512, tn=1024, name="s5_in_prompt")
        ops = _s5_prep(*ssm, chunk=PROMPT_CHUNK)
        u_l = _to_scan_layout(uz[0], bsz, seq, PROMPT_CHUNK)
        y_l, hr, hi = _s5_scan(u_l, ops, zeros_p, zeros_p, n_chunks=seq // PROMPT_CHUNK, bsz=bsz)
        y = _from_scan_layout(y_l, bsz, seq, PROMPT_CHUNK)
        v = _s5_glu(y, uz, dsk, w_glu2, b_glu2, tm=512, tn=512)
        xp = _residual_matmul(v, xp, w_out, tm=512, name="s5_out_prompt")
        hp_re.append(hr.transpose(1, 0, 2))
        hp_im.append(hi.transpose(1, 0, 2))

        uz = _norm_matmul(xs, g, w_in3, tm=512, tn=1024, name="s5_in_sample")
        ops = _s5_prep(*ssm, chunk=dseq)
        u_l = _to_scan_layout(uz[0], dbs, dseq, dseq)
        y_l, hr, hi = _s5_scan(u_l, ops, state_ssm_re[i].transpose(1, 0, 2), state_ssm_im[i].transpose(1, 0, 2),
                               n_chunks=1, bsz=dbs)
        y = _from_scan_layout(y_l, dbs, dseq, dseq)
        v = _s5_glu(y, uz, dsk, w_glu2, b_glu2, tm=512, tn=512)
        xs = _residual_matmul(v, xs, w_out, tm=512, name="s5_out_sample")
        hs_re.append(hr.transpose(1, 0, 2))
        hs_im.append(hi.transpose(1, 0, 2))

    cc_p, ss_p = _rope_tables(jnp.arange(seq, dtype=jnp.int32))
    pos_s = past_len + jnp.arange(dseq, dtype=jnp.int32)
    cc_s, ss_s = _rope_tables(jnp.tile(pos_s, dbs))
    w_lat = w_dkv[:, :KV_LORA].astype(BF16)
    w_kr = _pad_lanes(w_dkv[:, KV_LORA:]).astype(BF16)
    w_ks = _pad_lanes(_swap_halves(w_dkv[:, KV_LORA:])).astype(BF16)
    g_kv = norm_kv[None, :]
    g_lat = norm_latent[None, :]
    lat_p, kr_p, kcat_p = _latent(xp, g_kv, w_lat, w_kr, w_ks, g_lat, cc_p, ss_p, tm=512, name="latent_prompt")
    lat_s, kr_s, _ = _latent(xs, g_kv, w_lat, w_kr, w_ks, g_lat, cc_s, ss_s, tm=512, name="latent_sample")
    new_lat = lat_s.reshape(dbs, dseq, KV_LORA)
    new_kr = kr_s.reshape(dbs, dseq, QK_ROPE)

    cache_krope_t = jnp.swapaxes(cache_krope, 1, 2)
    w_ukt = w_uk.astype(BF16).reshape(KV_LORA, N_HEADS, QK_NOPE).transpose(1, 2, 0)
    w_uvh = w_uv.astype(BF16).reshape(KV_LORA, N_HEADS, V_HEAD).transpose(1, 0, 2)
    for j in range(n_b):
        g = norm_b[j][None, :]
        w_cq = w_in_b[j][:, :Q_LORA].astype(BF16)
        w_gate = w_in_b[j][:, Q_LORA:].astype(BF16)
        g_q = norm_q[j][None, :]
        wq3 = w_uq[j].reshape(Q_LORA, N_HEADS, QK_NOPE + QK_ROPE)
        w_qn = wq3[:, :, :QK_NOPE].reshape(Q_LORA, N_HEADS * QK_NOPE).astype(BF16)
        w_qr = _pad_lanes(wq3[:, :, QK_NOPE:]).reshape(Q_LORA, N_HEADS * LANES).astype(BF16)
        w_qs = _pad_lanes(_swap_halves(wq3[:, :, QK_NOPE:])).reshape(Q_LORA, N_HEADS * LANES).astype(BF16)
        w_o = w_out_b[j].astype(BF16)

        q, sg = _mla_front(xp, g, w_cq, w_gate, g_q, w_qn, w_qr, w_qs, w_ukt, cc_p, ss_p, tm=ATTN_TQ,
                           name="mla_front_prompt")
        o = _attn_prompt(q, kcat_p, bsz=bsz, seq=seq, tq=ATTN_TQ, tk=ATTN_TK)
        xp = _mla_back(o, sg, xp, w_uvh, w_o, name="mla_back_prompt")

        q, sg = _mla_front(xs, g, w_cq, w_gate, g_q, w_qn, w_qr, w_qs, w_ukt, cc_s, ss_s, tm=ATTN_TQ,
                           name="mla_front_sample")
        spt = ATTN_TQ // dseq
        q_seq = q.reshape(-1, N_HEADS, spt, dseq, QK_PAD).transpose(0, 2, 1, 3, 4)
        o = _attn_sample(q_seq.reshape(dbs, N_HEADS * dseq, QK_PAD), cache_latent, cache_krope_t, page_table,
                         new_lat, new_kr, n_pg=SAMPLE_PAGES_PER_STEP)
        o = o.reshape(-1, spt, N_HEADS, dseq, KV_LORA).transpose(0, 2, 1, 3, 4)
        xs = _mla_back(o.reshape(-1, N_HEADS, ATTN_TQ, KV_LORA), sg, xs, w_uvh, w_o, name="mla_back_sample")

    y_prompt = _final_norm(xp, norm_f[None, :], tm=512, name="final_norm_prompt").reshape(bsz, seq, d)
    y_sample = _final_norm(xs, norm_f[None, :], tm=512, name="final_norm_sample").reshape(dbs, dseq, d)
    return (y_prompt, y_sample,
            lat_p.reshape(bsz, seq, KV_LORA), kr_p.reshape(bsz, seq, QK_ROPE),
            new_lat, new_kr,
            jnp.stack(hp_re), jnp.stack(hp_im), jnp.stack(hs_re), jnp.stack(hs_im))
```

```python
import functools
import math

import jax
import jax.numpy as jnp
from jax import lax
from jax.experimental import pallas as pl
from jax.experimental.pallas import tpu as pltpu

F32 = jnp.float32
BF16 = jnp.bfloat16

GROUP_CH = 16
STATE_DIM = 64
N_HEADS = 8
QK_NOPE = 128
QK_ROPE = 64
V_HEAD = 128
KV_LORA = 256
Q_LORA = 384
PAGE_SIZE = 128
ROPE_THETA = 10000.0
RMS_EPS = 1e-6
SOFTMAX_SCALE = 1.0 / math.sqrt(QK_NOPE + QK_ROPE)
NEG_INF = -1e30
LANES = 128
QK_PAD = KV_LORA + LANES
PROMPT_CHUNK = 16
ATTN_TQ = 256
ATTN_TK = 512
SAMPLE_PAGES_PER_STEP = 16
SCAN_ROWS = 512
VMEM_LIMIT = 56 * 1024 * 1024


def _dot(a, b):
    return jnp.dot(a, b, preferred_element_type=F32)


def _dot_nt(a, b):
    return lax.dot_general(a, b, (((1,), (1,)), ((), ())), preferred_element_type=F32)


def _dot_f32(a, b):
    return jnp.dot(a, b, preferred_element_type=F32, precision=lax.Precision.HIGHEST)


def _rms(x, g):
    return x * lax.rsqrt(jnp.mean(x * x, axis=-1, keepdims=True) + RMS_EPS) * g


def _params(*sem):
    return pltpu.CompilerParams(dimension_semantics=sem, vmem_limit_bytes=VMEM_LIMIT)


def _norm_matmul_kernel(x_ref, g_ref, w_ref, o_ref, *, tn):
    h = _rms(x_ref[...], g_ref[...]).astype(BF16)
    parts, _, n = w_ref.shape
    for p in range(parts):
        for j in range(n // tn):
            cols = slice(j * tn, (j + 1) * tn)
            o_ref[p, :, cols] = _dot(h, w_ref[p, :, cols])


def _norm_matmul(x, g, w3, *, tm, tn, name):
    m, d = x.shape
    parts, _, n = w3.shape
    tm = min(tm, m)
    return pl.pallas_call(
        functools.partial(_norm_matmul_kernel, tn=tn),
        grid=(m // tm,),
        in_specs=[
            pl.BlockSpec((tm, d), lambda i: (i, 0)),
            pl.BlockSpec((1, d), lambda i: (0, 0)),
            pl.BlockSpec((parts, d, n), lambda i: (0, 0, 0), pipeline_mode=pl.Buffered(1)),
        ],
        out_specs=pl.BlockSpec((parts, tm, n), lambda i: (0, i, 0)),
        out_shape=jax.ShapeDtypeStruct((parts, m, n), F32),
        compiler_params=_params("parallel"),
        name=name,
    )(x, g, w3)


def _discretize(lam_re, lam_im, log_dt):
    dt = jnp.exp(log_dt)
    mag = jnp.exp(lam_re * dt)
    lb_re = mag * jnp.cos(lam_im * dt)
    lb_im = mag * jnp.sin(lam_im * dt)
    den = lam_re * lam_re + lam_im * lam_im
    nr = lb_re - 1.0
    f_re = (nr * lam_re + lb_im * lam_im) / den
    f_im = (lb_im * lam_re - nr * lam_im) / den
    return lb_re, lb_im, f_re, f_im


def _cpow(br, bi, e, nbits, shape):
    br = jnp.broadcast_to(br, shape)
    bi = jnp.broadcast_to(bi, shape)
    rr = jnp.ones(shape, F32)
    ri = jnp.zeros(shape, F32)
    for j in range(nbits):
        bit = ((e >> j) & 1) == 1
        nr = rr * br - ri * bi
        ni = rr * bi + ri * br
        rr = jnp.where(bit, nr, rr)
        ri = jnp.where(bit, ni, ri)
        if j + 1 < nbits:
            br, bi = br * br - bi * bi, 2.0 * br * bi
    return rr, ri


def _s5_prep_kernel(pcol_ref, prow_ref, btr_ref, bti_ref, ctr_ref, cti_ref,
                    toep_ref, wsr_ref, wsi_ref, wor_ref, woi_ref, at_ref, *, chunk):
    tc = chunk * GROUP_CH
    tl = ctr_ref.shape[2]
    nbits = chunk.bit_length()
    shift = GROUP_CH.bit_length() - 1

    pc = pcol_ref[0]
    lbr_c, lbi_c, _, _ = _discretize(pc[:, 0:1], pc[:, 1:2], pc[:, 2:3])
    pr = prow_ref[0]
    lbr_r, lbi_r, f_re, f_im = _discretize(pr[0:1], pr[1:2], pr[2:3])

    cr = ctr_ref[0]
    ci = cti_ref[0]
    shape = cr.shape
    lane_t = lax.broadcasted_iota(jnp.int32, shape, 1) >> shift
    p0r, p0i = _cpow(lbr_c, lbi_c, lane_t, nbits, shape)
    p1r = p0r * lbr_c - p0i * lbi_c
    p1i = p0r * lbi_c + p0i * lbr_c
    rr = cr * p0r - ci * p0i
    ri = cr * p0i + ci * p0r
    wor_ref[0] = (cr * p1r - ci * p1i)[:, :tc].astype(BF16)
    woi_ref[0] = (-(cr * p1i + ci * p1r))[:, :tc].astype(BF16)

    btr = btr_ref[0]
    bti = bti_ref[0]
    bbr = f_re * btr - f_im * bti
    bbi = f_re * bti + f_im * btr

    krow = _dot_f32(bbr[:GROUP_CH], rr) - _dot_f32(bbi[:GROUP_CH], ri)
    lane = lax.broadcasted_iota(jnp.int32, (GROUP_CH, tl), 1)
    for s in range(chunk):
        if s == 0:
            blk = krow
        else:
            blk = jnp.where(lane >= GROUP_CH * s, pltpu.roll(krow, GROUP_CH * s, 1), 0.0)
        toep_ref[0, GROUP_CH * s:GROUP_CH * (s + 1), :] = blk[:, :tc].astype(BF16)

    shape_s = btr.shape
    row_e = (chunk - 1) - (lax.broadcasted_iota(jnp.int32, shape_s, 0) >> shift)
    qr, qi = _cpow(lbr_r, lbi_r, row_e, nbits, shape_s)
    wsr_ref[0] = (bbr * qr - bbi * qi).astype(BF16)
    wsi_ref[0] = (bbr * qi + bbi * qr).astype(BF16)

    ar, ai = lbr_r, lbi_r
    for _ in range(chunk.bit_length() - 1):
        ar, ai = ar * ar - ai * ai, 2.0 * ar * ai
    at_ref[0, 0:1, :] = ar
    at_ref[0, 1:2, :] = ai


def _s5_prep(a_re, a_im, log_dt, b_re, b_im, c_re, c_im, *, chunk):
    g, p = a_re.shape
    tc = chunk * GROUP_CH
    tl = max(tc, LANES)
    ldt = jnp.broadcast_to(log_dt[:, None], (g, p))
    pcol = jnp.stack([a_re, a_im, ldt], axis=-1)
    prow = jnp.stack([a_re, a_im, ldt], axis=1)
    bt_re = jnp.tile(jnp.swapaxes(b_re, 1, 2), (1, chunk, 1))
    bt_im = jnp.tile(jnp.swapaxes(b_im, 1, 2), (1, chunk, 1))
    ct_re = jnp.tile(jnp.swapaxes(c_re, 1, 2), (1, 1, tl // GROUP_CH))
    ct_im = jnp.tile(jnp.swapaxes(c_im, 1, 2), (1, 1, tl // GROUP_CH))

    def spec(a, b):
        return pl.BlockSpec((1, a, b), lambda i: (i, 0, 0))

    return pl.pallas_call(
        functools.partial(_s5_prep_kernel, chunk=chunk),
        grid=(g,),
        in_specs=[spec(p, 3), spec(3, p), spec(tc, p), spec(tc, p), spec(p, tl), spec(p, tl)],
        out_specs=[spec(tc, tc), spec(tc, p), spec(tc, p), spec(p, tc), spec(p, tc), spec(2, p)],
        out_shape=[
            jax.ShapeDtypeStruct((g, tc, tc), BF16),
            jax.ShapeDtypeStruct((g, tc, p), BF16),
            jax.ShapeDtypeStruct((g, tc, p), BF16),
            jax.ShapeDtypeStruct((g, p, tc), BF16),
            jax.ShapeDtypeStruct((g, p, tc), BF16),
            jax.ShapeDtypeStruct((g, 2, p), F32),
        ],
        compiler_params=_params("parallel"),
        name=f"s5_prep_t{chunk}",
    )(pcol, prow, bt_re, bt_im, ct_re, ct_im)


def _s5_scan_kernel(u_ref, toep_ref, wsr_ref, wsi_ref, wor_ref, woi_ref, at_ref, h0r_ref, h0i_ref,
                    y_ref, hr_ref, hi_ref, sr_s, si_s, pr_s, pi_s, *, n_chunks, bsz):
    u = u_ref[0]
    y_intra = _dot(u, toep_ref[0])
    sr_s[...] = _dot(u, wsr_ref[0])
    si_s[...] = _dot(u, wsi_ref[0])
    ar = at_ref[0, 0:1, :]
    ai = at_ref[0, 1:2, :]

    def body(k, carry):
        hr, hi = carry
        rows = pl.ds(pl.multiple_of(k * bsz, bsz), bsz)
        pr_s[rows, :] = hr
        pi_s[rows, :] = hi
        return (ar * hr - ai * hi + sr_s[rows, :], ar * hi + ai * hr + si_s[rows, :])

    hr, hi = lax.fori_loop(0, n_chunks, body, (h0r_ref[0], h0i_ref[0]))
    hr_ref[0] = hr
    hi_ref[0] = hi
    y_ref[0] = (y_intra + _dot(pr_s[...].astype(BF16), wor_ref[0])
                + _dot(pi_s[...].astype(BF16), woi_ref[0]))


def _s5_scan(u_l, ops, h0r, h0i, *, n_chunks, bsz):
    toep, wsr, wsi, wor, woi, at = ops
    g, n, tc = u_l.shape
    p = STATE_DIM

    def spec(a, b):
        return pl.BlockSpec((1, a, b), lambda i: (i, 0, 0))

    return pl.pallas_call(
        functools.partial(_s5_scan_kernel, n_chunks=n_chunks, bsz=bsz),
        grid=(g,),
        in_specs=[spec(n, tc), spec(tc, tc), spec(tc, p), spec(tc, p), spec(p, tc), spec(p, tc),
                  spec(2, p), spec(bsz, p), spec(bsz, p)],
        out_specs=[spec(n, tc), spec(bsz, p), spec(bsz, p)],
        out_shape=[
            jax.ShapeDtypeStruct((g, n, tc), F32),
            jax.ShapeDtypeStruct((g, bsz, p), F32),
            jax.ShapeDtypeStruct((g, bsz, p), F32),
        ],
        scratch_shapes=[pltpu.VMEM((n, p), F32)] * 4,
        compiler_params=_params("parallel"),
        name=f"s5_scan_n{n}",
    )(u_l, toep, wsr, wsi, wor, woi, at, h0r, h0i)


def _block_transpose8(xs):
    lane = lax.broadcasted_iota(jnp.int32, xs[0].shape, 1)
    for d in (4, 2, 1):
        sh = d * GROUP_CH
        keep = (lane & sh) == 0
        nxt = list(xs)
        for i in range(8):
            if i & d == 0:
                lo, hi = xs[i], xs[i + d]
                nxt[i] = jnp.where(keep, lo, pltpu.roll(hi, sh, 1))
                nxt[i + d] = jnp.where(keep, pltpu.roll(lo, LANES - sh, 1), hi)
        xs = nxt
    return xs


def _s5_scan_fused_kernel(u_ref, toep_ref, wsr_ref, wsi_ref, wor_ref, woi_ref, at_ref, h0r_ref, h0i_ref, d_ref,
                          a_ref, hr_ref, hi_ref, hr_s, hi_s, sr_s, si_s, pr_s, pi_s, y_s, *, bsz):
    r = pl.program_id(1)
    _, n_t, rows, _ = u_ref.shape
    gpb = toep_ref.shape[0]
    n_half = n_t // 8

    @pl.when(r == 0)
    def _():
        hr_s[...] = h0r_ref[...]
        hi_s[...] = h0i_ref[...]

    packed = [pltpu.bitcast(u_ref[0, t].astype(BF16), jnp.uint32) for t in range(n_t)]
    halves = [_block_transpose8(packed[h * 8:(h + 1) * 8]) for h in range(n_half)]
    for g in range(gpb):
        u_g = jnp.concatenate([pltpu.bitcast(hv[g], BF16) for hv in halves], axis=1)
        y_s[g] = _dot(u_g, toep_ref[g])
        sr_s[g] = _dot(u_g, wsr_ref[g])
        si_s[g] = _dot(u_g, wsi_ref[g])

    ar = at_ref[:, 0:1, :]
    ai = at_ref[:, 1:2, :]

    def body(k, carry):
        hr, hi = carry
        rs = pl.ds(pl.multiple_of(k * bsz, bsz), bsz)
        pr_s[:, rs, :] = hr
        pi_s[:, rs, :] = hi
        return (ar * hr - ai * hi + sr_s[:, rs, :], ar * hi + ai * hr + si_s[:, rs, :])

    hr, hi = lax.fori_loop(0, rows // bsz, body, (hr_s[...], hi_s[...]))
    hr_s[...] = hr
    hi_s[...] = hi

    @pl.when(r == pl.num_programs(1) - 1)
    def _():
        hr_ref[...] = hr
        hi_ref[...] = hi

    ys = []
    for g in range(gpb):
        ys.append(y_s[g] + _dot(pr_s[g].astype(BF16), wor_ref[g]) + _dot(pi_s[g].astype(BF16), woi_ref[g]))
    d = d_ref[...]
    for h in range(n_half):
        zs = _block_transpose8([y[:, h * LANES:(h + 1) * LANES] for y in ys])
        for tl in range(8):
            t = h * 8 + tl
            a_ref[t] = jax.nn.gelu(zs[tl] + d * u_ref[0, t]).astype(BF16)


def _s5_scan_fused(uz4, ops, h0r, h0i, d_skip, *, bsz, rows):
    toep, wsr, wsi, wor, woi, at = ops
    _, n_t, n, width = uz4.shape
    g = toep.shape[0]
    tc = toep.shape[1]
    p = STATE_DIM
    gpb = LANES // GROUP_CH
    assert n_t % 8 == 0 and tc == n_t * GROUP_CH and n % rows == 0 and rows % bsz == 0

    def gspec(a, b):
        return pl.BlockSpec((gpb, a, b), lambda i, r: (i, 0, 0))

    return pl.pallas_call(
        functools.partial(_s5_scan_fused_kernel, bsz=bsz),
        grid=(g // gpb, n // rows),
        in_specs=[
            pl.BlockSpec((1, n_t, rows, LANES), lambda i, r: (0, 0, r, i)),
            gspec(tc, tc), gspec(tc, p), gspec(tc, p), gspec(p, tc), gspec(p, tc), gspec(2, p),
            gspec(bsz, p), gspec(bsz, p),
            pl.BlockSpec((1, LANES), lambda i, r: (0, i)),
        ],
        out_specs=[
            pl.BlockSpec((n_t, rows, LANES), lambda i, r: (0, r, i)),
            gspec(bsz, p), gspec(bsz, p),
        ],
        out_shape=[
            jax.ShapeDtypeStruct((n_t, n, width), BF16),
            jax.ShapeDtypeStruct((g, bsz, p), F32),
            jax.ShapeDtypeStruct((g, bsz, p), F32),
        ],
        scratch_shapes=[pltpu.VMEM((gpb, bsz, p), F32)] * 2 + [pltpu.VMEM((gpb, rows, p), F32)] * 4
        + [pltpu.VMEM((gpb, rows, tc), F32)],
        compiler_params=_params("parallel", "arbitrary"),
        name="s5_scan_fused",
    )(uz4, toep, wsr, wsi, wor, woi, at, h0r, h0i, d_skip)


def _s5_act_kernel(y_ref, u_ref, d_ref, a_ref):
    a_ref[...] = jax.nn.gelu(y_ref[...] + d_ref[...] * u_ref[0]).astype(BF16)


def _s5_act(y, uz, d_skip):
    m, w = y.shape
    return pl.pallas_call(
        _s5_act_kernel,
        grid=(1,),
        in_specs=[
            pl.BlockSpec((m, w), lambda i: (0, 0)),
            pl.BlockSpec((1, m, w), lambda i: (0, 0, 0)),
            pl.BlockSpec((1, w), lambda i: (0, 0)),
        ],
        out_specs=pl.BlockSpec((m, w), lambda i: (0, 0)),
        out_shape=jax.ShapeDtypeStruct((m, w), BF16),
        compiler_params=_params("arbitrary"),
        name="s5_act_sample",
    )(y, uz, d_skip)


def _s5_glu_out_kernel(a_ref, z_ref, x_ref, wa_ref, wb_ref, ba_ref, bb_ref, wo_ref, o_ref, acc_s):
    j = pl.program_id(1)
    a = a_ref[...]
    ga = _dot(a, wa_ref[0]) + ba_ref[0]
    gb = _dot(a, wb_ref[0]) + bb_ref[0]
    z = z_ref[0]
    v = (ga * jax.nn.sigmoid(gb) * (z * jax.nn.sigmoid(z))).astype(BF16)
    part = _dot(v, wo_ref[...])

    @pl.when(j == 0)
    def _():
        acc_s[...] = part

    @pl.when(j > 0)
    def _():
        acc_s[...] += part

    @pl.when(j == pl.num_programs(1) - 1)
    def _():
        o_ref[...] = x_ref[...] + acc_s[...]


def _s5_glu_out(a, uz, x, w_glu2, b_glu2, w_out, *, tm, tn, name):
    m, w = a.shape
    d = x.shape[1]
    tm = min(tm, m)
    return pl.pallas_call(
        _s5_glu_out_kernel,
        grid=(m // tm, w // tn),
        in_specs=[
            pl.BlockSpec((tm, w), lambda i, j: (i, 0)),
            pl.BlockSpec((1, tm, tn), lambda i, j: (1, i, j)),
            pl.BlockSpec((tm, d), lambda i, j: (i, 0)),
            pl.BlockSpec((1, w, tn), lambda i, j: (0, 0, j)),
            pl.BlockSpec((1, w, tn), lambda i, j: (1, 0, j)),
            pl.BlockSpec((1, 1, tn), lambda i, j: (0, 0, j)),
            pl.BlockSpec((1, 1, tn), lambda i, j: (1, 0, j)),
            pl.BlockSpec((tn, d), lambda i, j: (j, 0)),
        ],
        out_specs=pl.BlockSpec((tm, d), lambda i, j: (i, 0)),
        out_shape=jax.ShapeDtypeStruct((m, d), F32),
        scratch_shapes=[pltpu.VMEM((tm, d), F32)],
        compiler_params=_params("parallel", "arbitrary"),
        name=name,
    )(a, uz, x, w_glu2, w_glu2, b_glu2, b_glu2, w_out)


def _latent_kernel(x_ref, g_ref, wl_ref, wr_ref, ws_ref, gl_ref, cc_ref, ss_ref, lat_ref, kr_ref, kcat_ref):
    h = _rms(x_ref[...], g_ref[...]).astype(BF16)
    lat = _rms(_dot(h, wl_ref[...]), gl_ref[...])
    kr = _dot(h, wr_ref[...]) * cc_ref[...] + _dot(h, ws_ref[...]) * ss_ref[...]
    lat_ref[...] = lat
    kr_ref[...] = kr[:, :QK_ROPE]
    kcat_ref[:, :KV_LORA] = lat.astype(BF16)
    kcat_ref[:, KV_LORA:] = kr.astype(BF16)


def _latent(x, g, wl, wr, ws, gl, cc, ss, *, tm, name):
    m, d = x.shape
    tm = min(tm, m)
    nt = cc.shape[0] // tm

    def full(a):
        return pl.BlockSpec(a.shape, lambda i: (0,) * a.ndim)

    return pl.pallas_call(
        _latent_kernel,
        grid=(m // tm,),
        in_specs=[
            pl.BlockSpec((tm, d), lambda i: (i, 0)),
            full(g), full(wl), full(wr), full(ws), full(gl),
            pl.BlockSpec((tm, LANES), lambda i: (i % nt, 0)),
            pl.BlockSpec((tm, LANES), lambda i: (i % nt, 0)),
        ],
        out_specs=[
            pl.BlockSpec((tm, KV_LORA), lambda i: (i, 0)),
            pl.BlockSpec((tm, QK_ROPE), lambda i: (i, 0)),
            pl.BlockSpec((tm, QK_PAD), lambda i: (i, 0)),
        ],
        out_shape=[
            jax.ShapeDtypeStruct((m, KV_LORA), F32),
            jax.ShapeDtypeStruct((m, QK_ROPE), F32),
            jax.ShapeDtypeStruct((m, QK_PAD), BF16),
        ],
        compiler_params=_params("parallel"),
        name=name,
    )(x, g, wl, wr, ws, gl, cc, ss)


def _mla_front_kernel(x_ref, g_ref, wcq_ref, wg_ref, gq_ref, wqn_ref, wqr_ref, wqs_ref, wuk_ref,
                      cc_ref, ss_ref, q_ref, sg_ref):
    h = _rms(x_ref[...], g_ref[...]).astype(BF16)
    cq = _rms(_dot(h, wcq_ref[...]), gq_ref[...]).astype(BF16)
    gate = _dot(h, wg_ref[...])
    sg_ref[...] = gate * jax.nn.sigmoid(gate)
    qn = _dot(cq, wqn_ref[...]).astype(BF16)
    qr = _dot(cq, wqr_ref[...])
    qs = _dot(cq, wqs_ref[...])
    cc = cc_ref[...]
    ss = ss_ref[...]
    for hd in range(N_HEADS):
        sl = slice(hd * LANES, (hd + 1) * LANES)
        q_ref[0, hd, :, :KV_LORA] = _dot(qn[:, sl], wuk_ref[hd]).astype(BF16)
        q_ref[0, hd, :, KV_LORA:] = (qr[:, sl] * cc + qs[:, sl] * ss).astype(BF16)


def _mla_front(x, g, wcq, wg, gq, wqn, wqr, wqs, wuk, cc, ss, *, tm, name):
    m, d = x.shape
    tm = min(tm, m)
    nt = cc.shape[0] // tm

    def full(a):
        return pl.BlockSpec(a.shape, lambda i: (0,) * a.ndim)

    return pl.pallas_call(
        _mla_front_kernel,
        grid=(m // tm,),
        in_specs=[
            pl.BlockSpec((tm, d), lambda i: (i, 0)),
            full(g), full(wcq), full(wg), full(gq), full(wqn), full(wqr), full(wqs), full(wuk),
            pl.BlockSpec((tm, LANES), lambda i: (i % nt, 0)),
            pl.BlockSpec((tm, LANES), lambda i: (i % nt, 0)),
        ],
        out_specs=[
            pl.BlockSpec((1, N_HEADS, tm, QK_PAD), lambda i: (i, 0, 0, 0)),
            pl.BlockSpec((tm, N_HEADS * V_HEAD), lambda i: (i, 0)),
        ],
        out_shape=[
            jax.ShapeDtypeStruct((m // tm, N_HEADS, tm, QK_PAD), BF16),
            jax.ShapeDtypeStruct((m, N_HEADS * V_HEAD), F32),
        ],
        compiler_params=_params("parallel"),
        name=name,
    )(x, g, wcq, wg, gq, wqn, wqr, wqs, wuk, cc, ss)


def _lane_tile(x, width):
    return x if width == LANES else jnp.concatenate([x] * (width // LANES), axis=1)


def _softmax_update(s, kl, m_ref, l_ref, acc_ref, rows):
    m_prev = m_ref[rows, :]
    m_new = jnp.maximum(m_prev, jnp.max(s, axis=1, keepdims=True))
    alpha = jnp.exp(m_prev - m_new)
    p = jnp.exp(s - _lane_tile(m_new, s.shape[1]))
    l_ref[rows, :] = alpha * l_ref[rows, :] + jnp.sum(p, axis=1, keepdims=True)
    acc_ref[rows, :] = _lane_tile(alpha, kl.shape[1]) * acc_ref[rows, :] + _dot(p.astype(BF16), kl)
    m_ref[rows, :] = m_new


def _attn_prompt_kernel(q_ref, k_ref, o_ref, m_s, l_s, acc_s, *, tq, tk, heads_per_chunk):
    qi = pl.program_id(1)
    ki = pl.program_id(2)
    diag = (qi * tq + (tq - 1)) // tk
    rc = heads_per_chunk * tq

    @pl.when(ki == 0)
    def _():
        m_s[...] = jnp.full(m_s.shape, NEG_INF, F32)
        l_s[...] = jnp.zeros(l_s.shape, F32)
        acc_s[...] = jnp.zeros(acc_s.shape, F32)

    def update(masked):
        k = k_ref[...]
        kl = k[:, :KV_LORA]
        for c in range(N_HEADS // heads_per_chunk):
            q = q_ref[0, c * heads_per_chunk:(c + 1) * heads_per_chunk].reshape(rc, QK_PAD)
            s = _dot_nt(q, k) * SOFTMAX_SCALE
            if masked:
                tok = qi * tq + (lax.broadcasted_iota(jnp.int32, (rc, tk), 0) & (tq - 1))
                col = ki * tk + lax.broadcasted_iota(jnp.int32, (rc, tk), 1)
                s = jnp.where(col <= tok, s, NEG_INF)
            _softmax_update(s, kl, m_s, l_s, acc_s, pl.ds(c * rc, rc))

    @pl.when(ki < diag)
    def _():
        update(False)

    @pl.when(ki == diag)
    def _():
        update(True)
        for hd in range(N_HEADS):
            rows = pl.ds(hd * tq, tq)
            o_ref[0, hd] = (acc_s[rows, :] / _lane_tile(l_s[rows, :], KV_LORA)).astype(BF16)


def _attn_prompt(q, kcat, *, bsz, seq, tq, tk):
    assert tq & (tq - 1) == 0 and seq % tq == 0 and seq % tk == 0
    nq = seq // tq
    nk = seq // tk
    rows = N_HEADS * tq
    return pl.pallas_call(
        functools.partial(_attn_prompt_kernel, tq=tq, tk=tk, heads_per_chunk=2),
        grid=(bsz, nq, nk),
        in_specs=[
            pl.BlockSpec((1, N_HEADS, tq, QK_PAD), lambda b, i, j: (b * nq + i, 0, 0, 0)),
            pl.BlockSpec((tk, QK_PAD), lambda b, i, j: (b * nk + jnp.minimum(j, (i * tq + (tq - 1)) // tk), 0)),
        ],
        out_specs=pl.BlockSpec((1, N_HEADS, tq, KV_LORA), lambda b, i, j: (b * nq + i, 0, 0, 0)),
        out_shape=jax.ShapeDtypeStruct((bsz * nq, N_HEADS, tq, KV_LORA), BF16),
        scratch_shapes=[
            pltpu.VMEM((rows, LANES), F32),
            pltpu.VMEM((rows, LANES), F32),
            pltpu.VMEM((rows, KV_LORA), F32),
        ],
        compiler_params=_params("parallel", "parallel", "arbitrary"),
        name="attn_prompt",
    )(q, kcat)


def _attn_sample_kernel(pt_ref, q_ref, *refs, n_pg, dec_seq):
    lat_refs = refs[:n_pg]
    krt_refs = refs[n_pg:2 * n_pg]
    nl_ref, nk_ref, o_ref, kl_s, kr_s, m_s, l_s, acc_s = refs[2 * n_pg:]
    step = pl.program_id(1)

    @pl.when(step == 0)
    def _():
        m_s[...] = jnp.full(m_s.shape, NEG_INF, F32)
        l_s[...] = jnp.zeros(l_s.shape, F32)
        acc_s[...] = jnp.zeros(acc_s.shape, F32)

    for i in range(n_pg):
        kl_s[i * PAGE_SIZE:(i + 1) * PAGE_SIZE, :] = lat_refs[i][0].astype(BF16)
        kr_s[:, i * PAGE_SIZE:(i + 1) * PAGE_SIZE] = krt_refs[i][0].astype(BF16)

    q = q_ref[0]
    ql = q[:, :KV_LORA]
    qr = q[:, KV_LORA:KV_LORA + QK_ROPE]
    kl = kl_s[...]
    s = (_dot_nt(ql, kl) + _dot(qr, kr_s[...])) * SOFTMAX_SCALE
    _softmax_update(s, kl, m_s, l_s, acc_s, slice(None))

    @pl.when(step == pl.num_programs(1) - 1)
    def _():
        nl = nl_ref[0].astype(BF16).astype(F32)
        nk = nk_ref[0].astype(BF16).astype(F32)
        qlf = ql.astype(F32)
        qrf = qr.astype(F32)
        tok = lax.broadcasted_iota(jnp.int32, (q.shape[0], 1), 0) & (dec_seq - 1)
        m = m_s[...]
        l = l_s[...]
        acc = acc_s[...]
        for j in range(dec_seq):
            sj = (jnp.sum(qlf * nl[j:j + 1], axis=-1, keepdims=True)
                  + jnp.sum(qrf * nk[j:j + 1], axis=-1, keepdims=True)) * SOFTMAX_SCALE
            sj = jnp.where(tok >= j, sj, NEG_INF)
            m_new = jnp.maximum(m, sj)
            alpha = jnp.exp(m - m_new)
            pj = jnp.exp(sj - m_new)
            l = alpha * l + pj
            pv = _lane_tile(pj, KV_LORA).astype(BF16).astype(F32) * nl[j:j + 1]
            acc = _lane_tile(alpha, KV_LORA) * acc + pv
            m = m_new
        o_ref[0] = (acc / _lane_tile(l, KV_LORA)).astype(BF16)


def _attn_sample(q, cache_latent, cache_krope_t, page_table, new_lat, new_kr, *, n_pg):
    n_seq, n_pages = page_table.shape
    rows = q.shape[1]
    dec_seq = new_lat.shape[1]
    assert dec_seq & (dec_seq - 1) == 0 and n_pages % n_pg == 0
    steps = n_pages // n_pg
    pt = page_table.reshape(-1)

    def page_spec(a, b, i):
        return pl.BlockSpec((1, a, b), lambda n, s, pt_ref: (pt_ref[n * n_pages + s * n_pg + i], 0, 0))

    def seq_spec(a, b):
        return pl.BlockSpec((1, a, b), lambda n, s, pt_ref: (n, 0, 0))

    grid_spec = pltpu.PrefetchScalarGridSpec(
        num_scalar_prefetch=1,
        grid=(n_seq, steps),
        in_specs=([seq_spec(rows, QK_PAD)]
                  + [page_spec(PAGE_SIZE, KV_LORA, i) for i in range(n_pg)]
                  + [page_spec(QK_ROPE, PAGE_SIZE, i) for i in range(n_pg)]
                  + [seq_spec(dec_seq, KV_LORA), seq_spec(dec_seq, QK_ROPE)]),
        out_specs=seq_spec(rows, KV_LORA),
        scratch_shapes=[
            pltpu.VMEM((n_pg * PAGE_SIZE, KV_LORA), BF16),
            pltpu.VMEM((QK_ROPE, n_pg * PAGE_SIZE), BF16),
            pltpu.VMEM((rows, LANES), F32),
            pltpu.VMEM((rows, LANES), F32),
            pltpu.VMEM((rows, KV_LORA), F32),
        ],
    )
    return pl.pallas_call(
        functools.partial(_attn_sample_kernel, n_pg=n_pg, dec_seq=dec_seq),
        grid_spec=grid_spec,
        out_shape=jax.ShapeDtypeStruct((n_seq, rows, KV_LORA), BF16),
        compiler_params=_params("parallel", "arbitrary"),
        name="attn_sample",
    )(pt, q, *([cache_latent] * n_pg), *([cache_krope_t] * n_pg), new_lat, new_kr)


def _mla_back_kernel(o_ref, sg_ref, x_ref, wuv_ref, wo_ref, out_ref):
    parts = []
    for hd in range(N_HEADS):
        oh = _dot(o_ref[0, hd], wuv_ref[hd])
        parts.append((oh * sg_ref[:, hd * V_HEAD:(hd + 1) * V_HEAD]).astype(BF16))
    out_ref[...] = x_ref[...] + _dot(jnp.concatenate(parts, axis=1), wo_ref[...])


def _mla_back(o, sg, x, wuv, wo, *, name):
    m, d = x.shape
    tm = o.shape[2]

    def full(a):
        return pl.BlockSpec(a.shape, lambda i: (0,) * a.ndim)

    def rows(a):
        return pl.BlockSpec((tm, a.shape[1]), lambda i: (i, 0))

    return pl.pallas_call(
        _mla_back_kernel,
        grid=(m // tm,),
        in_specs=[pl.BlockSpec((1,) + o.shape[1:], lambda i: (i, 0, 0, 0)), rows(sg), rows(x), full(wuv), full(wo)],
        out_specs=pl.BlockSpec((tm, d), lambda i: (i, 0)),
        out_shape=jax.ShapeDtypeStruct((m, d), F32),
        compiler_params=_params("parallel"),
        name=name,
    )(o, sg, x, wuv, wo)


def _final_norm_kernel(x_ref, g_ref, o_ref):
    o_ref[...] = _rms(x_ref[...], g_ref[...])


def _final_norm(x, g, *, tm, name):
    m, d = x.shape
    tm = min(tm, m)
    return pl.pallas_call(
        _final_norm_kernel,
        grid=(m // tm,),
        in_specs=[pl.BlockSpec((tm, d), lambda i: (i, 0)), pl.BlockSpec((1, d), lambda i: (0, 0))],
        out_specs=pl.BlockSpec((tm, d), lambda i: (i, 0)),
        out_shape=jax.ShapeDtypeStruct((m, d), F32),
        compiler_params=_params("parallel"),
        name=name,
    )(x, g)


def _to_scan_layout(u, bsz, seq, chunk):
    k = seq // chunk
    g = u.shape[1] // GROUP_CH
    u5 = u.astype(BF16).reshape(bsz, k, chunk, g, GROUP_CH)
    return u5.transpose(3, 1, 0, 2, 4).reshape(g, k * bsz, chunk * GROUP_CH)


def _from_scan_layout(y, bsz, seq, chunk):
    k = seq // chunk
    g = y.shape[0]
    y5 = y.reshape(g, k, bsz, chunk, GROUP_CH)
    return y5.transpose(2, 1, 3, 0, 4).reshape(bsz * seq, g * GROUP_CH)


def _rope_tables(pos):
    half = QK_ROPE // 2
    inv = ROPE_THETA ** (-jnp.arange(half, dtype=F32) / half)
    ang = pos.astype(F32)[:, None] * inv[None, :]
    cos = jnp.cos(ang)
    sin = jnp.sin(ang)
    pad = jnp.zeros((pos.shape[0], LANES - QK_ROPE), F32)
    cc = jnp.concatenate([cos, cos, pad], axis=1)
    ss = jnp.concatenate([-sin, sin, pad], axis=1)
    return cc, ss


def _swap_halves(w):
    half = w.shape[-1] // 2
    return jnp.concatenate([w[..., half:], w[..., :half]], axis=-1)


def _pad_lanes(w):
    pad = [(0, 0)] * (w.ndim - 1) + [(0, LANES - w.shape[-1])]
    return jnp.pad(w, pad)


def kernel(x_prompt, x_sample, cache_latent, cache_krope, page_table, state_ssm_re, state_ssm_im, norm_a, w_in_a, a_re, a_im, log_dt, b_re, b_im, c_re, c_im, d_skip, w_glu, b_glu, w_out_a, norm_kv, w_dkv, norm_latent, w_uk, w_uv, norm_b, w_in_b, norm_q, w_uq, w_out_b, norm_f):
    bsz, seq, d = x_prompt.shape
    dbs, dseq, _ = x_sample.shape
    n_a = norm_a.shape[0]
    n_b = norm_b.shape[0]
    width = d_skip.shape[1]
    n_groups = width // GROUP_CH
    past_len = page_table.shape[1] * PAGE_SIZE

    n_chunks = seq // PROMPT_CHUNK
    xp = x_prompt.reshape(bsz, n_chunks, PROMPT_CHUNK, d).transpose(2, 1, 0, 3).reshape(bsz * seq, d)
    xs = x_sample.reshape(dbs * dseq, d)

    hp_re, hp_im, hs_re, hs_im = [], [], [], []
    zeros_p = jnp.zeros((n_groups, bsz, STATE_DIM), F32)
    for i in range(n_a):
        w_in3 = w_in_a[i].astype(BF16).reshape(d, 2, width).transpose(1, 0, 2)
        w_glu2 = w_glu[i].astype(BF16).reshape(width, 2, width).transpose(1, 0, 2)
        b_glu2 = b_glu[i].reshape(2, 1, width)
        w_out = w_out_a[i].astype(BF16)
        g = norm_a[i][None, :]
        dsk = d_skip[i][None, :]
        ssm = (a_re[i], a_im[i], log_dt[i], b_re[i], b_im[i], c_re[i], c_im[i])

        uz = _norm_matmul(xp, g, w_in3, tm=512, tn=1024, name="s5_in_prompt")
        ops = _s5_prep(*ssm, chunk=PROMPT_CHUNK)
        uz4 = uz.reshape(2, PROMPT_CHUNK, n_chunks * bsz, width)
        a, hr, hi = _s5_scan_fused(uz4, ops, zeros_p, zeros_p, dsk, bsz=bsz, rows=SCAN_ROWS)
        xp = _s5_glu_out(a.reshape(bsz * seq, width), uz, xp, w_glu2, b_glu2, w_out, tm=1024, tn=512,
                         name="s5_glu_out_prompt")
        hp_re.append(hr.transpose(1, 0, 2))
        hp_im.append(hi.transpose(1, 0, 2))

        uz = _norm_matmul(xs, g, w_in3, tm=512, tn=1024, name="s5_in_sample")
        ops = _s5_prep(*ssm, chunk=dseq)
        u_l = _to_scan_layout(uz[0], dbs, dseq, dseq)
        y_l, hr, hi = _s5_scan(u_l, ops, state_ssm_re[i].transpose(1, 0, 2), state_ssm_im[i].transpose(1, 0, 2),
                               n_chunks=1, bsz=dbs)
        y = _from_scan_layout(y_l, dbs, dseq, dseq)
        a = _s5_act(y, uz, dsk)
        xs = _s5_glu_out(a, uz, xs, w_glu2, b_glu2, w_out, tm=512, tn=512, name="s5_glu_out_sample")
        hs_re.append(hr.transpose(1, 0, 2))
        hs_im.append(hi.transpose(1, 0, 2))

    xp = xp.reshape(PROMPT_CHUNK, n_chunks, bsz, d).transpose(2, 1, 0, 3).reshape(bsz * seq, d)

    cc_p, ss_p = _rope_tables(jnp.arange(seq, dtype=jnp.int32))
    pos_s = past_len + jnp.arange(dseq, dtype=jnp.int32)
    cc_s, ss_s = _rope_tables(jnp.tile(pos_s, dbs))
    w_lat = w_dkv[:, :KV_LORA].astype(BF16)
    w_kr = _pad_lanes(w_dkv[:, KV_LORA:]).astype(BF16)
    w_ks = _pad_lanes(_swap_halves(w_dkv[:, KV_LORA:])).astype(BF16)
    g_kv = norm_kv[None, :]
    g_lat = norm_latent[None, :]
    lat_p, kr_p, kcat_p = _latent(xp, g_kv, w_lat, w_kr, w_ks, g_lat, cc_p, ss_p, tm=512, name="latent_prompt")
    lat_s, kr_s, _ = _latent(xs, g_kv, w_lat, w_kr, w_ks, g_lat, cc_s, ss_s, tm=512, name="latent_sample")
    new_lat = lat_s.reshape(dbs, dseq, KV_LORA)
    new_kr = kr_s.reshape(dbs, dseq, QK_ROPE)

    cache_krope_t = jnp.swapaxes(cache_krope, 1, 2)
    w_ukt = w_uk.astype(BF16).reshape(KV_LORA, N_HEADS, QK_NOPE).transpose(1, 2, 0)
    w_uvh = w_uv.astype(BF16).reshape(KV_LORA, N_HEADS, V_HEAD).transpose(1, 0, 2)
    for j in range(n_b):
        g = norm_b[j][None, :]
        w_cq = w_in_b[j][:, :Q_LORA].astype(BF16)
        w_gate = w_in_b[j][:, Q_LORA:].astype(BF16)
        g_q = norm_q[j][None, :]
        wq3 = w_uq[j].reshape(Q_LORA, N_HEADS, QK_NOPE + QK_ROPE)
        w_qn = wq3[:, :, :QK_NOPE].reshape(Q_LORA, N_HEADS * QK_NOPE).astype(BF16)
        w_qr = _pad_lanes(wq3[:, :, QK_NOPE:]).reshape(Q_LORA, N_HEADS * LANES).astype(BF16)
        w_qs = _pad_lanes(_swap_halves(wq3[:, :, QK_NOPE:])).reshape(Q_LORA, N_HEADS * LANES).astype(BF16)
        w_o = w_out_b[j].astype(BF16)

        q, sg = _mla_front(xp, g, w_cq, w_gate, g_q, w_qn, w_qr, w_qs, w_ukt, cc_p, ss_p, tm=ATTN_TQ,
                           name="mla_front_prompt")
        o = _attn_prompt(q, kcat_p, bsz=bsz, seq=seq, tq=ATTN_TQ, tk=ATTN_TK)
        xp = _mla_back(o, sg, xp, w_uvh, w_o, name="mla_back_prompt")

        q, sg = _mla_front(xs, g, w_cq, w_gate, g_q, w_qn, w_qr, w_qs, w_ukt, cc_s, ss_s, tm=ATTN_TQ,
                           name="mla_front_sample")
        spt = ATTN_TQ // dseq
        q_seq = q.reshape(-1, N_HEADS, spt, dseq, QK_PAD).transpose(0, 2, 1, 3, 4)
        o = _attn_sample(q_seq.reshape(dbs, N_HEADS * dseq, QK_PAD), cache_latent, cache_krope_t, page_table,
                         new_lat, new_kr, n_pg=SAMPLE_PAGES_PER_STEP)
        o = o.reshape(-1, spt, N_HEADS, dseq, KV_LORA).transpose(0, 2, 1, 3, 4)
        xs = _mla_back(o.reshape(-1, N_HEADS, ATTN_TQ, KV_LORA), sg, xs, w_uvh, w_o, name="mla_back_sample")

    y_prompt = _final_norm(xp, norm_f[None, :], tm=512, name="final_norm_prompt").reshape(bsz, seq, d)
    y_sample = _final_norm(xs, norm_f[None, :], tm=512, name="final_norm_sample").reshape(dbs, dseq, d)
    return (y_prompt, y_sample,
            lat_p.reshape(bsz, seq, KV_LORA), kr_p.reshape(bsz, seq, QK_ROPE),
            new_lat, new_kr,
            jnp.stack(hp_re), jnp.stack(hp_im), jnp.stack(hs_re), jnp.stack(hs_im))
```

```python
import functools
import math

import jax
import jax.numpy as jnp
from jax import lax
from jax.experimental import pallas as pl
from jax.experimental.pallas import tpu as pltpu

F32 = jnp.float32
BF16 = jnp.bfloat16

GROUP_CH = 16
STATE_DIM = 64
N_HEADS = 8
QK_NOPE = 128
QK_ROPE = 64
V_HEAD = 128
KV_LORA = 256
Q_LORA = 384
PAGE_SIZE = 128
ROPE_THETA = 10000.0
RMS_EPS = 1e-6
SOFTMAX_SCALE = 1.0 / math.sqrt(QK_NOPE + QK_ROPE)
SCALE_LOG2E = SOFTMAX_SCALE * math.log2(math.e)
NEG_INF = -1e30
LANES = 128
QK_PAD = KV_LORA + LANES
PROMPT_CHUNK = 16
ATTN_TQ = 256
ATTN_TK = 512
SAMPLE_PAGES_PER_STEP = 16
SCAN_ROWS = 512
VMEM_LIMIT = 56 * 1024 * 1024


def _dot(a, b):
    return jnp.dot(a, b, preferred_element_type=F32)


def _dot_nt(a, b):
    return lax.dot_general(a, b, (((1,), (1,)), ((), ())), preferred_element_type=F32)


def _dot_f32(a, b):
    return jnp.dot(a, b, preferred_element_type=F32, precision=lax.Precision.HIGHEST)


def _rms(x, g):
    return x * lax.rsqrt(jnp.mean(x * x, axis=-1, keepdims=True) + RMS_EPS) * g


def _params(*sem):
    return pltpu.CompilerParams(dimension_semantics=sem, vmem_limit_bytes=VMEM_LIMIT)


def _norm_matmul_kernel(x_ref, g_ref, w_ref, o_ref, *, tn):
    h = _rms(x_ref[...], g_ref[...]).astype(BF16)
    parts, _, n = w_ref.shape
    for p in range(parts):
        for j in range(n // tn):
            cols = slice(j * tn, (j + 1) * tn)
            o_ref[p, :, cols] = _dot(h, w_ref[p, :, cols])


def _norm_matmul(x, g, w3, *, tm, tn, name):
    m, d = x.shape
    parts, _, n = w3.shape
    tm = min(tm, m)
    return pl.pallas_call(
        functools.partial(_norm_matmul_kernel, tn=tn),
        grid=(m // tm,),
        in_specs=[
            pl.BlockSpec((tm, d), lambda i: (i, 0)),
            pl.BlockSpec((1, d), lambda i: (0, 0)),
            pl.BlockSpec((parts, d, n), lambda i: (0, 0, 0), pipeline_mode=pl.Buffered(1)),
        ],
        out_specs=pl.BlockSpec((parts, tm, n), lambda i: (0, i, 0)),
        out_shape=jax.ShapeDtypeStruct((parts, m, n), F32),
        compiler_params=_params("parallel"),
        name=name,
    )(x, g, w3)


def _discretize(lam_re, lam_im, log_dt):
    dt = jnp.exp(log_dt)
    mag = jnp.exp(lam_re * dt)
    lb_re = mag * jnp.cos(lam_im * dt)
    lb_im = mag * jnp.sin(lam_im * dt)
    den = lam_re * lam_re + lam_im * lam_im
    nr = lb_re - 1.0
    f_re = (nr * lam_re + lb_im * lam_im) / den
    f_im = (lb_im * lam_re - nr * lam_im) / den
    return lb_re, lb_im, f_re, f_im


def _cpow(br, bi, e, nbits, shape):
    br = jnp.broadcast_to(br, shape)
    bi = jnp.broadcast_to(bi, shape)
    rr = jnp.ones(shape, F32)
    ri = jnp.zeros(shape, F32)
    for j in range(nbits):
        bit = ((e >> j) & 1) == 1
        nr = rr * br - ri * bi
        ni = rr * bi + ri * br
        rr = jnp.where(bit, nr, rr)
        ri = jnp.where(bit, ni, ri)
        if j + 1 < nbits:
            br, bi = br * br - bi * bi, 2.0 * br * bi
    return rr, ri


def _s5_prep_kernel(*refs, chunk):
    for g in range(refs[0].shape[0]):
        _s5_prep_group(g, *refs, chunk=chunk)


def _s5_prep_group(g, pcol_ref, prow_ref, btr_ref, bti_ref, ctr_ref, cti_ref,
                   toep_ref, wsr_ref, wsi_ref, wor_ref, woi_ref, at_ref, *, chunk):
    tc = chunk * GROUP_CH
    tl = ctr_ref.shape[2]
    nbits = chunk.bit_length()
    shift = GROUP_CH.bit_length() - 1

    pc = pcol_ref[g]
    lbr_c, lbi_c, _, _ = _discretize(pc[:, 0:1], pc[:, 1:2], pc[:, 2:3])
    pr = prow_ref[g]
    lbr_r, lbi_r, f_re, f_im = _discretize(pr[0:1], pr[1:2], pr[2:3])

    cr = ctr_ref[g]
    ci = cti_ref[g]
    shape = cr.shape
    lane_t = lax.broadcasted_iota(jnp.int32, shape, 1) >> shift
    p0r, p0i = _cpow(lbr_c, lbi_c, lane_t, nbits, shape)
    p1r = p0r * lbr_c - p0i * lbi_c
    p1i = p0r * lbi_c + p0i * lbr_c
    rr = cr * p0r - ci * p0i
    ri = cr * p0i + ci * p0r
    wor_ref[g] = (cr * p1r - ci * p1i)[:, :tc].astype(BF16)
    woi_ref[g] = (-(cr * p1i + ci * p1r))[:, :tc].astype(BF16)

    btr = btr_ref[g]
    bti = bti_ref[g]
    bbr = f_re * btr - f_im * bti
    bbi = f_re * bti + f_im * btr

    krow = _dot_f32(bbr[:GROUP_CH], rr) - _dot_f32(bbi[:GROUP_CH], ri)
    lane = lax.broadcasted_iota(jnp.int32, (GROUP_CH, tl), 1)
    for s in range(chunk):
        if s == 0:
            blk = krow
        else:
            blk = jnp.where(lane >= GROUP_CH * s, pltpu.roll(krow, GROUP_CH * s, 1), 0.0)
        toep_ref[g, GROUP_CH * s:GROUP_CH * (s + 1), :] = blk[:, :tc].astype(BF16)

    shape_s = btr.shape
    row_e = (chunk - 1) - (lax.broadcasted_iota(jnp.int32, shape_s, 0) >> shift)
    qr, qi = _cpow(lbr_r, lbi_r, row_e, nbits, shape_s)
    wsr_ref[g] = (bbr * qr - bbi * qi).astype(BF16)
    wsi_ref[g] = (bbr * qi + bbi * qr).astype(BF16)

    ar, ai = lbr_r, lbi_r
    for _ in range(chunk.bit_length() - 1):
        ar, ai = ar * ar - ai * ai, 2.0 * ar * ai
    at_ref[g, 0:1, :] = ar
    at_ref[g, 1:2, :] = ai


def _s5_prep(a_re, a_im, log_dt, b_re, b_im, c_re, c_im, *, chunk):
    p = a_re.shape[-1]
    a_re, a_im = a_re.reshape(-1, p), a_im.reshape(-1, p)
    log_dt = log_dt.reshape(-1)
    b_re, b_im = b_re.reshape((-1,) + b_re.shape[-2:]), b_im.reshape((-1,) + b_im.shape[-2:])
    c_re, c_im = c_re.reshape((-1,) + c_re.shape[-2:]), c_im.reshape((-1,) + c_im.shape[-2:])
    g = a_re.shape[0]
    gpb = LANES // GROUP_CH
    tc = chunk * GROUP_CH
    tl = max(tc, LANES)
    ldt = jnp.broadcast_to(log_dt[:, None], (g, p))
    pcol = jnp.stack([a_re, a_im, ldt], axis=-1)
    prow = jnp.stack([a_re, a_im, ldt], axis=1)
    bt_re = jnp.tile(jnp.swapaxes(b_re, 1, 2), (1, chunk, 1))
    bt_im = jnp.tile(jnp.swapaxes(b_im, 1, 2), (1, chunk, 1))
    ct_re = jnp.tile(jnp.swapaxes(c_re, 1, 2), (1, 1, tl // GROUP_CH))
    ct_im = jnp.tile(jnp.swapaxes(c_im, 1, 2), (1, 1, tl // GROUP_CH))

    def spec(a, b):
        return pl.BlockSpec((gpb, a, b), lambda i: (i, 0, 0))

    return pl.pallas_call(
        functools.partial(_s5_prep_kernel, chunk=chunk),
        grid=(g // gpb,),
        in_specs=[spec(p, 3), spec(3, p), spec(tc, p), spec(tc, p), spec(p, tl), spec(p, tl)],
        out_specs=[spec(tc, tc), spec(tc, p), spec(tc, p), spec(p, tc), spec(p, tc), spec(2, p)],
        out_shape=[
            jax.ShapeDtypeStruct((g, tc, tc), BF16),
            jax.ShapeDtypeStruct((g, tc, p), BF16),
            jax.ShapeDtypeStruct((g, tc, p), BF16),
            jax.ShapeDtypeStruct((g, p, tc), BF16),
            jax.ShapeDtypeStruct((g, p, tc), BF16),
            jax.ShapeDtypeStruct((g, 2, p), F32),
        ],
        compiler_params=_params("parallel"),
        name=f"s5_prep_t{chunk}",
    )(pcol, prow, bt_re, bt_im, ct_re, ct_im)


def _s5_scan_kernel(u_ref, toep_ref, wsr_ref, wsi_ref, wor_ref, woi_ref, at_ref, h0r_ref, h0i_ref,
                    y_ref, hr_ref, hi_ref, sr_s, si_s, pr_s, pi_s, *, n_chunks, bsz):
    u = u_ref[0]
    y_intra = _dot(u, toep_ref[0])
    sr_s[...] = _dot(u, wsr_ref[0])
    si_s[...] = _dot(u, wsi_ref[0])
    ar = at_ref[0, 0:1, :]
    ai = at_ref[0, 1:2, :]

    def body(k, carry):
        hr, hi = carry
        rows = pl.ds(pl.multiple_of(k * bsz, bsz), bsz)
        pr_s[rows, :] = hr
        pi_s[rows, :] = hi
        return (ar * hr - ai * hi + sr_s[rows, :], ar * hi + ai * hr + si_s[rows, :])

    hr, hi = lax.fori_loop(0, n_chunks, body, (h0r_ref[0], h0i_ref[0]))
    hr_ref[0] = hr
    hi_ref[0] = hi
    y_ref[0] = (y_intra + _dot(pr_s[...].astype(BF16), wor_ref[0])
                + _dot(pi_s[...].astype(BF16), woi_ref[0]))


def _s5_scan(u_l, ops, h0r, h0i, *, layer, n_chunks, bsz):
    toep, wsr, wsi, wor, woi, at = ops
    g, n, tc = u_l.shape
    p = STATE_DIM

    def spec(a, b):
        return pl.BlockSpec((1, a, b), lambda i: (i, 0, 0))

    def ospec(a, b):
        return pl.BlockSpec((1, a, b), lambda i: (layer * g + i, 0, 0))

    return pl.pallas_call(
        functools.partial(_s5_scan_kernel, n_chunks=n_chunks, bsz=bsz),
        grid=(g,),
        in_specs=[spec(n, tc), ospec(tc, tc), ospec(tc, p), ospec(tc, p), ospec(p, tc), ospec(p, tc),
                  ospec(2, p), spec(bsz, p), spec(bsz, p)],
        out_specs=[spec(n, tc), spec(bsz, p), spec(bsz, p)],
        out_shape=[
            jax.ShapeDtypeStruct((g, n, tc), F32),
            jax.ShapeDtypeStruct((g, bsz, p), F32),
            jax.ShapeDtypeStruct((g, bsz, p), F32),
        ],
        scratch_shapes=[pltpu.VMEM((n, p), F32)] * 4,
        compiler_params=_params("parallel"),
        name=f"s5_scan_n{n}",
    )(u_l, toep, wsr, wsi, wor, woi, at, h0r, h0i)


def _block_transpose8(xs):
    lane = lax.broadcasted_iota(jnp.int32, xs[0].shape, 1)
    for d in (4, 2, 1):
        sh = d * GROUP_CH
        keep = (lane & sh) == 0
        nxt = list(xs)
        for i in range(8):
            if i & d == 0:
                lo, hi = xs[i], xs[i + d]
                nxt[i] = jnp.where(keep, lo, pltpu.roll(hi, sh, 1))
                nxt[i + d] = jnp.where(keep, pltpu.roll(lo, LANES - sh, 1), hi)
        xs = nxt
    return xs


def _s5_scan_fused_kernel(u_ref, toep_ref, wsr_ref, wsi_ref, wor_ref, woi_ref, at_ref, h0r_ref, h0i_ref, d_ref,
                          a_ref, hr_ref, hi_ref, hr_s, hi_s, sr_s, si_s, pr_s, pi_s, y_s, *, bsz):
    r = pl.program_id(1)
    _, n_t, rows, _ = u_ref.shape
    gpb = toep_ref.shape[0]
    n_half = n_t // 8

    @pl.when(r == 0)
    def _():
        hr_s[...] = h0r_ref[...]
        hi_s[...] = h0i_ref[...]

    halves = [_block_transpose8([u_ref[0, h * 8 + tl] for tl in range(8)]) for h in range(n_half)]
    for g in range(gpb):
        u_g = jnp.concatenate([hv[g] for hv in halves], axis=1).astype(BF16)
        y_s[g] = _dot(u_g, toep_ref[g])
        sr_s[g] = _dot(u_g, wsr_ref[g])
        si_s[g] = _dot(u_g, wsi_ref[g])

    ar = at_ref[:, 0:1, :]
    ai = at_ref[:, 1:2, :]

    def body(k, carry):
        hr, hi = carry
        rs = pl.ds(pl.multiple_of(k * bsz, bsz), bsz)
        pr_s[:, rs, :] = hr
        pi_s[:, rs, :] = hi
        return (ar * hr - ai * hi + sr_s[:, rs, :], ar * hi + ai * hr + si_s[:, rs, :])

    hr, hi = lax.fori_loop(0, rows // bsz, body, (hr_s[...], hi_s[...]))
    hr_s[...] = hr
    hi_s[...] = hi

    @pl.when(r == pl.num_programs(1) - 1)
    def _():
        hr_ref[...] = hr
        hi_ref[...] = hi

    ys = []
    for g in range(gpb):
        ys.append(y_s[g] + _dot(pr_s[g].astype(BF16), wor_ref[g]) + _dot(pi_s[g].astype(BF16), woi_ref[g]))
    d = d_ref[...]
    for h in range(n_half):
        zs = _block_transpose8([y[:, h * LANES:(h + 1) * LANES] for y in ys])
        for tl in range(8):
            t = h * 8 + tl
            a_ref[t] = jax.nn.gelu(zs[tl] + d * u_ref[0, t]).astype(BF16)


def _s5_scan_fused(uz4, ops, h0r, h0i, d_skip, *, layer, bsz, rows):
    toep, wsr, wsi, wor, woi, at = ops
    _, n_t, n, width = uz4.shape
    g = width // GROUP_CH
    tc = toep.shape[1]
    p = STATE_DIM
    gpb = LANES // GROUP_CH
    nblk = g // gpb
    assert n_t % 8 == 0 and tc == n_t * GROUP_CH and n % rows == 0 and rows % bsz == 0

    def gspec(a, b):
        return pl.BlockSpec((gpb, a, b), lambda i, r: (i, 0, 0))

    def ospec(a, b):
        return pl.BlockSpec((gpb, a, b), lambda i, r: (layer * nblk + i, 0, 0))

    return pl.pallas_call(
        functools.partial(_s5_scan_fused_kernel, bsz=bsz),
        grid=(nblk, n // rows),
        in_specs=[
            pl.BlockSpec((1, n_t, rows, LANES), lambda i, r: (0, 0, r, i)),
            ospec(tc, tc), ospec(tc, p), ospec(tc, p), ospec(p, tc), ospec(p, tc), ospec(2, p),
            gspec(bsz, p), gspec(bsz, p),
            pl.BlockSpec((1, LANES), lambda i, r: (0, i)),
        ],
        out_specs=[
            pl.BlockSpec((n_t, rows, LANES), lambda i, r: (0, r, i)),
            gspec(bsz, p), gspec(bsz, p),
        ],
        out_shape=[
            jax.ShapeDtypeStruct((n_t, n, width), BF16),
            jax.ShapeDtypeStruct((g, bsz, p), F32),
            jax.ShapeDtypeStruct((g, bsz, p), F32),
        ],
        scratch_shapes=[pltpu.VMEM((gpb, bsz, p), F32)] * 2 + [pltpu.VMEM((gpb, rows, p), F32)] * 4
        + [pltpu.VMEM((gpb, rows, tc), F32)],
        compiler_params=_params("parallel", "arbitrary"),
        name="s5_scan_fused",
    )(uz4, toep, wsr, wsi, wor, woi, at, h0r, h0i, d_skip)


def _s5_act_kernel(y_ref, u_ref, d_ref, a_ref):
    a_ref[...] = jax.nn.gelu(y_ref[...] + d_ref[...] * u_ref[0]).astype(BF16)


def _s5_act(y, uz, d_skip):
    m, w = y.shape
    return pl.pallas_call(
        _s5_act_kernel,
        grid=(1,),
        in_specs=[
            pl.BlockSpec((m, w), lambda i: (0, 0)),
            pl.BlockSpec((1, m, w), lambda i: (0, 0, 0)),
            pl.BlockSpec((1, w), lambda i: (0, 0)),
        ],
        out_specs=pl.BlockSpec((m, w), lambda i: (0, 0)),
        out_shape=jax.ShapeDtypeStruct((m, w), BF16),
        compiler_params=_params("arbitrary"),
        name="s5_act_sample",
    )(y, uz, d_skip)


def _s5_glu_out_kernel(a_ref, z_ref, x_ref, wa_ref, wb_ref, ba_ref, bb_ref, wo_ref, o_ref, acc_s):
    j = pl.program_id(1)
    a = a_ref[...]
    ga = _dot(a, wa_ref[0]) + ba_ref[0]
    gb = _dot(a, wb_ref[0]) + bb_ref[0]
    z = z_ref[0]
    v = (ga * jax.nn.sigmoid(gb) * (z * jax.nn.sigmoid(z))).astype(BF16)
    part = _dot(v, wo_ref[...])

    @pl.when(j == 0)
    def _():
        acc_s[...] = part

    @pl.when(j > 0)
    def _():
        acc_s[...] += part

    @pl.when(j == pl.num_programs(1) - 1)
    def _():
        o_ref[...] = x_ref[...] + acc_s[...]


def _s5_glu_out(a, uz, x, w_glu2, b_glu2, w_out, *, tm, tn, name):
    m, w = a.shape
    d = x.shape[1]
    tm = min(tm, m)
    return pl.pallas_call(
        _s5_glu_out_kernel,
        grid=(m // tm, w // tn),
        in_specs=[
            pl.BlockSpec((tm, w), lambda i, j: (i, 0)),
            pl.BlockSpec((1, tm, tn), lambda i, j: (1, i, j)),
            pl.BlockSpec((tm, d), lambda i, j: (i, 0)),
            pl.BlockSpec((1, w, tn), lambda i, j: (0, 0, j)),
            pl.BlockSpec((1, w, tn), lambda i, j: (1, 0, j)),
            pl.BlockSpec((1, 1, tn), lambda i, j: (0, 0, j)),
            pl.BlockSpec((1, 1, tn), lambda i, j: (1, 0, j)),
            pl.BlockSpec((tn, d), lambda i, j: (j, 0)),
        ],
        out_specs=pl.BlockSpec((tm, d), lambda i, j: (i, 0)),
        out_shape=jax.ShapeDtypeStruct((m, d), F32),
        scratch_shapes=[pltpu.VMEM((tm, d), F32)],
        compiler_params=_params("parallel", "arbitrary"),
        name=name,
    )(a, uz, x, w_glu2, w_glu2, b_glu2, b_glu2, w_out)


def _latent_kernel(x_ref, g_ref, wl_ref, wr_ref, ws_ref, gl_ref, cc_ref, ss_ref, lat_ref, kr_ref, kcat_ref):
    h = _rms(x_ref[...], g_ref[...]).astype(BF16)
    lat = _rms(_dot(h, wl_ref[...]), gl_ref[...])
    kr = _dot(h, wr_ref[...]) * cc_ref[...] + _dot(h, ws_ref[...]) * ss_ref[...]
    lat_ref[...] = lat
    kr_ref[...] = kr[:, :QK_ROPE]
    kcat_ref[:, :KV_LORA] = lat.astype(BF16)
    kcat_ref[:, KV_LORA:] = kr.astype(BF16)


def _latent(x, g, wl, wr, ws, gl, cc, ss, *, tm, name):
    m, d = x.shape
    tm = min(tm, m)
    nt = cc.shape[0] // tm

    def full(a):
        return pl.BlockSpec(a.shape, lambda i: (0,) * a.ndim)

    return pl.pallas_call(
        _latent_kernel,
        grid=(m // tm,),
        in_specs=[
            pl.BlockSpec((tm, d), lambda i: (i, 0)),
            full(g), full(wl), full(wr), full(ws), full(gl),
            pl.BlockSpec((tm, LANES), lambda i: (i % nt, 0)),
            pl.BlockSpec((tm, LANES), lambda i: (i % nt, 0)),
        ],
        out_specs=[
            pl.BlockSpec((tm, KV_LORA), lambda i: (i, 0)),
            pl.BlockSpec((tm, QK_ROPE), lambda i: (i, 0)),
            pl.BlockSpec((tm, QK_PAD), lambda i: (i, 0)),
        ],
        out_shape=[
            jax.ShapeDtypeStruct((m, KV_LORA), F32),
            jax.ShapeDtypeStruct((m, QK_ROPE), F32),
            jax.ShapeDtypeStruct((m, QK_PAD), BF16),
        ],
        compiler_params=_params("parallel"),
        name=name,
    )(x, g, wl, wr, ws, gl, cc, ss)


def _mla_front_kernel(x_ref, g_ref, wcq_ref, wg_ref, gq_ref, wqn_ref, wqr_ref, wqs_ref, wuk_ref,
                      cc_ref, ss_ref, q_ref, sg_ref):
    h = _rms(x_ref[...], g_ref[...]).astype(BF16)
    cq = _rms(_dot(h, wcq_ref[...]), gq_ref[...]).astype(BF16)
    gate = _dot(h, wg_ref[...])
    sg_ref[...] = gate * jax.nn.sigmoid(gate)
    qn = _dot(cq, wqn_ref[...]).astype(BF16)
    qr = _dot(cq, wqr_ref[...])
    qs = _dot(cq, wqs_ref[...])
    cc = cc_ref[...]
    ss = ss_ref[...]
    for hd in range(N_HEADS):
        sl = slice(hd * LANES, (hd + 1) * LANES)
        q_ref[0, hd, :, :KV_LORA] = _dot(qn[:, sl], wuk_ref[hd]).astype(BF16)
        q_ref[0, hd, :, KV_LORA:] = (qr[:, sl] * cc + qs[:, sl] * ss).astype(BF16)


def _mla_front(x, g, wcq, wg, gq, wqn, wqr, wqs, wuk, cc, ss, *, tm, name):
    m, d = x.shape
    tm = min(tm, m)
    nt = cc.shape[0] // tm

    def full(a):
        return pl.BlockSpec(a.shape, lambda i: (0,) * a.ndim)

    return pl.pallas_call(
        _mla_front_kernel,
        grid=(m // tm,),
        in_specs=[
            pl.BlockSpec((tm, d), lambda i: (i, 0)),
            full(g), full(wcq), full(wg), full(gq), full(wqn), full(wqr), full(wqs), full(wuk),
            pl.BlockSpec((tm, LANES), lambda i: (i % nt, 0)),
            pl.BlockSpec((tm, LANES), lambda i: (i % nt, 0)),
        ],
        out_specs=[
            pl.BlockSpec((1, N_HEADS, tm, QK_PAD), lambda i: (i, 0, 0, 0)),
            pl.BlockSpec((tm, N_HEADS * V_HEAD), lambda i: (i, 0)),
        ],
        out_shape=[
            jax.ShapeDtypeStruct((m // tm, N_HEADS, tm, QK_PAD), BF16),
            jax.ShapeDtypeStruct((m, N_HEADS * V_HEAD), F32),
        ],
        compiler_params=_params("parallel"),
        name=name,
    )(x, g, wcq, wg, gq, wqn, wqr, wqs, wuk, cc, ss)


def _lane_tile(x, width):
    return x if width == LANES else jnp.concatenate([x] * (width // LANES), axis=1)


def _softmax_update(s, kl, m_ref, l_ref, acc_ref, rows):
    m_prev = m_ref[rows, :]
    m_new = jnp.maximum(m_prev, jnp.max(s, axis=1, keepdims=True))
    alpha = jnp.exp2(m_prev - m_new)
    p = jnp.exp2(s - _lane_tile(m_new, s.shape[1]))
    l_ref[rows, :] = alpha * l_ref[rows, :] + jnp.sum(p, axis=1, keepdims=True)
    acc_ref[rows, :] = _lane_tile(alpha, kl.shape[1]) * acc_ref[rows, :] + _dot(p.astype(BF16), kl)
    m_ref[rows, :] = m_new


def _attn_prompt_kernel(q_ref, k_ref, o_ref, m_s, l_s, acc_s, *, tq, tk, heads_per_chunk):
    qi = pl.program_id(1)
    ki = pl.program_id(2)
    diag = (qi * tq + (tq - 1)) // tk
    rc = heads_per_chunk * tq

    @pl.when(ki == 0)
    def _():
        m_s[...] = jnp.full(m_s.shape, NEG_INF, F32)
        l_s[...] = jnp.zeros(l_s.shape, F32)
        acc_s[...] = jnp.zeros(acc_s.shape, F32)

    def update(masked):
        k = k_ref[...]
        kl = k[:, :KV_LORA]
        for c in range(N_HEADS // heads_per_chunk):
            q = q_ref[0, c * heads_per_chunk:(c + 1) * heads_per_chunk].reshape(rc, QK_PAD)
            s = _dot_nt(q, k) * SCALE_LOG2E
            if masked:
                tok = qi * tq + (lax.broadcasted_iota(jnp.int32, (rc, tk), 0) & (tq - 1))
                col = ki * tk + lax.broadcasted_iota(jnp.int32, (rc, tk), 1)
                s = jnp.where(col <= tok, s, NEG_INF)
            _softmax_update(s, kl, m_s, l_s, acc_s, pl.ds(c * rc, rc))

    @pl.when(ki < diag)
    def _():
        update(False)

    @pl.when(ki == diag)
    def _():
        update(True)
        for hd in range(N_HEADS):
            rows = pl.ds(hd * tq, tq)
            o_ref[0, hd] = (acc_s[rows, :] / _lane_tile(l_s[rows, :], KV_LORA)).astype(BF16)


def _attn_prompt(q, kcat, *, bsz, seq, tq, tk):
    assert tq & (tq - 1) == 0 and seq % tq == 0 and seq % tk == 0
    nq = seq // tq
    nk = seq // tk
    rows = N_HEADS * tq
    return pl.pallas_call(
        functools.partial(_attn_prompt_kernel, tq=tq, tk=tk, heads_per_chunk=2),
        grid=(bsz, nq, nk),
        in_specs=[
            pl.BlockSpec((1, N_HEADS, tq, QK_PAD), lambda b, i, j: (b * nq + i, 0, 0, 0)),
            pl.BlockSpec((tk, QK_PAD), lambda b, i, j: (b * nk + jnp.minimum(j, (i * tq + (tq - 1)) // tk), 0)),
        ],
        out_specs=pl.BlockSpec((1, N_HEADS, tq, KV_LORA), lambda b, i, j: (b * nq + i, 0, 0, 0)),
        out_shape=jax.ShapeDtypeStruct((bsz * nq, N_HEADS, tq, KV_LORA), BF16),
        scratch_shapes=[
            pltpu.VMEM((rows, LANES), F32),
            pltpu.VMEM((rows, LANES), F32),
            pltpu.VMEM((rows, KV_LORA), F32),
        ],
        compiler_params=_params("parallel", "parallel", "arbitrary"),
        name="attn_prompt",
    )(q, kcat)


def _attn_sample_kernel(pt_ref, q_ref, *refs, n_pg, dec_seq):
    lat_refs = refs[:n_pg]
    krt_refs = refs[n_pg:2 * n_pg]
    nl_ref, nk_ref, o_ref, kl_s, kr_s, m_s, l_s, acc_s = refs[2 * n_pg:]
    step = pl.program_id(1)

    @pl.when(step == 0)
    def _():
        m_s[...] = jnp.full(m_s.shape, NEG_INF, F32)
        l_s[...] = jnp.zeros(l_s.shape, F32)
        acc_s[...] = jnp.zeros(acc_s.shape, F32)

    for i in range(n_pg):
        kl_s[i * PAGE_SIZE:(i + 1) * PAGE_SIZE, :] = lat_refs[i][0].astype(BF16)
        kr_s[:, i * PAGE_SIZE:(i + 1) * PAGE_SIZE] = krt_refs[i][0].astype(BF16)

    q = q_ref[0]
    ql = q[:, :KV_LORA]
    qr = q[:, KV_LORA:KV_LORA + QK_ROPE]
    kl = kl_s[...]
    s = (_dot_nt(ql, kl) + _dot(qr, kr_s[...])) * SCALE_LOG2E
    _softmax_update(s, kl, m_s, l_s, acc_s, slice(None))

    @pl.when(step == pl.num_programs(1) - 1)
    def _():
        nl = nl_ref[0].astype(BF16).astype(F32)
        nk = nk_ref[0].astype(BF16).astype(F32)
        qlf = ql.astype(F32)
        qrf = qr.astype(F32)
        tok = lax.broadcasted_iota(jnp.int32, (q.shape[0], 1), 0) & (dec_seq - 1)
        m = m_s[...]
        l = l_s[...]
        acc = acc_s[...]
        for j in range(dec_seq):
            sj = (jnp.sum(qlf * nl[j:j + 1], axis=-1, keepdims=True)
                  + jnp.sum(qrf * nk[j:j + 1], axis=-1, keepdims=True)) * SCALE_LOG2E
            sj = jnp.where(tok >= j, sj, NEG_INF)
            m_new = jnp.maximum(m, sj)
            alpha = jnp.exp2(m - m_new)
            pj = jnp.exp2(sj - m_new)
            l = alpha * l + pj
            pv = _lane_tile(pj, KV_LORA).astype(BF16).astype(F32) * nl[j:j + 1]
            acc = _lane_tile(alpha, KV_LORA) * acc + pv
            m = m_new
        o_ref[0] = (acc / _lane_tile(l, KV_LORA)).astype(BF16)


def _attn_sample(q, cache_latent, cache_krope_t, page_table, new_lat, new_kr, *, n_pg):
    n_seq, n_pages = page_table.shape
    rows = q.shape[1]
    dec_seq = new_lat.shape[1]
    assert dec_seq & (dec_seq - 1) == 0 and n_pages % n_pg == 0
    steps = n_pages // n_pg
    pt = page_table.reshape(-1)

    def page_spec(a, b, i):
        return pl.BlockSpec((1, a, b), lambda n, s, pt_ref: (pt_ref[n * n_pages + s * n_pg + i], 0, 0))

    def seq_spec(a, b):
        return pl.BlockSpec((1, a, b), lambda n, s, pt_ref: (n, 0, 0))

    grid_spec = pltpu.PrefetchScalarGridSpec(
        num_scalar_prefetch=1,
        grid=(n_seq, steps),
        in_specs=([seq_spec(rows, QK_PAD)]
                  + [page_spec(PAGE_SIZE, KV_LORA, i) for i in range(n_pg)]
                  + [page_spec(QK_ROPE, PAGE_SIZE, i) for i in range(n_pg)]
                  + [seq_spec(dec_seq, KV_LORA), seq_spec(dec_seq, QK_ROPE)]),
        out_specs=seq_spec(rows, KV_LORA),
        scratch_shapes=[
            pltpu.VMEM((n_pg * PAGE_SIZE, KV_LORA), BF16),
            pltpu.VMEM((QK_ROPE, n_pg * PAGE_SIZE), BF16),
            pltpu.VMEM((rows, LANES), F32),
            pltpu.VMEM((rows, LANES), F32),
            pltpu.VMEM((rows, KV_LORA), F32),
        ],
    )
    return pl.pallas_call(
        functools.partial(_attn_sample_kernel, n_pg=n_pg, dec_seq=dec_seq),
        grid_spec=grid_spec,
        out_shape=jax.ShapeDtypeStruct((n_seq, rows, KV_LORA), BF16),
        compiler_params=_params("parallel", "arbitrary"),
        name="attn_sample",
    )(pt, q, *([cache_latent] * n_pg), *([cache_krope_t] * n_pg), new_lat, new_kr)


def _mla_back_kernel(o_ref, sg_ref, x_ref, wuv_ref, wo_ref, gf_ref, out_ref, *, final_norm):
    parts = []
    for hd in range(N_HEADS):
        oh = _dot(o_ref[0, hd], wuv_ref[hd])
        parts.append((oh * sg_ref[:, hd * V_HEAD:(hd + 1) * V_HEAD]).astype(BF16))
    y = x_ref[...] + _dot(jnp.concatenate(parts, axis=1), wo_ref[...])
    out_ref[...] = _rms(y, gf_ref[...]) if final_norm else y


def _mla_back(o, sg, x, wuv, wo, gf, *, final_norm, name):
    m, d = x.shape
    tm = o.shape[2]

    def full(a):
        return pl.BlockSpec(a.shape, lambda i: (0,) * a.ndim)

    def rows(a):
        return pl.BlockSpec((tm, a.shape[1]), lambda i: (i, 0))

    return pl.pallas_call(
        functools.partial(_mla_back_kernel, final_norm=final_norm),
        grid=(m // tm,),
        in_specs=[pl.BlockSpec((1,) + o.shape[1:], lambda i: (i, 0, 0, 0)), rows(sg), rows(x), full(wuv), full(wo),
                  full(gf)],
        out_specs=pl.BlockSpec((tm, d), lambda i: (i, 0)),
        out_shape=jax.ShapeDtypeStruct((m, d), F32),
        compiler_params=_params("parallel"),
        name=name,
    )(o, sg, x, wuv, wo, gf)


def _to_scan_layout(u, bsz, seq, chunk):
    k = seq // chunk
    g = u.shape[1] // GROUP_CH
    u5 = u.astype(BF16).reshape(bsz, k, chunk, g, GROUP_CH)
    return u5.transpose(3, 1, 0, 2, 4).reshape(g, k * bsz, chunk * GROUP_CH)


def _from_scan_layout(y, bsz, seq, chunk):
    k = seq // chunk
    g = y.shape[0]
    y5 = y.reshape(g, k, bsz, chunk, GROUP_CH)
    return y5.transpose(2, 1, 3, 0, 4).reshape(bsz * seq, g * GROUP_CH)


def _rope_tables(pos):
    half = QK_ROPE // 2
    inv = ROPE_THETA ** (-jnp.arange(half, dtype=F32) / half)
    ang = pos.astype(F32)[:, None] * inv[None, :]
    cos = jnp.cos(ang)
    sin = jnp.sin(ang)
    pad = jnp.zeros((pos.shape[0], LANES - QK_ROPE), F32)
    cc = jnp.concatenate([cos, cos, pad], axis=1)
    ss = jnp.concatenate([-sin, sin, pad], axis=1)
    return cc, ss


def _swap_halves(w):
    half = w.shape[-1] // 2
    return jnp.concatenate([w[..., half:], w[..., :half]], axis=-1)


def _pad_lanes(w):
    pad = [(0, 0)] * (w.ndim - 1) + [(0, LANES - w.shape[-1])]
    return jnp.pad(w, pad)


def kernel(x_prompt, x_sample, cache_latent, cache_krope, page_table, state_ssm_re, state_ssm_im, norm_a, w_in_a, a_re, a_im, log_dt, b_re, b_im, c_re, c_im, d_skip, w_glu, b_glu, w_out_a, norm_kv, w_dkv, norm_latent, w_uk, w_uv, norm_b, w_in_b, norm_q, w_uq, w_out_b, norm_f):
    bsz, seq, d = x_prompt.shape
    dbs, dseq, _ = x_sample.shape
    n_a = norm_a.shape[0]
    n_b = norm_b.shape[0]
    width = d_skip.shape[1]
    n_groups = width // GROUP_CH
    past_len = page_table.shape[1] * PAGE_SIZE

    n_chunks = seq // PROMPT_CHUNK
    xp = x_prompt.reshape(bsz, n_chunks, PROMPT_CHUNK, d).transpose(2, 1, 0, 3).reshape(bsz * seq, d)
    xs = x_sample.reshape(dbs * dseq, d)

    hp_re, hp_im, hs_re, hs_im = [], [], [], []
    zeros_p = jnp.zeros((n_groups, bsz, STATE_DIM), F32)
    ops_p = _s5_prep(a_re, a_im, log_dt, b_re, b_im, c_re, c_im, chunk=PROMPT_CHUNK)
    ops_s = _s5_prep(a_re, a_im, log_dt, b_re, b_im, c_re, c_im, chunk=dseq)
    for i in range(n_a):
        w_in3 = w_in_a[i].astype(BF16).reshape(d, 2, width).transpose(1, 0, 2)
        w_glu2 = w_glu[i].astype(BF16).reshape(width, 2, width).transpose(1, 0, 2)
        b_glu2 = b_glu[i].reshape(2, 1, width)
        w_out = w_out_a[i].astype(BF16)
        g = norm_a[i][None, :]
        dsk = d_skip[i][None, :]

        uz = _norm_matmul(xp, g, w_in3, tm=512, tn=1024, name="s5_in_prompt")
        uz4 = uz.reshape(2, PROMPT_CHUNK, n_chunks * bsz, width)
        a, hr, hi = _s5_scan_fused(uz4, ops_p, zeros_p, zeros_p, dsk, layer=i, bsz=bsz, rows=SCAN_ROWS)
        xp = _s5_glu_out(a.reshape(bsz * seq, width), uz, xp, w_glu2, b_glu2, w_out, tm=1024, tn=512,
                         name="s5_glu_out_prompt")
        hp_re.append(hr.transpose(1, 0, 2))
        hp_im.append(hi.transpose(1, 0, 2))

        uz = _norm_matmul(xs, g, w_in3, tm=512, tn=1024, name="s5_in_sample")
        u_l = _to_scan_layout(uz[0], dbs, dseq, dseq)
        y_l, hr, hi = _s5_scan(u_l, ops_s, state_ssm_re[i].transpose(1, 0, 2), state_ssm_im[i].transpose(1, 0, 2),
                               layer=i, n_chunks=1, bsz=dbs)
        y = _from_scan_layout(y_l, dbs, dseq, dseq)
        a = _s5_act(y, uz, dsk)
        xs = _s5_glu_out(a, uz, xs, w_glu2, b_glu2, w_out, tm=512, tn=512, name="s5_glu_out_sample")
        hs_re.append(hr.transpose(1, 0, 2))
        hs_im.append(hi.transpose(1, 0, 2))

    xp = xp.reshape(PROMPT_CHUNK, n_chunks, bsz, d).transpose(2, 1, 0, 3).reshape(bsz * seq, d)

    cc_p, ss_p = _rope_tables(jnp.arange(seq, dtype=jnp.int32))
    pos_s = past_len + jnp.arange(dseq, dtype=jnp.int32)
    cc_s, ss_s = _rope_tables(jnp.tile(pos_s, dbs))
    w_lat = w_dkv[:, :KV_LORA].astype(BF16)
    w_kr = _pad_lanes(w_dkv[:, KV_LORA:]).astype(BF16)
    w_ks = _pad_lanes(_swap_halves(w_dkv[:, KV_LORA:])).astype(BF16)
    g_kv = norm_kv[None, :]
    g_lat = norm_latent[None, :]
    lat_p, kr_p, kcat_p = _latent(xp, g_kv, w_lat, w_kr, w_ks, g_lat, cc_p, ss_p, tm=512, name="latent_prompt")
    lat_s, kr_s, _ = _latent(xs, g_kv, w_lat, w_kr, w_ks, g_lat, cc_s, ss_s, tm=512, name="latent_sample")
    new_lat = lat_s.reshape(dbs, dseq, KV_LORA)
    new_kr = kr_s.reshape(dbs, dseq, QK_ROPE)

    cache_krope_t = jnp.swapaxes(cache_krope, 1, 2)
    w_ukt = w_uk.astype(BF16).reshape(KV_LORA, N_HEADS, QK_NOPE).transpose(1, 2, 0)
    w_uvh = w_uv.astype(BF16).reshape(KV_LORA, N_HEADS, V_HEAD).transpose(1, 0, 2)
    g_f = norm_f[None, :]
    assert n_b > 0
    for j in range(n_b):
        g = norm_b[j][None, :]
        w_cq = w_in_b[j][:, :Q_LORA].astype(BF16)
        w_gate = w_in_b[j][:, Q_LORA:].astype(BF16)
        g_q = norm_q[j][None, :]
        wq3 = w_uq[j].reshape(Q_LORA, N_HEADS, QK_NOPE + QK_ROPE)
        w_qn = wq3[:, :, :QK_NOPE].reshape(Q_LORA, N_HEADS * QK_NOPE).astype(BF16)
        w_qr = _pad_lanes(wq3[:, :, QK_NOPE:]).reshape(Q_LORA, N_HEADS * LANES).astype(BF16)
        w_qs = _pad_lanes(_swap_halves(wq3[:, :, QK_NOPE:])).reshape(Q_LORA, N_HEADS * LANES).astype(BF16)
        w_o = w_out_b[j].astype(BF16)

        q, sg = _mla_front(xp, g, w_cq, w_gate, g_q, w_qn, w_qr, w_qs, w_ukt, cc_p, ss_p, tm=ATTN_TQ,
                           name="mla_front_prompt")
        o = _attn_prompt(q, kcat_p, bsz=bsz, seq=seq, tq=ATTN_TQ, tk=ATTN_TK)
        last = j == n_b - 1
        xp = _mla_back(o, sg, xp, w_uvh, w_o, g_f, final_norm=last, name="mla_back_prompt")

        q, sg = _mla_front(xs, g, w_cq, w_gate, g_q, w_qn, w_qr, w_qs, w_ukt, cc_s, ss_s, tm=ATTN_TQ,
                           name="mla_front_sample")
        spt = ATTN_TQ // dseq
        q_seq = q.reshape(-1, N_HEADS, spt, dseq, QK_PAD).transpose(0, 2, 1, 3, 4)
        o = _attn_sample(q_seq.reshape(dbs, N_HEADS * dseq, QK_PAD), cache_latent, cache_krope_t, page_table,
                         new_lat, new_kr, n_pg=SAMPLE_PAGES_PER_STEP)
        o = o.reshape(-1, spt, N_HEADS, dseq, KV_LORA).transpose(0, 2, 1, 3, 4)
        xs = _mla_back(o.reshape(-1, N_HEADS, ATTN_TQ, KV_LORA), sg, xs, w_uvh, w_o, g_f, final_norm=last,
                       name="mla_back_sample")

    return (xp.reshape(bsz, seq, d), xs.reshape(dbs, dseq, d),
            lat_p.reshape(bsz, seq, KV_LORA), kr_p.reshape(bsz, seq, QK_ROPE),
            new_lat, new_kr,
            jnp.stack(hp_re), jnp.stack(hp_im), jnp.stack(hs_re), jnp.stack(hs_im))
```

```python
import functools
import math

import jax
import jax.numpy as jnp
from jax import lax
from jax.experimental import pallas as pl
from jax.experimental.pallas import tpu as pltpu

F32 = jnp.float32
BF16 = jnp.bfloat16

GROUP_CH = 16
STATE_DIM = 64
N_HEADS = 8
QK_NOPE = 128
QK_ROPE = 64
V_HEAD = 128
KV_LORA = 256
Q_LORA = 384
PAGE_SIZE = 128
ROPE_THETA = 10000.0
RMS_EPS = 1e-6
SOFTMAX_SCALE = 1.0 / math.sqrt(QK_NOPE + QK_ROPE)
SCALE_LOG2E = SOFTMAX_SCALE * math.log2(math.e)
NEG_INF = -1e30
LANES = 128
QK_PAD = KV_LORA + LANES
PROMPT_CHUNK = 16
ATTN_TQ = 256
ATTN_TK = 512
ATTN_HEADS_PER_CHUNK = 4
SAMPLE_PAGES_PER_STEP = 32
SCAN_ROWS = 512
GLU_ROW_SPLIT = 4
VMEM_LIMIT = 56 * 1024 * 1024


def _dot(a, b):
    return jnp.dot(a, b, preferred_element_type=F32)


def _dot_nt(a, b):
    return lax.dot_general(a, b, (((1,), (1,)), ((), ())), preferred_element_type=F32)


def _dot_f32(a, b):
    return jnp.dot(a, b, preferred_element_type=F32, precision=lax.Precision.HIGHEST)


def _rms(x, g):
    return x * lax.rsqrt(jnp.mean(x * x, axis=-1, keepdims=True) + RMS_EPS) * g


def _params(*sem):
    return pltpu.CompilerParams(dimension_semantics=sem, vmem_limit_bytes=VMEM_LIMIT)


def _norm_matmul_kernel(x_ref, g_ref, w_ref, o_ref, *, tn):
    h = _rms(x_ref[...], g_ref[...]).astype(BF16)
    parts, _, n = w_ref.shape
    for p in range(parts):
        for j in range(n // tn):
            cols = slice(j * tn, (j + 1) * tn)
            o_ref[p, :, cols] = _dot(h, w_ref[p, :, cols])


def _norm_matmul(x, g, w3, *, tm, tn, name):
    m, d = x.shape
    parts, _, n = w3.shape
    tm = min(tm, m)
    return pl.pallas_call(
        functools.partial(_norm_matmul_kernel, tn=tn),
        grid=(m // tm,),
        in_specs=[
            pl.BlockSpec((tm, d), lambda i: (i, 0)),
            pl.BlockSpec((1, d), lambda i: (0, 0)),
            pl.BlockSpec((parts, d, n), lambda i: (0, 0, 0), pipeline_mode=pl.Buffered(1)),
        ],
        out_specs=pl.BlockSpec((parts, tm, n), lambda i: (0, i, 0)),
        out_shape=jax.ShapeDtypeStruct((parts, m, n), F32),
        compiler_params=_params("parallel"),
        name=name,
    )(x, g, w3)


def _discretize(lam_re, lam_im, log_dt):
    dt = jnp.exp(log_dt)
    mag = jnp.exp(lam_re * dt)
    lb_re = mag * jnp.cos(lam_im * dt)
    lb_im = mag * jnp.sin(lam_im * dt)
    den = lam_re * lam_re + lam_im * lam_im
    nr = lb_re - 1.0
    f_re = (nr * lam_re + lb_im * lam_im) / den
    f_im = (lb_im * lam_re - nr * lam_im) / den
    return lb_re, lb_im, f_re, f_im


def _cpow(br, bi, e, nbits, shape):
    br = jnp.broadcast_to(br, shape)
    bi = jnp.broadcast_to(bi, shape)
    rr = jnp.ones(shape, F32)
    ri = jnp.zeros(shape, F32)
    for j in range(nbits):
        bit = ((e >> j) & 1) == 1
        nr = rr * br - ri * bi
        ni = rr * bi + ri * br
        rr = jnp.where(bit, nr, rr)
        ri = jnp.where(bit, ni, ri)
        if j + 1 < nbits:
            br, bi = br * br - bi * bi, 2.0 * br * bi
    return rr, ri


def _s5_prep_kernel(*refs, chunk):
    for g in range(refs[0].shape[0]):
        _s5_prep_group(g, *refs, chunk=chunk)


def _s5_prep_group(g, pcol_ref, prow_ref, btr_ref, bti_ref, ctr_ref, cti_ref,
                   toep_ref, wsr_ref, wsi_ref, wor_ref, woi_ref, at_ref, *, chunk):
    tc = chunk * GROUP_CH
    tl = ctr_ref.shape[2]
    nbits = chunk.bit_length()
    shift = GROUP_CH.bit_length() - 1

    pc = pcol_ref[g]
    lbr_c, lbi_c, _, _ = _discretize(pc[:, 0:1], pc[:, 1:2], pc[:, 2:3])
    pr = prow_ref[g]
    lbr_r, lbi_r, f_re, f_im = _discretize(pr[0:1], pr[1:2], pr[2:3])

    cr = ctr_ref[g]
    ci = cti_ref[g]
    shape = cr.shape
    lane_t = lax.broadcasted_iota(jnp.int32, shape, 1) >> shift
    p0r, p0i = _cpow(lbr_c, lbi_c, lane_t, nbits, shape)
    p1r = p0r * lbr_c - p0i * lbi_c
    p1i = p0r * lbi_c + p0i * lbr_c
    rr = cr * p0r - ci * p0i
    ri = cr * p0i + ci * p0r
    wor_ref[g] = (cr * p1r - ci * p1i)[:, :tc].astype(BF16)
    woi_ref[g] = (-(cr * p1i + ci * p1r))[:, :tc].astype(BF16)

    btr = btr_ref[g]
    bti = bti_ref[g]
    bbr = f_re * btr - f_im * bti
    bbi = f_re * bti + f_im * btr

    krow = _dot_f32(bbr[:GROUP_CH], rr) - _dot_f32(bbi[:GROUP_CH], ri)
    lane = lax.broadcasted_iota(jnp.int32, (GROUP_CH, tl), 1)
    for s in range(chunk):
        if s == 0:
            blk = krow
        else:
            blk = jnp.where(lane >= GROUP_CH * s, pltpu.roll(krow, GROUP_CH * s, 1), 0.0)
        toep_ref[g, GROUP_CH * s:GROUP_CH * (s + 1), :] = blk[:, :tc].astype(BF16)

    shape_s = btr.shape
    row_e = (chunk - 1) - (lax.broadcasted_iota(jnp.int32, shape_s, 0) >> shift)
    qr, qi = _cpow(lbr_r, lbi_r, row_e, nbits, shape_s)
    wsr_ref[g] = (bbr * qr - bbi * qi).astype(BF16)
    wsi_ref[g] = (bbr * qi + bbi * qr).astype(BF16)

    ar, ai = lbr_r, lbi_r
    for _ in range(chunk.bit_length() - 1):
        ar, ai = ar * ar - ai * ai, 2.0 * ar * ai
    at_ref[g, 0:1, :] = ar
    at_ref[g, 1:2, :] = ai


def _s5_prep(a_re, a_im, log_dt, b_re, b_im, c_re, c_im, *, chunk):
    p = a_re.shape[-1]
    a_re, a_im = a_re.reshape(-1, p), a_im.reshape(-1, p)
    log_dt = log_dt.reshape(-1)
    b_re, b_im = b_re.reshape((-1,) + b_re.shape[-2:]), b_im.reshape((-1,) + b_im.shape[-2:])
    c_re, c_im = c_re.reshape((-1,) + c_re.shape[-2:]), c_im.reshape((-1,) + c_im.shape[-2:])
    g = a_re.shape[0]
    gpb = LANES // GROUP_CH
    tc = chunk * GROUP_CH
    tl = max(tc, LANES)
    ldt = jnp.broadcast_to(log_dt[:, None], (g, p))
    pcol = jnp.stack([a_re, a_im, ldt], axis=-1)
    prow = jnp.stack([a_re, a_im, ldt], axis=1)
    bt_re = jnp.tile(jnp.swapaxes(b_re, 1, 2), (1, chunk, 1))
    bt_im = jnp.tile(jnp.swapaxes(b_im, 1, 2), (1, chunk, 1))
    ct_re = jnp.tile(jnp.swapaxes(c_re, 1, 2), (1, 1, tl // GROUP_CH))
    ct_im = jnp.tile(jnp.swapaxes(c_im, 1, 2), (1, 1, tl // GROUP_CH))

    def spec(a, b):
        return pl.BlockSpec((gpb, a, b), lambda i: (i, 0, 0))

    return pl.pallas_call(
        functools.partial(_s5_prep_kernel, chunk=chunk),
        grid=(g // gpb,),
        in_specs=[spec(p, 3), spec(3, p), spec(tc, p), spec(tc, p), spec(p, tl), spec(p, tl)],
        out_specs=[spec(tc, tc), spec(tc, p), spec(tc, p), spec(p, tc), spec(p, tc), spec(2, p)],
        out_shape=[
            jax.ShapeDtypeStruct((g, tc, tc), BF16),
            jax.ShapeDtypeStruct((g, tc, p), BF16),
            jax.ShapeDtypeStruct((g, tc, p), BF16),
            jax.ShapeDtypeStruct((g, p, tc), BF16),
            jax.ShapeDtypeStruct((g, p, tc), BF16),
            jax.ShapeDtypeStruct((g, 2, p), F32),
        ],
        compiler_params=_params("parallel"),
        name=f"s5_prep_t{chunk}",
    )(pcol, prow, bt_re, bt_im, ct_re, ct_im)


def _s5_scan_kernel(u_ref, toep_ref, wsr_ref, wsi_ref, wor_ref, woi_ref, at_ref, h0r_ref, h0i_ref,
                    y_ref, hr_ref, hi_ref, sr_s, si_s, pr_s, pi_s, *, n_chunks, bsz):
    u = u_ref[0]
    y_intra = _dot(u, toep_ref[0])
    sr_s[...] = _dot(u, wsr_ref[0])
    si_s[...] = _dot(u, wsi_ref[0])
    ar = at_ref[0, 0:1, :]
    ai = at_ref[0, 1:2, :]

    def body(k, carry):
        hr, hi = carry
        rows = pl.ds(pl.multiple_of(k * bsz, bsz), bsz)
        pr_s[rows, :] = hr
        pi_s[rows, :] = hi
        return (ar * hr - ai * hi + sr_s[rows, :], ar * hi + ai * hr + si_s[rows, :])

    hr, hi = lax.fori_loop(0, n_chunks, body, (h0r_ref[0], h0i_ref[0]))
    hr_ref[0] = hr
    hi_ref[0] = hi
    y_ref[0] = (y_intra + _dot(pr_s[...].astype(BF16), wor_ref[0])
                + _dot(pi_s[...].astype(BF16), woi_ref[0]))


def _s5_scan(u_l, ops, h0r, h0i, *, layer, n_chunks, bsz):
    toep, wsr, wsi, wor, woi, at = ops
    g, n, tc = u_l.shape
    p = STATE_DIM

    def spec(a, b):
        return pl.BlockSpec((1, a, b), lambda i: (i, 0, 0))

    def ospec(a, b):
        return pl.BlockSpec((1, a, b), lambda i: (layer * g + i, 0, 0))

    return pl.pallas_call(
        functools.partial(_s5_scan_kernel, n_chunks=n_chunks, bsz=bsz),
        grid=(g,),
        in_specs=[spec(n, tc), ospec(tc, tc), ospec(tc, p), ospec(tc, p), ospec(p, tc), ospec(p, tc),
                  ospec(2, p), spec(bsz, p), spec(bsz, p)],
        out_specs=[spec(n, tc), spec(bsz, p), spec(bsz, p)],
        out_shape=[
            jax.ShapeDtypeStruct((g, n, tc), F32),
            jax.ShapeDtypeStruct((g, bsz, p), F32),
            jax.ShapeDtypeStruct((g, bsz, p), F32),
        ],
        scratch_shapes=[pltpu.VMEM((n, p), F32)] * 4,
        compiler_params=_params("parallel"),
        name=f"s5_scan_n{n}",
    )(u_l, toep, wsr, wsi, wor, woi, at, h0r, h0i)


def _block_transpose8(xs):
    lane = lax.broadcasted_iota(jnp.int32, xs[0].shape, 1)
    for d in (4, 2, 1):
        sh = d * GROUP_CH
        keep = (lane & sh) == 0
        nxt = list(xs)
        for i in range(8):
            if i & d == 0:
                lo, hi = xs[i], xs[i + d]
                nxt[i] = jnp.where(keep, lo, pltpu.roll(hi, sh, 1))
                nxt[i + d] = jnp.where(keep, pltpu.roll(lo, LANES - sh, 1), hi)
        xs = nxt
    return xs


def _s5_scan_fused_kernel(u_ref, toep_ref, wsr_ref, wsi_ref, wor_ref, woi_ref, at_ref, h0r_ref, h0i_ref, d_ref,
                          a_ref, hr_ref, hi_ref, hr_s, hi_s, sr_s, si_s, pr_s, pi_s, y_s, *, bsz):
    r = pl.program_id(1)
    _, n_t, rows, _ = u_ref.shape
    gpb = toep_ref.shape[0]
    n_half = n_t // 8

    @pl.when(r == 0)
    def _():
        hr_s[...] = h0r_ref[...]
        hi_s[...] = h0i_ref[...]

    halves = [_block_transpose8([u_ref[0, h * 8 + tl] for tl in range(8)]) for h in range(n_half)]
    for g in range(gpb):
        u_g = jnp.concatenate([hv[g] for hv in halves], axis=1).astype(BF16)
        y_s[g] = _dot(u_g, toep_ref[g])
        sr_s[g] = _dot(u_g, wsr_ref[g])
        si_s[g] = _dot(u_g, wsi_ref[g])

    ar = at_ref[:, 0:1, :]
    ai = at_ref[:, 1:2, :]

    def body(k, carry):
        hr, hi = carry
        rs = pl.ds(pl.multiple_of(k * bsz, bsz), bsz)
        pr_s[:, rs, :] = hr
        pi_s[:, rs, :] = hi
        return (ar * hr - ai * hi + sr_s[:, rs, :], ar * hi + ai * hr + si_s[:, rs, :])

    hr, hi = lax.fori_loop(0, rows // bsz, body, (hr_s[...], hi_s[...]))
    hr_s[...] = hr
    hi_s[...] = hi

    @pl.when(r == pl.num_programs(1) - 1)
    def _():
        hr_ref[...] = hr
        hi_ref[...] = hi

    ys = []
    for g in range(gpb):
        ys.append(y_s[g] + _dot(pr_s[g].astype(BF16), wor_ref[g]) + _dot(pi_s[g].astype(BF16), woi_ref[g]))
    d = d_ref[...]
    for h in range(n_half):
        zs = _block_transpose8([y[:, h * LANES:(h + 1) * LANES] for y in ys])
        for tl in range(8):
            t = h * 8 + tl
            a_ref[t] = jax.nn.gelu(zs[tl] + d * u_ref[0, t]).astype(BF16)


def _s5_scan_fused(uz4, ops, h0r, h0i, d_skip, *, layer, bsz, rows):
    toep, wsr, wsi, wor, woi, at = ops
    _, n_t, n, width = uz4.shape
    g = width // GROUP_CH
    tc = toep.shape[1]
    p = STATE_DIM
    gpb = LANES // GROUP_CH
    nblk = g // gpb
    assert n_t % 8 == 0 and tc == n_t * GROUP_CH and n % rows == 0 and rows % bsz == 0

    def gspec(a, b):
        return pl.BlockSpec((gpb, a, b), lambda i, r: (i, 0, 0))

    def ospec(a, b):
        return pl.BlockSpec((gpb, a, b), lambda i, r: (layer * nblk + i, 0, 0))

    return pl.pallas_call(
        functools.partial(_s5_scan_fused_kernel, bsz=bsz),
        grid=(nblk, n // rows),
        in_specs=[
            pl.BlockSpec((1, n_t, rows, LANES), lambda i, r: (0, 0, r, i)),
            ospec(tc, tc), ospec(tc, p), ospec(tc, p), ospec(p, tc), ospec(p, tc), ospec(2, p),
            gspec(bsz, p), gspec(bsz, p),
            pl.BlockSpec((1, LANES), lambda i, r: (0, i)),
        ],
        out_specs=[
            pl.BlockSpec((n_t, rows, LANES), lambda i, r: (0, r, i)),
            gspec(bsz, p), gspec(bsz, p),
        ],
        out_shape=[
            jax.ShapeDtypeStruct((n_t, n, width), BF16),
            jax.ShapeDtypeStruct((g, bsz, p), F32),
            jax.ShapeDtypeStruct((g, bsz, p), F32),
        ],
        scratch_shapes=[pltpu.VMEM((gpb, bsz, p), F32)] * 2 + [pltpu.VMEM((gpb, rows, p), F32)] * 4
        + [pltpu.VMEM((gpb, rows, tc), F32)],
        compiler_params=_params("parallel", "arbitrary"),
        name="s5_scan_fused",
    )(uz4, toep, wsr, wsi, wor, woi, at, h0r, h0i, d_skip)


def _s5_act_kernel(y_ref, u_ref, d_ref, a_ref):
    a_ref[...] = jax.nn.gelu(y_ref[...] + d_ref[...] * u_ref[0]).astype(BF16)


def _s5_act(y, uz, d_skip):
    m, w = y.shape
    return pl.pallas_call(
        _s5_act_kernel,
        grid=(1,),
        in_specs=[
            pl.BlockSpec((m, w), lambda i: (0, 0)),
            pl.BlockSpec((1, m, w), lambda i: (0, 0, 0)),
            pl.BlockSpec((1, w), lambda i: (0, 0)),
        ],
        out_specs=pl.BlockSpec((m, w), lambda i: (0, 0)),
        out_shape=jax.ShapeDtypeStruct((m, w), BF16),
        compiler_params=_params("arbitrary"),
        name="s5_act_sample",
    )(y, uz, d_skip)


def _s5_glu_out_kernel(a_ref, z_ref, x_ref, wa_ref, wb_ref, ba_ref, bb_ref, wo_ref, o_ref, acc_s):
    j = pl.program_id(1)

    @pl.when(j == 0)
    def _():
        acc_s[...] = jnp.zeros(acc_s.shape, F32)

    rs = a_ref.shape[0] // GLU_ROW_SPLIT
    for h in range(GLU_ROW_SPLIT):
        rows = slice(h * rs, (h + 1) * rs)
        a = a_ref[rows, :]
        ga = _dot(a, wa_ref[0]) + ba_ref[0]
        gb = _dot(a, wb_ref[0]) + bb_ref[0]
        z = z_ref[0, rows, :]
        v = (ga * jax.nn.sigmoid(gb) * (z * jax.nn.sigmoid(z))).astype(BF16)
        acc_s[rows, :] += _dot(v, wo_ref[...])

    @pl.when(j == pl.num_programs(1) - 1)
    def _():
        o_ref[...] = x_ref[...] + acc_s[...]


def _s5_glu_out(a, uz, x, w_glu2, b_glu2, w_out, *, tm, tn, name):
    m, w = a.shape
    d = x.shape[1]
    tm = min(tm, m)
    return pl.pallas_call(
        _s5_glu_out_kernel,
        grid=(m // tm, w // tn),
        in_specs=[
            pl.BlockSpec((tm, w), lambda i, j: (i, 0)),
            pl.BlockSpec((1, tm, tn), lambda i, j: (1, i, j)),
            pl.BlockSpec((tm, d), lambda i, j: (i, 0)),
            pl.BlockSpec((1, w, tn), lambda i, j: (0, 0, j)),
            pl.BlockSpec((1, w, tn), lambda i, j: (1, 0, j)),
            pl.BlockSpec((1, 1, tn), lambda i, j: (0, 0, j)),
            pl.BlockSpec((1, 1, tn), lambda i, j: (1, 0, j)),
            pl.BlockSpec((tn, d), lambda i, j: (j, 0)),
        ],
        out_specs=pl.BlockSpec((tm, d), lambda i, j: (i, 0)),
        out_shape=jax.ShapeDtypeStruct((m, d), F32),
        scratch_shapes=[pltpu.VMEM((tm, d), F32)],
        compiler_params=_params("parallel", "arbitrary"),
        name=name,
    )(a, uz, x, w_glu2, w_glu2, b_glu2, b_glu2, w_out)


def _latent_kernel(x_ref, g_ref, wl_ref, wr_ref, ws_ref, gl_ref, cc_ref, ss_ref, lat_ref, kr_ref, kcat_ref):
    h = _rms(x_ref[...], g_ref[...]).astype(BF16)
    lat = _rms(_dot(h, wl_ref[...]), gl_ref[...])
    kr = _dot(h, wr_ref[...]) * cc_ref[...] + _dot(h, ws_ref[...]) * ss_ref[...]
    lat_ref[...] = lat
    kr_ref[...] = kr[:, :QK_ROPE]
    kcat_ref[:, :KV_LORA] = lat.astype(BF16)
    kcat_ref[:, KV_LORA:] = kr.astype(BF16)


def _latent(x, g, wl, wr, ws, gl, cc, ss, *, tm, name):
    m, d = x.shape
    tm = min(tm, m)
    nt = cc.shape[0] // tm

    def full(a):
        return pl.BlockSpec(a.shape, lambda i: (0,) * a.ndim)

    return pl.pallas_call(
        _latent_kernel,
        grid=(m // tm,),
        in_specs=[
            pl.BlockSpec((tm, d), lambda i: (i, 0)),
            full(g), full(wl), full(wr), full(ws), full(gl),
            pl.BlockSpec((tm, LANES), lambda i: (i % nt, 0)),
            pl.BlockSpec((tm, LANES), lambda i: (i % nt, 0)),
        ],
        out_specs=[
            pl.BlockSpec((tm, KV_LORA), lambda i: (i, 0)),
            pl.BlockSpec((tm, QK_ROPE), lambda i: (i, 0)),
            pl.BlockSpec((tm, QK_PAD), lambda i: (i, 0)),
        ],
        out_shape=[
            jax.ShapeDtypeStruct((m, KV_LORA), F32),
            jax.ShapeDtypeStruct((m, QK_ROPE), F32),
            jax.ShapeDtypeStruct((m, QK_PAD), BF16),
        ],
        compiler_params=_params("parallel"),
        name=name,
    )(x, g, wl, wr, ws, gl, cc, ss)


def _mla_front_kernel(x_ref, g_ref, wcq_ref, wg_ref, gq_ref, wqn_ref, wqr_ref, wqs_ref, wuk_ref,
                      cc_ref, ss_ref, q_ref, sg_ref):
    h = _rms(x_ref[...], g_ref[...]).astype(BF16)
    cq = _rms(_dot(h, wcq_ref[...]), gq_ref[...]).astype(BF16)
    gate = _dot(h, wg_ref[...])
    sg_ref[...] = gate * jax.nn.sigmoid(gate)
    qn = _dot(cq, wqn_ref[...]).astype(BF16)
    qr = _dot(cq, wqr_ref[...])
    qs = _dot(cq, wqs_ref[...])
    cc = cc_ref[...]
    ss = ss_ref[...]
    for hd in range(N_HEADS):
        sl = slice(hd * LANES, (hd + 1) * LANES)
        q_ref[0, hd, :, :KV_LORA] = _dot(qn[:, sl], wuk_ref[hd]).astype(BF16)
        q_ref[0, hd, :, KV_LORA:] = (qr[:, sl] * cc + qs[:, sl] * ss).astype(BF16)


def _mla_front(x, g, wcq, wg, gq, wqn, wqr, wqs, wuk, cc, ss, *, tm, name):
    m, d = x.shape
    tm = min(tm, m)
    nt = cc.shape[0] // tm

    def full(a):
        return pl.BlockSpec(a.shape, lambda i: (0,) * a.ndim)

    return pl.pallas_call(
        _mla_front_kernel,
        grid=(m // tm,),
        in_specs=[
            pl.BlockSpec((tm, d), lambda i: (i, 0)),
            full(g), full(wcq), full(wg), full(gq), full(wqn), full(wqr), full(wqs), full(wuk),
            pl.BlockSpec((tm, LANES), lambda i: (i % nt, 0)),
            pl.BlockSpec((tm, LANES), lambda i: (i % nt, 0)),
        ],
        out_specs=[
            pl.BlockSpec((1, N_HEADS, tm, QK_PAD), lambda i: (i, 0, 0, 0)),
            pl.BlockSpec((tm, N_HEADS * V_HEAD), lambda i: (i, 0)),
        ],
        out_shape=[
            jax.ShapeDtypeStruct((m // tm, N_HEADS, tm, QK_PAD), BF16),
            jax.ShapeDtypeStruct((m, N_HEADS * V_HEAD), F32),
        ],
        compiler_params=_params("parallel"),
        name=name,
    )(x, g, wcq, wg, gq, wqn, wqr, wqs, wuk, cc, ss)


def _lane_tile(x, width):
    return x if width == LANES else jnp.concatenate([x] * (width // LANES), axis=1)


def _softmax_update(s, kl, m_ref, l_ref, acc_ref, rows):
    m_prev = m_ref[rows, :]
    m_new = jnp.maximum(m_prev, jnp.max(s, axis=1, keepdims=True))
    alpha = jnp.exp2(m_prev - m_new)
    p = jnp.exp2(s - _lane_tile(m_new, s.shape[1]))
    l_ref[rows, :] = alpha * l_ref[rows, :] + jnp.sum(p, axis=1, keepdims=True)
    acc_ref[rows, :] = _lane_tile(alpha, kl.shape[1]) * acc_ref[rows, :] + _dot(p.astype(BF16), kl)
    m_ref[rows, :] = m_new


def _attn_prompt_kernel(q_ref, k_ref, o_ref, m_s, l_s, acc_s, *, tq, tk, heads_per_chunk):
    qi = pl.program_id(1)
    ki = pl.program_id(2)
    diag = (qi * tq + (tq - 1)) // tk
    rc = heads_per_chunk * tq

    @pl.when(ki == 0)
    def _():
        m_s[...] = jnp.full(m_s.shape, NEG_INF, F32)
        l_s[...] = jnp.zeros(l_s.shape, F32)
        acc_s[...] = jnp.zeros(acc_s.shape, F32)

    def update(masked):
        k = k_ref[...]
        kl = k[:, :KV_LORA]
        for c in range(N_HEADS // heads_per_chunk):
            q = q_ref[0, c * heads_per_chunk:(c + 1) * heads_per_chunk].reshape(rc, QK_PAD)
            s = _dot_nt(q, k) * SCALE_LOG2E
            if masked:
                tok = qi * tq + (lax.broadcasted_iota(jnp.int32, (rc, tk), 0) & (tq - 1))
                col = ki * tk + lax.broadcasted_iota(jnp.int32, (rc, tk), 1)
                s = jnp.where(col <= tok, s, NEG_INF)
            _softmax_update(s, kl, m_s, l_s, acc_s, pl.ds(c * rc, rc))

    @pl.when(ki < diag)
    def _():
        update(False)

    @pl.when(ki == diag)
    def _():
        update(True)
        for hd in range(N_HEADS):
            rows = pl.ds(hd * tq, tq)
            o_ref[0, hd] = (acc_s[rows, :] / _lane_tile(l_s[rows, :], KV_LORA)).astype(BF16)


def _attn_prompt(q, kcat, *, bsz, seq, tq, tk):
    assert tq & (tq - 1) == 0 and seq % tq == 0 and seq % tk == 0
    nq = seq // tq
    nk = seq // tk
    rows = N_HEADS * tq
    return pl.pallas_call(
        functools.partial(_attn_prompt_kernel, tq=tq, tk=tk, heads_per_chunk=ATTN_HEADS_PER_CHUNK),
        grid=(bsz, nq, nk),
        in_specs=[
            pl.BlockSpec((1, N_HEADS, tq, QK_PAD), lambda b, i, j: (b * nq + i, 0, 0, 0)),
            pl.BlockSpec((tk, QK_PAD), lambda b, i, j: (b * nk + jnp.minimum(j, (i * tq + (tq - 1)) // tk), 0)),
        ],
        out_specs=pl.BlockSpec((1, N_HEADS, tq, KV_LORA), lambda b, i, j: (b * nq + i, 0, 0, 0)),
        out_shape=jax.ShapeDtypeStruct((bsz * nq, N_HEADS, tq, KV_LORA), BF16),
        scratch_shapes=[
            pltpu.VMEM((rows, LANES), F32),
            pltpu.VMEM((rows, LANES), F32),
            pltpu.VMEM((rows, KV_LORA), F32),
        ],
        compiler_params=_params("parallel", "parallel", "arbitrary"),
        name="attn_prompt",
    )(q, kcat)


def _attn_sample_kernel(pt_ref, q_ref, nl_ref, nk_ref, lat_hbm, krt_hbm, o_ref,
                        lat_buf, krt_buf, sems, kl_s, kr_s, m_s, l_s, acc_s, *, n_pg, dec_seq):
    step = pl.program_id(1)
    steps = pl.num_programs(1)
    gstep = pl.program_id(0) * steps + step
    total = pl.num_programs(0) * steps
    slot = gstep & 1

    def page_copies(at, buf_slot):
        out = []
        for i in range(n_pg):
            page = pt_ref[at * n_pg + i]
            out.append(pltpu.make_async_copy(lat_hbm.at[page], lat_buf.at[buf_slot, i], sems.at[0, buf_slot]))
            out.append(pltpu.make_async_copy(krt_hbm.at[page], krt_buf.at[buf_slot, i], sems.at[1, buf_slot]))
        return out

    @pl.when(gstep == 0)
    def _():
        for c in page_copies(0, 0):
            c.start()

    @pl.when(gstep + 1 < total)
    def _():
        for c in page_copies(gstep + 1, 1 - slot):
            c.start()

    @pl.when(step == 0)
    def _():
        m_s[...] = jnp.full(m_s.shape, NEG_INF, F32)
        l_s[...] = jnp.zeros(l_s.shape, F32)
        acc_s[...] = jnp.zeros(acc_s.shape, F32)

    for c in page_copies(gstep, slot):
        c.wait()

    for i in range(n_pg):
        kl_s[i * PAGE_SIZE:(i + 1) * PAGE_SIZE, :] = lat_buf[slot, i].astype(BF16)
        kr_s[:, i * PAGE_SIZE:(i + 1) * PAGE_SIZE] = krt_buf[slot, i].astype(BF16)

    q = q_ref[0]
    ql = q[:, :KV_LORA]
    qr = q[:, KV_LORA:KV_LORA + QK_ROPE]
    kl = kl_s[...]
    s = (_dot_nt(ql, kl) + _dot(qr, kr_s[...])) * SCALE_LOG2E
    _softmax_update(s, kl, m_s, l_s, acc_s, slice(None))

    @pl.when(step == pl.num_programs(1) - 1)
    def _():
        nl = nl_ref[0].astype(BF16).astype(F32)
        nk = nk_ref[0].astype(BF16).astype(F32)
        qlf = ql.astype(F32)
        qrf = qr.astype(F32)
        tok = lax.broadcasted_iota(jnp.int32, (q.shape[0], 1), 0) & (dec_seq - 1)
        m = m_s[...]
        l = l_s[...]
        acc = acc_s[...]
        for j in range(dec_seq):
            sj = (jnp.sum(qlf * nl[j:j + 1], axis=-1, keepdims=True)
                  + jnp.sum(qrf * nk[j:j + 1], axis=-1, keepdims=True)) * SCALE_LOG2E
            sj = jnp.where(tok >= j, sj, NEG_INF)
            m_new = jnp.maximum(m, sj)
            alpha = jnp.exp2(m - m_new)
            pj = jnp.exp2(sj - m_new)
            l = alpha * l + pj
            pv = _lane_tile(pj, KV_LORA).astype(BF16).astype(F32) * nl[j:j + 1]
            acc = _lane_tile(alpha, KV_LORA) * acc + pv
            m = m_new
        o_ref[0] = (acc / _lane_tile(l, KV_LORA)).astype(BF16)


def _attn_sample(q, cache_latent, cache_krope_t, page_table, new_lat, new_kr, *, n_pg):
    n_seq, n_pages = page_table.shape
    rows = q.shape[1]
    dec_seq = new_lat.shape[1]
    assert dec_seq & (dec_seq - 1) == 0 and n_pages % n_pg == 0
    steps = n_pages // n_pg
    pt = page_table.reshape(-1)

    def seq_spec(a, b):
        return pl.BlockSpec((1, a, b), lambda n, s, pt_ref: (n, 0, 0))

    grid_spec = pltpu.PrefetchScalarGridSpec(
        num_scalar_prefetch=1,
        grid=(n_seq, steps),
        in_specs=[seq_spec(rows, QK_PAD), seq_spec(dec_seq, KV_LORA), seq_spec(dec_seq, QK_ROPE),
                  pl.BlockSpec(memory_space=pl.ANY), pl.BlockSpec(memory_space=pl.ANY)],
        out_specs=seq_spec(rows, KV_LORA),
        scratch_shapes=[
            pltpu.VMEM((2, n_pg, PAGE_SIZE, KV_LORA), F32),
            pltpu.VMEM((2, n_pg, QK_ROPE, PAGE_SIZE), F32),
            pltpu.SemaphoreType.DMA((2, 2)),
            pltpu.VMEM((n_pg * PAGE_SIZE, KV_LORA), BF16),
            pltpu.VMEM((QK_ROPE, n_pg * PAGE_SIZE), BF16),
            pltpu.VMEM((rows, LANES), F32),
            pltpu.VMEM((rows, LANES), F32),
            pltpu.VMEM((rows, KV_LORA), F32),
        ],
    )
    return pl.pallas_call(
        functools.partial(_attn_sample_kernel, n_pg=n_pg, dec_seq=dec_seq),
        grid_spec=grid_spec,
        out_shape=jax.ShapeDtypeStruct((n_seq, rows, KV_LORA), BF16),
        compiler_params=_params("arbitrary", "arbitrary"),
        name="attn_sample",
    )(pt, q, new_lat, new_kr, cache_latent, cache_krope_t)


def _mla_back_kernel(o_ref, sg_ref, x_ref, wuv_ref, wo_ref, gf_ref, out_ref, *, final_norm):
    parts = []
    for hd in range(N_HEADS):
        oh = _dot(o_ref[0, hd], wuv_ref[hd])
        parts.append((oh * sg_ref[:, hd * V_HEAD:(hd + 1) * V_HEAD]).astype(BF16))
    y = x_ref[...] + _dot(jnp.concatenate(parts, axis=1), wo_ref[...])
    out_ref[...] = _rms(y, gf_ref[...]) if final_norm else y


def _mla_back(o, sg, x, wuv, wo, gf, *, final_norm, name):
    m, d = x.shape
    tm = o.shape[2]

    def full(a):
        return pl.BlockSpec(a.shape, lambda i: (0,) * a.ndim)

    def rows(a):
        return pl.BlockSpec((tm, a.shape[1]), lambda i: (i, 0))

    return pl.pallas_call(
        functools.partial(_mla_back_kernel, final_norm=final_norm),
        grid=(m // tm,),
        in_specs=[pl.BlockSpec((1,) + o.shape[1:], lambda i: (i, 0, 0, 0)), rows(sg), rows(x), full(wuv), full(wo),
                  full(gf)],
        out_specs=pl.BlockSpec((tm, d), lambda i: (i, 0)),
        out_shape=jax.ShapeDtypeStruct((m, d), F32),
        compiler_params=_params("parallel"),
        name=name,
    )(o, sg, x, wuv, wo, gf)


def _to_scan_layout(u, bsz, seq, chunk):
    k = seq // chunk
    g = u.shape[1] // GROUP_CH
    u5 = u.astype(BF16).reshape(bsz, k, chunk, g, GROUP_CH)
    return u5.transpose(3, 1, 0, 2, 4).reshape(g, k * bsz, chunk * GROUP_CH)


def _from_scan_layout(y, bsz, seq, chunk):
    k = seq // chunk
    g = y.shape[0]
    y5 = y.reshape(g, k, bsz, chunk, GROUP_CH)
    return y5.transpose(2, 1, 3, 0, 4).reshape(bsz * seq, g * GROUP_CH)


def _rope_tables(pos):
    half = QK_ROPE // 2
    inv = ROPE_THETA ** (-jnp.arange(half, dtype=F32) / half)
    ang = pos.astype(F32)[:, None] * inv[None, :]
    cos = jnp.cos(ang)
    sin = jnp.sin(ang)
    pad = jnp.zeros((pos.shape[0], LANES - QK_ROPE), F32)
    cc = jnp.concatenate([cos, cos, pad], axis=1)
    ss = jnp.concatenate([-sin, sin, pad], axis=1)
    return cc, ss


def _swap_halves(w):
    half = w.shape[-1] // 2
    return jnp.concatenate([w[..., half:], w[..., :half]], axis=-1)


def _pad_lanes(w):
    pad = [(0, 0)] * (w.ndim - 1) + [(0, LANES - w.shape[-1])]
    return jnp.pad(w, pad)


def kernel(x_prompt, x_sample, cache_latent, cache_krope, page_table, state_ssm_re, state_ssm_im, norm_a, w_in_a, a_re, a_im, log_dt, b_re, b_im, c_re, c_im, d_skip, w_glu, b_glu, w_out_a, norm_kv, w_dkv, norm_latent, w_uk, w_uv, norm_b, w_in_b, norm_q, w_uq, w_out_b, norm_f):
    bsz, seq, d = x_prompt.shape
    dbs, dseq, _ = x_sample.shape
    n_a = norm_a.shape[0]
    n_b = norm_b.shape[0]
    width = d_skip.shape[1]
    n_groups = width // GROUP_CH
    past_len = page_table.shape[1] * PAGE_SIZE

    n_chunks = seq // PROMPT_CHUNK
    xp = x_prompt.reshape(bsz, n_chunks, PROMPT_CHUNK, d).transpose(2, 1, 0, 3).reshape(bsz * seq, d)
    xs = x_sample.reshape(dbs * dseq, d)

    hp_re, hp_im, hs_re, hs_im = [], [], [], []
    zeros_p = jnp.zeros((n_groups, bsz, STATE_DIM), F32)
    ops_p = _s5_prep(a_re, a_im, log_dt, b_re, b_im, c_re, c_im, chunk=PROMPT_CHUNK)
    ops_s = _s5_prep(a_re, a_im, log_dt, b_re, b_im, c_re, c_im, chunk=dseq)
    for i in range(n_a):
        w_in3 = w_in_a[i].astype(BF16).reshape(d, 2, width).transpose(1, 0, 2)
        w_glu2 = w_glu[i].astype(BF16).reshape(width, 2, width).transpose(1, 0, 2)
        b_glu2 = b_glu[i].reshape(2, 1, width)
        w_out = w_out_a[i].astype(BF16)
        g = norm_a[i][None, :]
        dsk = d_skip[i][None, :]

        uz = _norm_matmul(xp, g, w_in3, tm=512, tn=1024, name="s5_in_prompt")
        uz4 = uz.reshape(2, PROMPT_CHUNK, n_chunks * bsz, width)
        a, hr, hi = _s5_scan_fused(uz4, ops_p, zeros_p, zeros_p, dsk, layer=i, bsz=bsz, rows=SCAN_ROWS)
        xp = _s5_glu_out(a.reshape(bsz * seq, width), uz, xp, w_glu2, b_glu2, w_out, tm=1024, tn=512,
                         name="s5_glu_out_prompt")
        hp_re.append(hr.transpose(1, 0, 2))
        hp_im.append(hi.transpose(1, 0, 2))

        uz = _norm_matmul(xs, g, w_in3, tm=512, tn=1024, name="s5_in_sample")
        u_l = _to_scan_layout(uz[0], dbs, dseq, dseq)
        y_l, hr, hi = _s5_scan(u_l, ops_s, state_ssm_re[i].transpose(1, 0, 2), state_ssm_im[i].transpose(1, 0, 2),
                               layer=i, n_chunks=1, bsz=dbs)
        y = _from_scan_layout(y_l, dbs, dseq, dseq)
        a = _s5_act(y, uz, dsk)
        xs = _s5_glu_out(a, uz, xs, w_glu2, b_glu2, w_out, tm=512, tn=512, name="s5_glu_out_sample")
        hs_re.append(hr.transpose(1, 0, 2))
        hs_im.append(hi.transpose(1, 0, 2))

    xp = xp.reshape(PROMPT_CHUNK, n_chunks, bsz, d).transpose(2, 1, 0, 3).reshape(bsz * seq, d)

    cc_p, ss_p = _rope_tables(jnp.arange(seq, dtype=jnp.int32))
    pos_s = past_len + jnp.arange(dseq, dtype=jnp.int32)
    cc_s, ss_s = _rope_tables(jnp.tile(pos_s, dbs))
    w_lat = w_dkv[:, :KV_LORA].astype(BF16)
    w_kr = _pad_lanes(w_dkv[:, KV_LORA:]).astype(BF16)
    w_ks = _pad_lanes(_swap_halves(w_dkv[:, KV_LORA:])).astype(BF16)
    g_kv = norm_kv[None, :]
    g_lat = norm_latent[None, :]
    lat_p, kr_p, kcat_p = _latent(xp, g_kv, w_lat, w_kr, w_ks, g_lat, cc_p, ss_p, tm=512, name="latent_prompt")
    lat_s, kr_s, _ = _latent(xs, g_kv, w_lat, w_kr, w_ks, g_lat, cc_s, ss_s, tm=512, name="latent_sample")
    new_lat = lat_s.reshape(dbs, dseq, KV_LORA)
    new_kr = kr_s.reshape(dbs, dseq, QK_ROPE)

    cache_krope_t = jnp.swapaxes(cache_krope, 1, 2)
    w_ukt = w_uk.astype(BF16).reshape(KV_LORA, N_HEADS, QK_NOPE).transpose(1, 2, 0)
    w_uvh = w_uv.astype(BF16).reshape(KV_LORA, N_HEADS, V_HEAD).transpose(1, 0, 2)
    g_f = norm_f[None, :]
    assert n_b > 0
    for j in range(n_b):
        g = norm_b[j][None, :]
        w_cq = w_in_b[j][:, :Q_LORA].astype(BF16)
        w_gate = w_in_b[j][:, Q_LORA:].astype(BF16)
        g_q = norm_q[j][None, :]
        wq3 = w_uq[j].reshape(Q_LORA, N_HEADS, QK_NOPE + QK_ROPE)
        w_qn = wq3[:, :, :QK_NOPE].reshape(Q_LORA, N_HEADS * QK_NOPE).astype(BF16)
        w_qr = _pad_lanes(wq3[:, :, QK_NOPE:]).reshape(Q_LORA, N_HEADS * LANES).astype(BF16)
        w_qs = _pad_lanes(_swap_halves(wq3[:, :, QK_NOPE:])).reshape(Q_LORA, N_HEADS * LANES).astype(BF16)
        w_o = w_out_b[j].astype(BF16)

        q, sg = _mla_front(xp, g, w_cq, w_gate, g_q, w_qn, w_qr, w_qs, w_ukt, cc_p, ss_p, tm=ATTN_TQ,
                           name="mla_front_prompt")
        o = _attn_prompt(q, kcat_p, bsz=bsz, seq=seq, tq=ATTN_TQ, tk=ATTN_TK)
        last = j == n_b - 1
        xp = _mla_back(o, sg, xp, w_uvh, w_o, g_f, final_norm=last, name="mla_back_prompt")

        q, sg = _mla_front(xs, g, w_cq, w_gate, g_q, w_qn, w_qr, w_qs, w_ukt, cc_s, ss_s, tm=ATTN_TQ,
                           name="mla_front_sample")
        spt = ATTN_TQ // dseq
        q_seq = q.reshape(-1, N_HEADS, spt, dseq, QK_PAD).transpose(0, 2, 1, 3, 4)
        o = _attn_sample(q_seq.reshape(dbs, N_HEADS * dseq, QK_PAD), cache_latent, cache_krope_t, page_table,
                         new_lat, new_kr, n_pg=SAMPLE_PAGES_PER_STEP)
        o = o.reshape(-1, spt, N_HEADS, dseq, KV_LORA).transpose(0, 2, 1, 3, 4)
        xs = _mla_back(o.reshape(-1, N_HEADS, ATTN_TQ, KV_LORA), sg, xs, w_uvh, w_o, g_f, final_norm=last,
                       name="mla_back_sample")

    return (xp.reshape(bsz, seq, d), xs.reshape(dbs, dseq, d),
            lat_p.reshape(bsz, seq, KV_LORA), kr_p.reshape(bsz, seq, QK_ROPE),
            new_lat, new_kr,
            jnp.stack(hp_re), jnp.stack(hp_im), jnp.stack(hs_re), jnp.stack(hs_im))
```

```python
import functools
import math

import jax
import jax.numpy as jnp
from jax import lax
from jax.experimental import pallas as pl
from jax.experimental.pallas import tpu as pltpu

F32 = jnp.float32
BF16 = jnp.bfloat16

GROUP_CH = 16
STATE_DIM = 64
N_HEADS = 8
QK_NOPE = 128
QK_ROPE = 64
V_HEAD = 128
KV_LORA = 256
Q_LORA = 384
PAGE_SIZE = 128
ROPE_THETA = 10000.0
RMS_EPS = 1e-6
SOFTMAX_SCALE = 1.0 / math.sqrt(QK_NOPE + QK_ROPE)
SCALE_LOG2E = SOFTMAX_SCALE * math.log2(math.e)
NEG_INF = -1e30
LANES = 128
QK_PAD = KV_LORA + LANES
PROMPT_CHUNK = 16
ATTN_TQ = 512
ATTN_TK = 512
ATTN_HEADS_PER_CHUNK = 2
SAMPLE_PAGES_PER_STEP = 32
SCAN_ROWS = 512
GLU_ROW_SPLIT = 4
VMEM_LIMIT = 56 * 1024 * 1024


def _dot(a, b):
    return jnp.dot(a, b, preferred_element_type=F32)


def _dot_nt(a, b):
    return lax.dot_general(a, b, (((1,), (1,)), ((), ())), preferred_element_type=F32)


def _dot_f32(a, b):
    return jnp.dot(a, b, preferred_element_type=F32, precision=lax.Precision.HIGHEST)


def _rms(x, g):
    return x * lax.rsqrt(jnp.mean(x * x, axis=-1, keepdims=True) + RMS_EPS) * g


def _params(*sem):
    return pltpu.CompilerParams(dimension_semantics=sem, vmem_limit_bytes=VMEM_LIMIT)


def _norm_matmul_kernel(x_ref, g_ref, w_ref, o_ref, *, tn):
    h = _rms(x_ref[...], g_ref[...]).astype(BF16)
    parts, _, n = w_ref.shape
    for p in range(parts):
        for j in range(n // tn):
            cols = slice(j * tn, (j + 1) * tn)
            o_ref[p, :, cols] = _dot(h, w_ref[p, :, cols])


def _norm_matmul(x, g, w3, *, tm, tn, name):
    m, d = x.shape
    parts, _, n = w3.shape
    tm = min(tm, m)
    return pl.pallas_call(
        functools.partial(_norm_matmul_kernel, tn=tn),
        grid=(m // tm,),
        in_specs=[
            pl.BlockSpec((tm, d), lambda i: (i, 0)),
            pl.BlockSpec((1, d), lambda i: (0, 0)),
            pl.BlockSpec((parts, d, n), lambda i: (0, 0, 0), pipeline_mode=pl.Buffered(1)),
        ],
        out_specs=pl.BlockSpec((parts, tm, n), lambda i: (0, i, 0)),
        out_shape=jax.ShapeDtypeStruct((parts, m, n), F32),
        compiler_params=_params("parallel"),
        name=name,
    )(x, g, w3)


def _discretize(lam_re, lam_im, log_dt):
    dt = jnp.exp(log_dt)
    mag = jnp.exp(lam_re * dt)
    lb_re = mag * jnp.cos(lam_im * dt)
    lb_im = mag * jnp.sin(lam_im * dt)
    den = lam_re * lam_re + lam_im * lam_im
    nr = lb_re - 1.0
    f_re = (nr * lam_re + lb_im * lam_im) / den
    f_im = (lb_im * lam_re - nr * lam_im) / den
    return lb_re, lb_im, f_re, f_im


def _cpow(br, bi, e, nbits, shape):
    br = jnp.broadcast_to(br, shape)
    bi = jnp.broadcast_to(bi, shape)
    rr = jnp.ones(shape, F32)
    ri = jnp.zeros(shape, F32)
    for j in range(nbits):
        bit = ((e >> j) & 1) == 1
        nr = rr * br - ri * bi
        ni = rr * bi + ri * br
        rr = jnp.where(bit, nr, rr)
        ri = jnp.where(bit, ni, ri)
        if j + 1 < nbits:
            br, bi = br * br - bi * bi, 2.0 * br * bi
    return rr, ri


def _s5_prep_kernel(*refs, chunk, rotate):
    for g in range(refs[0].shape[0]):
        _s5_prep_group(g, *refs, chunk=chunk, rotate=rotate)


def _s5_prep_group(g, pcol_ref, prow_ref, btr_ref, bti_ref, ctr_ref, cti_ref,
                   toep_ref, wsr_ref, wsi_ref, wor_ref, woi_ref, at_ref, *, chunk, rotate):
    tc = chunk * GROUP_CH
    tl = ctr_ref.shape[2]
    nbits = chunk.bit_length()
    shift = GROUP_CH.bit_length() - 1
    per_half = LANES // GROUP_CH

    def rot_cols(x):
        if not rotate or g == 0:
            return x
        return jnp.concatenate([_rot_blocks(x[:, h * LANES:(h + 1) * LANES], g) for h in range(tl // LANES)], axis=1)

    def dest_row(s):
        if not rotate:
            return GROUP_CH * s
        return LANES * (s // per_half) + GROUP_CH * ((s + g) % per_half)

    pc = pcol_ref[g]
    lbr_c, lbi_c, _, _ = _discretize(pc[:, 0:1], pc[:, 1:2], pc[:, 2:3])
    pr = prow_ref[g]
    lbr_r, lbi_r, f_re, f_im = _discretize(pr[0:1], pr[1:2], pr[2:3])

    cr = ctr_ref[g]
    ci = cti_ref[g]
    shape = cr.shape
    lane_t = lax.broadcasted_iota(jnp.int32, shape, 1) >> shift
    p0r, p0i = _cpow(lbr_c, lbi_c, lane_t, nbits, shape)
    p1r = p0r * lbr_c - p0i * lbi_c
    p1i = p0r * lbi_c + p0i * lbr_c
    rr = cr * p0r - ci * p0i
    ri = cr * p0i + ci * p0r
    wor_ref[g] = rot_cols(cr * p1r - ci * p1i)[:, :tc].astype(BF16)
    woi_ref[g] = rot_cols(-(cr * p1i + ci * p1r))[:, :tc].astype(BF16)

    btr = btr_ref[g]
    bti = bti_ref[g]
    bbr = f_re * btr - f_im * bti
    bbi = f_re * bti + f_im * btr

    krow = _dot_f32(bbr[:GROUP_CH], rr) - _dot_f32(bbi[:GROUP_CH], ri)
    lane = lax.broadcasted_iota(jnp.int32, (GROUP_CH, tl), 1)
    for s in range(chunk):
        if s == 0:
            blk = krow
        else:
            blk = jnp.where(lane >= GROUP_CH * s, pltpu.roll(krow, GROUP_CH * s, 1), 0.0)
        toep_ref[g, dest_row(s):dest_row(s) + GROUP_CH, :] = rot_cols(blk)[:, :tc].astype(BF16)

    shape_s = btr.shape
    row_e = (chunk - 1) - (lax.broadcasted_iota(jnp.int32, shape_s, 0) >> shift)
    qr, qi = _cpow(lbr_r, lbi_r, row_e, nbits, shape_s)
    wsr = (bbr * qr - bbi * qi).astype(BF16)
    wsi = (bbr * qi + bbi * qr).astype(BF16)
    for s in range(chunk):
        src = slice(GROUP_CH * s, GROUP_CH * (s + 1))
        dst = slice(dest_row(s), dest_row(s) + GROUP_CH)
        wsr_ref[g, dst, :] = wsr[src]
        wsi_ref[g, dst, :] = wsi[src]

    ar, ai = lbr_r, lbi_r
    for _ in range(chunk.bit_length() - 1):
        ar, ai = ar * ar - ai * ai, 2.0 * ar * ai
    at_ref[g, 0:1, :] = ar
    at_ref[g, 1:2, :] = ai


def _s5_prep(a_re, a_im, log_dt, b_re, b_im, c_re, c_im, *, chunk, rotate):
    p = a_re.shape[-1]
    a_re, a_im = a_re.reshape(-1, p), a_im.reshape(-1, p)
    log_dt = log_dt.reshape(-1)
    b_re, b_im = b_re.reshape((-1,) + b_re.shape[-2:]), b_im.reshape((-1,) + b_im.shape[-2:])
    c_re, c_im = c_re.reshape((-1,) + c_re.shape[-2:]), c_im.reshape((-1,) + c_im.shape[-2:])
    g = a_re.shape[0]
    gpb = LANES // GROUP_CH
    tc = chunk * GROUP_CH
    tl = max(tc, LANES)
    ldt = jnp.broadcast_to(log_dt[:, None], (g, p))
    pcol = jnp.stack([a_re, a_im, ldt], axis=-1)
    prow = jnp.stack([a_re, a_im, ldt], axis=1)
    bt_re = jnp.tile(jnp.swapaxes(b_re, 1, 2), (1, chunk, 1))
    bt_im = jnp.tile(jnp.swapaxes(b_im, 1, 2), (1, chunk, 1))
    ct_re = jnp.tile(jnp.swapaxes(c_re, 1, 2), (1, 1, tl // GROUP_CH))
    ct_im = jnp.tile(jnp.swapaxes(c_im, 1, 2), (1, 1, tl // GROUP_CH))

    def spec(a, b):
        return pl.BlockSpec((gpb, a, b), lambda i: (i, 0, 0))

    return pl.pallas_call(
        functools.partial(_s5_prep_kernel, chunk=chunk, rotate=rotate),
        grid=(g // gpb,),
        in_specs=[spec(p, 3), spec(3, p), spec(tc, p), spec(tc, p), spec(p, tl), spec(p, tl)],
        out_specs=[spec(tc, tc), spec(tc, p), spec(tc, p), spec(p, tc), spec(p, tc), spec(2, p)],
        out_shape=[
            jax.ShapeDtypeStruct((g, tc, tc), BF16),
            jax.ShapeDtypeStruct((g, tc, p), BF16),
            jax.ShapeDtypeStruct((g, tc, p), BF16),
            jax.ShapeDtypeStruct((g, p, tc), BF16),
            jax.ShapeDtypeStruct((g, p, tc), BF16),
            jax.ShapeDtypeStruct((g, 2, p), F32),
        ],
        compiler_params=_params("parallel"),
        name=f"s5_prep_t{chunk}",
    )(pcol, prow, bt_re, bt_im, ct_re, ct_im)


def _s5_scan_kernel(u_ref, toep_ref, wsr_ref, wsi_ref, wor_ref, woi_ref, at_ref, h0r_ref, h0i_ref,
                    y_ref, hr_ref, hi_ref, sr_s, si_s, pr_s, pi_s, *, n_chunks, bsz):
    for g in range(u_ref.shape[0]):
        sr_s[g] = _dot(u_ref[g], wsr_ref[g])
        si_s[g] = _dot(u_ref[g], wsi_ref[g])
    ar = at_ref[:, 0:1, :]
    ai = at_ref[:, 1:2, :]

    def body(k, carry):
        hr, hi = carry
        rows = pl.ds(pl.multiple_of(k * bsz, bsz), bsz)
        pr_s[:, rows, :] = hr
        pi_s[:, rows, :] = hi
        return (ar * hr - ai * hi + sr_s[:, rows, :], ar * hi + ai * hr + si_s[:, rows, :])

    hr, hi = lax.fori_loop(0, n_chunks, body, (h0r_ref[...], h0i_ref[...]))
    hr_ref[...] = hr
    hi_ref[...] = hi
    for g in range(u_ref.shape[0]):
        y_ref[g] = (_dot(u_ref[g], toep_ref[g]) + _dot(pr_s[g].astype(BF16), wor_ref[g])
                    + _dot(pi_s[g].astype(BF16), woi_ref[g]))


def _s5_scan(u_l, ops, h0r, h0i, *, layer, n_chunks, bsz):
    toep, wsr, wsi, wor, woi, at = ops
    g, n, tc = u_l.shape
    p = STATE_DIM
    gpb = LANES // GROUP_CH
    nblk = g // gpb

    def spec(a, b):
        return pl.BlockSpec((gpb, a, b), lambda i: (i, 0, 0))

    def ospec(a, b):
        return pl.BlockSpec((gpb, a, b), lambda i: (layer * nblk + i, 0, 0))

    return pl.pallas_call(
        functools.partial(_s5_scan_kernel, n_chunks=n_chunks, bsz=bsz),
        grid=(nblk,),
        in_specs=[spec(n, tc), ospec(tc, tc), ospec(tc, p), ospec(tc, p), ospec(p, tc), ospec(p, tc),
                  ospec(2, p), spec(bsz, p), spec(bsz, p)],
        out_specs=[spec(n, tc), spec(bsz, p), spec(bsz, p)],
        out_shape=[
            jax.ShapeDtypeStruct((g, n, tc), F32),
            jax.ShapeDtypeStruct((g, bsz, p), F32),
            jax.ShapeDtypeStruct((g, bsz, p), F32),
        ],
        scratch_shapes=[pltpu.VMEM((gpb, n, p), F32)] * 4,
        compiler_params=_params("parallel"),
        name=f"s5_scan_n{n}",
    )(u_l, toep, wsr, wsi, wor, woi, at, h0r, h0i)


def _rot_blocks(x, k):
    k %= LANES // GROUP_CH
    return x if k == 0 else pltpu.roll(x, k * GROUP_CH, 1)


def _merge_blocks(xs, offset):
    n = LANES // GROUP_CH
    blk = lax.broadcasted_iota(jnp.int32, xs[0].shape, 1) >> (GROUP_CH.bit_length() - 1)
    out = xs[(-offset) % n]
    for b in range(1, n):
        out = jnp.where(blk == b, xs[(b - offset) % n], out)
    return out


def _s5_scan_fused_kernel(u_ref, toep_ref, wsr_ref, wsi_ref, wor_ref, woi_ref, at_ref, h0r_ref, h0i_ref, d_ref,
                          a_ref, hr_ref, hi_ref, hr_s, hi_s, sr_s, si_s, pr_s, pi_s, y_s, *, bsz):
    r = pl.program_id(1)
    _, n_t, rows, _ = u_ref.shape
    gpb = toep_ref.shape[0]
    n_half = n_t // 8

    @pl.when(r == 0)
    def _():
        hr_s[...] = h0r_ref[...]
        hi_s[...] = h0i_ref[...]

    rot = [_rot_blocks(u_ref[0, t], t) for t in range(n_t)]
    for g in range(gpb):
        u_g = jnp.concatenate([_merge_blocks(rot[h * 8:(h + 1) * 8], g) for h in range(n_half)],
                              axis=1).astype(BF16)
        y_s[g] = _dot(u_g, toep_ref[g])
        sr_s[g] = _dot(u_g, wsr_ref[g])
        si_s[g] = _dot(u_g, wsi_ref[g])

    ar = at_ref[:, 0:1, :]
    ai = at_ref[:, 1:2, :]

    def body(k, carry):
        hr, hi = carry
        rs = pl.ds(pl.multiple_of(k * bsz, bsz), bsz)
        pr_s[:, rs, :] = hr
        pi_s[:, rs, :] = hi
        return (ar * hr - ai * hi + sr_s[:, rs, :], ar * hi + ai * hr + si_s[:, rs, :])

    hr, hi = lax.fori_loop(0, rows // bsz, body, (hr_s[...], hi_s[...]))
    hr_s[...] = hr
    hi_s[...] = hi

    @pl.when(r == pl.num_programs(1) - 1)
    def _():
        hr_ref[...] = hr
        hi_ref[...] = hi

    ys = []
    for g in range(gpb):
        ys.append(y_s[g] + _dot(pr_s[g].astype(BF16), wor_ref[g]) + _dot(pi_s[g].astype(BF16), woi_ref[g]))
    d = d_ref[...]
    for h in range(n_half):
        y_half = [y[:, h * LANES:(h + 1) * LANES] for y in ys]
        for tl in range(8):
            t = h * 8 + tl
            y_t = _rot_blocks(_merge_blocks(y_half, tl), -tl)
            a_ref[t] = jax.nn.gelu(y_t + d * u_ref[0, t]).astype(BF16)


def _s5_scan_fused(uz4, ops, h0r, h0i, d_skip, *, layer, bsz, rows):
    toep, wsr, wsi, wor, woi, at = ops
    _, n_t, n, width = uz4.shape
    g = width // GROUP_CH
    tc = toep.shape[1]
    p = STATE_DIM
    gpb = LANES // GROUP_CH
    nblk = g // gpb
    assert n_t % 8 == 0 and tc == n_t * GROUP_CH and n % rows == 0 and rows % bsz == 0

    def gspec(a, b):
        return pl.BlockSpec((gpb, a, b), lambda i, r: (i, 0, 0))

    def ospec(a, b):
        return pl.BlockSpec((gpb, a, b), lambda i, r: (layer * nblk + i, 0, 0))

    return pl.pallas_call(
        functools.partial(_s5_scan_fused_kernel, bsz=bsz),
        grid=(nblk, n // rows),
        in_specs=[
            pl.BlockSpec((1, n_t, rows, LANES), lambda i, r: (0, 0, r, i)),
            ospec(tc, tc), ospec(tc, p), ospec(tc, p), ospec(p, tc), ospec(p, tc), ospec(2, p),
            gspec(bsz, p), gspec(bsz, p),
            pl.BlockSpec((1, LANES), lambda i, r: (0, i)),
        ],
        out_specs=[
            pl.BlockSpec((n_t, rows, LANES), lambda i, r: (0, r, i)),
            gspec(bsz, p), gspec(bsz, p),
        ],
        out_shape=[
            jax.ShapeDtypeStruct((n_t, n, width), BF16),
            jax.ShapeDtypeStruct((g, bsz, p), F32),
            jax.ShapeDtypeStruct((g, bsz, p), F32),
        ],
        scratch_shapes=[pltpu.VMEM((gpb, bsz, p), F32)] * 2 + [pltpu.VMEM((gpb, rows, p), F32)] * 4
        + [pltpu.VMEM((gpb, rows, tc), F32)],
        compiler_params=_params("parallel", "arbitrary"),
        name="s5_scan_fused",
    )(uz4, toep, wsr, wsi, wor, woi, at, h0r, h0i, d_skip)


def _s5_act_kernel(y_ref, u_ref, d_ref, a_ref):
    a_ref[...] = jax.nn.gelu(y_ref[...] + d_ref[...] * u_ref[0]).astype(BF16)


def _s5_act(y, uz, d_skip):
    m, w = y.shape
    return pl.pallas_call(
        _s5_act_kernel,
        grid=(1,),
        in_specs=[
            pl.BlockSpec((m, w), lambda i: (0, 0)),
            pl.BlockSpec((1, m, w), lambda i: (0, 0, 0)),
            pl.BlockSpec((1, w), lambda i: (0, 0)),
        ],
        out_specs=pl.BlockSpec((m, w), lambda i: (0, 0)),
        out_shape=jax.ShapeDtypeStruct((m, w), BF16),
        compiler_params=_params("arbitrary"),
        name="s5_act_sample",
    )(y, uz, d_skip)


def _s5_glu_out_kernel(a_ref, z_ref, x_ref, wa_ref, wb_ref, ba_ref, bb_ref, wo_ref, o_ref, acc_s):
    j = pl.program_id(1)

    @pl.when(j == 0)
    def _():
        acc_s[...] = jnp.zeros(acc_s.shape, F32)

    rs = a_ref.shape[0] // GLU_ROW_SPLIT
    for h in range(GLU_ROW_SPLIT):
        rows = slice(h * rs, (h + 1) * rs)
        a = a_ref[rows, :]
        ga = _dot(a, wa_ref[0]) + ba_ref[0]
        gb = _dot(a, wb_ref[0]) + bb_ref[0]
        z = z_ref[0, rows, :]
        v = (ga * jax.nn.sigmoid(gb) * (z * jax.nn.sigmoid(z))).astype(BF16)
        acc_s[rows, :] += _dot(v, wo_ref[...])

    @pl.when(j == pl.num_programs(1) - 1)
    def _():
        o_ref[...] = x_ref[...] + acc_s[...]


def _s5_glu_out(a, uz, x, w_glu2, b_glu2, w_out, *, tm, tn, name):
    m, w = a.shape
    d = x.shape[1]
    tm = min(tm, m)
    return pl.pallas_call(
        _s5_glu_out_kernel,
        grid=(m // tm, w // tn),
        in_specs=[
            pl.BlockSpec((tm, w), lambda i, j: (i, 0)),
            pl.BlockSpec((1, tm, tn), lambda i, j: (1, i, j)),
            pl.BlockSpec((tm, d), lambda i, j: (i, 0)),
            pl.BlockSpec((1, w, tn), lambda i, j: (0, 0, j)),
            pl.BlockSpec((1, w, tn), lambda i, j: (1, 0, j)),
            pl.BlockSpec((1, 1, tn), lambda i, j: (0, 0, j)),
            pl.BlockSpec((1, 1, tn), lambda i, j: (1, 0, j)),
            pl.BlockSpec((tn, d), lambda i, j: (j, 0)),
        ],
        out_specs=pl.BlockSpec((tm, d), lambda i, j: (i, 0)),
        out_shape=jax.ShapeDtypeStruct((m, d), F32),
        scratch_shapes=[pltpu.VMEM((tm, d), F32)],
        compiler_params=_params("parallel", "arbitrary"),
        name=name,
    )(a, uz, x, w_glu2, w_glu2, b_glu2, b_glu2, w_out)


def _latent_kernel(x_ref, g_ref, wl_ref, wr_ref, ws_ref, gl_ref, cc_ref, ss_ref, lat_ref, kr_ref, kcat_ref):
    h = _rms(x_ref[...], g_ref[...]).astype(BF16)
    lat = _rms(_dot(h, wl_ref[...]), gl_ref[...])
    kr = _dot(h, wr_ref[...]) * cc_ref[...] + _dot(h, ws_ref[...]) * ss_ref[...]
    lat_ref[...] = lat
    kr_ref[...] = kr[:, :QK_ROPE]
    kcat_ref[:, :KV_LORA] = lat.astype(BF16)
    kcat_ref[:, KV_LORA:] = kr.astype(BF16)


def _latent(x, g, wl, wr, ws, gl, cc, ss, *, tm, name):
    m, d = x.shape
    tm = min(tm, m)
    nt = cc.shape[0] // tm

    def full(a):
        return pl.BlockSpec(a.shape, lambda i: (0,) * a.ndim)

    return pl.pallas_call(
        _latent_kernel,
        grid=(m // tm,),
        in_specs=[
            pl.BlockSpec((tm, d), lambda i: (i, 0)),
            full(g), full(wl), full(wr), full(ws), full(gl),
            pl.BlockSpec((tm, LANES), lambda i: (i % nt, 0)),
            pl.BlockSpec((tm, LANES), lambda i: (i % nt, 0)),
        ],
        out_specs=[
            pl.BlockSpec((tm, KV_LORA), lambda i: (i, 0)),
            pl.BlockSpec((tm, QK_ROPE), lambda i: (i, 0)),
            pl.BlockSpec((tm, QK_PAD), lambda i: (i, 0)),
        ],
        out_shape=[
            jax.ShapeDtypeStruct((m, KV_LORA), F32),
            jax.ShapeDtypeStruct((m, QK_ROPE), F32),
            jax.ShapeDtypeStruct((m, QK_PAD), BF16),
        ],
        compiler_params=_params("parallel"),
        name=name,
    )(x, g, wl, wr, ws, gl, cc, ss)


def _mla_front_kernel(x_ref, g_ref, wcq_ref, wg_ref, gq_ref, wqn_ref, wqr_ref, wqs_ref, wuk_ref,
                      cc_ref, ss_ref, q_ref, sg_ref):
    h = _rms(x_ref[...], g_ref[...]).astype(BF16)
    cq = _rms(_dot(h, wcq_ref[...]), gq_ref[...]).astype(BF16)
    gate = _dot(h, wg_ref[...])
    sg_ref[...] = gate * jax.nn.sigmoid(gate)
    qn = _dot(cq, wqn_ref[...]).astype(BF16)
    qr = _dot(cq, wqr_ref[...])
    qs = _dot(cq, wqs_ref[...])
    cc = cc_ref[...]
    ss = ss_ref[...]
    for hd in range(N_HEADS):
        sl = slice(hd * LANES, (hd + 1) * LANES)
        q_ref[0, hd, :, :KV_LORA] = _dot(qn[:, sl], wuk_ref[hd]).astype(BF16)
        q_ref[0, hd, :, KV_LORA:] = (qr[:, sl] * cc + qs[:, sl] * ss).astype(BF16)


def _mla_front(x, g, wcq, wg, gq, wqn, wqr, wqs, wuk, cc, ss, *, tm, name):
    m, d = x.shape
    tm = min(tm, m)
    nt = cc.shape[0] // tm

    def full(a):
        return pl.BlockSpec(a.shape, lambda i: (0,) * a.ndim)

    return pl.pallas_call(
        _mla_front_kernel,
        grid=(m // tm,),
        in_specs=[
            pl.BlockSpec((tm, d), lambda i: (i, 0)),
            full(g), full(wcq), full(wg), full(gq), full(wqn), full(wqr), full(wqs), full(wuk),
            pl.BlockSpec((tm, LANES), lambda i: (i % nt, 0)),
            pl.BlockSpec((tm, LANES), lambda i: (i % nt, 0)),
        ],
        out_specs=[
            pl.BlockSpec((1, N_HEADS, tm, QK_PAD), lambda i: (i, 0, 0, 0)),
            pl.BlockSpec((tm, N_HEADS * V_HEAD), lambda i: (i, 0)),
        ],
        out_shape=[
            jax.ShapeDtypeStruct((m // tm, N_HEADS, tm, QK_PAD), BF16),
            jax.ShapeDtypeStruct((m, N_HEADS * V_HEAD), F32),
        ],
        compiler_params=_params("parallel"),
        name=name,
    )(x, g, wcq, wg, gq, wqn, wqr, wqs, wuk, cc, ss)


def _lane_tile(x, width):
    return x if width == LANES else jnp.concatenate([x] * (width // LANES), axis=1)


def _softmax_update(s, kl, m_ref, l_ref, acc_ref, rows):
    m_prev = m_ref[rows, :]
    m_new = jnp.maximum(m_prev, jnp.max(s, axis=1, keepdims=True))
    alpha = jnp.exp2(m_prev - m_new)
    p = jnp.exp2(s - _lane_tile(m_new, s.shape[1]))
    l_ref[rows, :] = alpha * l_ref[rows, :] + jnp.sum(p, axis=1, keepdims=True)
    acc_ref[rows, :] = _lane_tile(alpha, kl.shape[1]) * acc_ref[rows, :] + _dot(p.astype(BF16), kl)
    m_ref[rows, :] = m_new


def _attn_prompt_kernel(qi_ref, ki_ref, q_ref, k_ref, o_ref, m_s, l_s, acc_s, *, tq, tk, heads_per_chunk):
    step = pl.program_id(1)
    qi = qi_ref[step]
    ki = ki_ref[step]
    diag = (qi * tq + (tq - 1)) // tk
    rc = heads_per_chunk * tq

    @pl.when(ki == 0)
    def _():
        m_s[...] = jnp.full(m_s.shape, NEG_INF, F32)
        l_s[...] = jnp.zeros(l_s.shape, F32)
        acc_s[...] = jnp.zeros(acc_s.shape, F32)

    def update(masked):
        k = k_ref[...]
        kl = k[:, :KV_LORA]
        for c in range(N_HEADS // heads_per_chunk):
            q = q_ref[0, c * heads_per_chunk:(c + 1) * heads_per_chunk].reshape(rc, QK_PAD)
            s = _dot_nt(q, k) * SCALE_LOG2E
            if masked:
                tok = qi * tq + (lax.broadcasted_iota(jnp.int32, (rc, tk), 0) & (tq - 1))
                col = ki * tk + lax.broadcasted_iota(jnp.int32, (rc, tk), 1)
                s = jnp.where(col <= tok, s, NEG_INF)
            _softmax_update(s, kl, m_s, l_s, acc_s, pl.ds(c * rc, rc))

    @pl.when(ki < diag)
    def _():
        update(False)

    @pl.when(ki == diag)
    def _():
        update(True)
        for hd in range(N_HEADS):
            rows = pl.ds(hd * tq, tq)
            o_ref[0, hd] = (acc_s[rows, :] / _lane_tile(l_s[rows, :], KV_LORA)).astype(BF16)


def _attn_prompt(q, kcat, *, bsz, seq, tq, tk):
    assert tq & (tq - 1) == 0 and seq % tq == 0 and seq % tk == 0
    nq = seq // tq
    nk = seq // tk
    rows = N_HEADS * tq
    pairs = [(i, j) for i in range(nq) for j in range((i * tq + tq - 1) // tk + 1)]
    qi_arr = jnp.asarray([p[0] for p in pairs], jnp.int32)
    ki_arr = jnp.asarray([p[1] for p in pairs], jnp.int32)
    grid_spec = pltpu.PrefetchScalarGridSpec(
        num_scalar_prefetch=2,
        grid=(bsz, len(pairs)),
        in_specs=[
            pl.BlockSpec((1, N_HEADS, tq, QK_PAD), lambda b, s, qi, ki: (b * nq + qi[s], 0, 0, 0)),
            pl.BlockSpec((tk, QK_PAD), lambda b, s, qi, ki: (b * nk + ki[s], 0)),
        ],
        out_specs=pl.BlockSpec((1, N_HEADS, tq, KV_LORA), lambda b, s, qi, ki: (b * nq + qi[s], 0, 0, 0)),
        scratch_shapes=[
            pltpu.VMEM((rows, LANES), F32),
            pltpu.VMEM((rows, LANES), F32),
            pltpu.VMEM((rows, KV_LORA), F32),
        ],
    )
    return pl.pallas_call(
        functools.partial(_attn_prompt_kernel, tq=tq, tk=tk, heads_per_chunk=ATTN_HEADS_PER_CHUNK),
        grid_spec=grid_spec,
        out_shape=jax.ShapeDtypeStruct((bsz * nq, N_HEADS, tq, KV_LORA), BF16),
        compiler_params=_params("parallel", "arbitrary"),
        name="attn_prompt",
    )(qi_arr, ki_arr, q, kcat)


def _attn_sample_kernel(pt_ref, q_ref, nl_ref, nk_ref, lat_hbm, krt_hbm, o_ref,
                        lat_buf, krt_buf, sems, kl_s, kr_s, m_s, l_s, acc_s, *, n_pg, dec_seq):
    step = pl.program_id(1)
    steps = pl.num_programs(1)
    gstep = pl.program_id(0) * steps + step
    total = pl.num_programs(0) * steps
    slot = gstep & 1

    def page_copies(at, buf_slot):
        out = []
        for i in range(n_pg):
            page = pt_ref[at * n_pg + i]
            out.append(pltpu.make_async_copy(lat_hbm.at[page], lat_buf.at[buf_slot, i], sems.at[0, buf_slot]))
            out.append(pltpu.make_async_copy(krt_hbm.at[page], krt_buf.at[buf_slot, i], sems.at[1, buf_slot]))
        return out

    @pl.when(gstep == 0)
    def _():
        for c in page_copies(0, 0):
            c.start()

    @pl.when(gstep + 1 < total)
    def _():
        for c in page_copies(gstep + 1, 1 - slot):
            c.start()

    @pl.when(step == 0)
    def _():
        m_s[...] = jnp.full(m_s.shape, NEG_INF, F32)
        l_s[...] = jnp.zeros(l_s.shape, F32)
        acc_s[...] = jnp.zeros(acc_s.shape, F32)

    for c in page_copies(gstep, slot):
        c.wait()

    for i in range(n_pg):
        kl_s[i * PAGE_SIZE:(i + 1) * PAGE_SIZE, :] = lat_buf[slot, i].astype(BF16)
        kr_s[:, i * PAGE_SIZE:(i + 1) * PAGE_SIZE] = krt_buf[slot, i].astype(BF16)

    q = q_ref[0]
    ql = q[:, :KV_LORA]
    qr = q[:, KV_LORA:KV_LORA + QK_ROPE]
    kl = kl_s[...]
    s = (_dot_nt(ql, kl) + _dot(qr, kr_s[...])) * SCALE_LOG2E
    _softmax_update(s, kl, m_s, l_s, acc_s, slice(None))

    @pl.when(step == pl.num_programs(1) - 1)
    def _():
        nl = nl_ref[0].astype(BF16).astype(F32)
        nk = nk_ref[0].astype(BF16).astype(F32)
        qlf = ql.astype(F32)
        qrf = qr.astype(F32)
        tok = lax.broadcasted_iota(jnp.int32, (q.shape[0], 1), 0) & (dec_seq - 1)
        m = m_s[...]
        l = l_s[...]
        acc = acc_s[...]
        for j in range(dec_seq):
            sj = (jnp.sum(qlf * nl[j:j + 1], axis=-1, keepdims=True)
                  + jnp.sum(qrf * nk[j:j + 1], axis=-1, keepdims=True)) * SCALE_LOG2E
            sj = jnp.where(tok >= j, sj, NEG_INF)
            m_new = jnp.maximum(m, sj)
            alpha = jnp.exp2(m - m_new)
            pj = jnp.exp2(sj - m_new)
            l = alpha * l + pj
            pv = _lane_tile(pj, KV_LORA).astype(BF16).astype(F32) * nl[j:j + 1]
            acc = _lane_tile(alpha, KV_LORA) * acc + pv
            m = m_new
        o_ref[0] = (acc / _lane_tile(l, KV_LORA)).astype(BF16)


def _attn_sample(q, cache_latent, cache_krope_t, page_table, new_lat, new_kr, *, n_pg):
    n_seq, n_pages = page_table.shape
    rows = q.shape[1]
    dec_seq = new_lat.shape[1]
    assert dec_seq & (dec_seq - 1) == 0 and n_pages % n_pg == 0
    steps = n_pages // n_pg
    pt = page_table.reshape(-1)

    def seq_spec(a, b):
        return pl.BlockSpec((1, a, b), lambda n, s, pt_ref: (n, 0, 0))

    grid_spec = pltpu.PrefetchScalarGridSpec(
        num_scalar_prefetch=1,
        grid=(n_seq, steps),
        in_specs=[seq_spec(rows, QK_PAD), seq_spec(dec_seq, KV_LORA), seq_spec(dec_seq, QK_ROPE),
                  pl.BlockSpec(memory_space=pl.ANY), pl.BlockSpec(memory_space=pl.ANY)],
        out_specs=seq_spec(rows, KV_LORA),
        scratch_shapes=[
            pltpu.VMEM((2, n_pg, PAGE_SIZE, KV_LORA), F32),
            pltpu.VMEM((2, n_pg, QK_ROPE, PAGE_SIZE), F32),
            pltpu.SemaphoreType.DMA((2, 2)),
            pltpu.VMEM((n_pg * PAGE_SIZE, KV_LORA), BF16),
            pltpu.VMEM((QK_ROPE, n_pg * PAGE_SIZE), BF16),
            pltpu.VMEM((rows, LANES), F32),
            pltpu.VMEM((rows, LANES), F32),
            pltpu.VMEM((rows, KV_LORA), F32),
        ],
    )
    return pl.pallas_call(
        functools.partial(_attn_sample_kernel, n_pg=n_pg, dec_seq=dec_seq),
        grid_spec=grid_spec,
        out_shape=jax.ShapeDtypeStruct((n_seq, rows, KV_LORA), BF16),
        compiler_params=_params("arbitrary", "arbitrary"),
        name="attn_sample",
    )(pt, q, new_lat, new_kr, cache_latent, cache_krope_t)


def _mla_back_kernel(o_ref, sg_ref, x_ref, wuv_ref, wo_ref, gf_ref, out_ref, *, final_norm):
    parts = []
    for hd in range(N_HEADS):
        oh = _dot(o_ref[0, hd], wuv_ref[hd])
        parts.append((oh * sg_ref[:, hd * V_HEAD:(hd + 1) * V_HEAD]).astype(BF16))
    y = x_ref[...] + _dot(jnp.concatenate(parts, axis=1), wo_ref[...])
    out_ref[...] = _rms(y, gf_ref[...]) if final_norm else y


def _mla_back(o, sg, x, wuv, wo, gf, *, final_norm, name):
    m, d = x.shape
    tm = o.shape[2]

    def full(a):
        return pl.BlockSpec(a.shape, lambda i: (0,) * a.ndim)

    def rows(a):
        return pl.BlockSpec((tm, a.shape[1]), lambda i: (i, 0))

    return pl.pallas_call(
        functools.partial(_mla_back_kernel, final_norm=final_norm),
        grid=(m // tm,),
        in_specs=[pl.BlockSpec((1,) + o.shape[1:], lambda i: (i, 0, 0, 0)), rows(sg), rows(x), full(wuv), full(wo),
                  full(gf)],
        out_specs=pl.BlockSpec((tm, d), lambda i: (i, 0)),
        out_shape=jax.ShapeDtypeStruct((m, d), F32),
        compiler_params=_params("parallel"),
        name=name,
    )(o, sg, x, wuv, wo, gf)


def _to_scan_layout(u, bsz, seq, chunk):
    k = seq // chunk
    g = u.shape[1] // GROUP_CH
    u5 = u.astype(BF16).reshape(bsz, k, chunk, g, GROUP_CH)
    return u5.transpose(3, 1, 0, 2, 4).reshape(g, k * bsz, chunk * GROUP_CH)


def _from_scan_layout(y, bsz, seq, chunk):
    k = seq // chunk
    g = y.shape[0]
    y5 = y.reshape(g, k, bsz, chunk, GROUP_CH)
    return y5.transpose(2, 1, 3, 0, 4).reshape(bsz * seq, g * GROUP_CH)


def _rope_tables(pos):
    half = QK_ROPE // 2
    inv = ROPE_THETA ** (-jnp.arange(half, dtype=F32) / half)
    ang = pos.astype(F32)[:, None] * inv[None, :]
    cos = jnp.cos(ang)
    sin = jnp.sin(ang)
    pad = jnp.zeros((pos.shape[0], LANES - QK_ROPE), F32)
    cc = jnp.concatenate([cos, cos, pad], axis=1)
    ss = jnp.concatenate([-sin, sin, pad], axis=1)
    return cc, ss


def _swap_halves(w):
    half = w.shape[-1] // 2
    return jnp.concatenate([w[..., half:], w[..., :half]], axis=-1)


def _pad_lanes(w):
    pad = [(0, 0)] * (w.ndim - 1) + [(0, LANES - w.shape[-1])]
    return jnp.pad(w, pad)


def kernel(x_prompt, x_sample, cache_latent, cache_krope, page_table, state_ssm_re, state_ssm_im, norm_a, w_in_a, a_re, a_im, log_dt, b_re, b_im, c_re, c_im, d_skip, w_glu, b_glu, w_out_a, norm_kv, w_dkv, norm_latent, w_uk, w_uv, norm_b, w_in_b, norm_q, w_uq, w_out_b, norm_f):
    bsz, seq, d = x_prompt.shape
    dbs, dseq, _ = x_sample.shape
    n_a = norm_a.shape[0]
    n_b = norm_b.shape[0]
    width = d_skip.shape[1]
    n_groups = width // GROUP_CH
    past_len = page_table.shape[1] * PAGE_SIZE

    n_chunks = seq // PROMPT_CHUNK
    xp = x_prompt.reshape(bsz, n_chunks, PROMPT_CHUNK, d).transpose(2, 1, 0, 3).reshape(bsz * seq, d)
    xs = x_sample.reshape(dbs * dseq, d)

    hp_re, hp_im, hs_re, hs_im = [], [], [], []
    zeros_p = jnp.zeros((n_groups, bsz, STATE_DIM), F32)
    ops_p = _s5_prep(a_re, a_im, log_dt, b_re, b_im, c_re, c_im, chunk=PROMPT_CHUNK, rotate=True)
    ops_s = _s5_prep(a_re, a_im, log_dt, b_re, b_im, c_re, c_im, chunk=dseq, rotate=False)
    for i in range(n_a):
        w_in3 = w_in_a[i].astype(BF16).reshape(d, 2, width).transpose(1, 0, 2)
        w_glu2 = w_glu[i].astype(BF16).reshape(width, 2, width).transpose(1, 0, 2)
        b_glu2 = b_glu[i].reshape(2, 1, width)
        w_out = w_out_a[i].astype(BF16)
        g = norm_a[i][None, :]
        dsk = d_skip[i][None, :]

        uz = _norm_matmul(xp, g, w_in3, tm=512, tn=1024, name="s5_in_prompt")
        uz4 = uz.reshape(2, PROMPT_CHUNK, n_chunks * bsz, width)
        a, hr, hi = _s5_scan_fused(uz4, ops_p, zeros_p, zeros_p, dsk, layer=i, bsz=bsz, rows=SCAN_ROWS)
        xp = _s5_glu_out(a.reshape(bsz * seq, width), uz, xp, w_glu2, b_glu2, w_out, tm=1024, tn=512,
                         name="s5_glu_out_prompt")
        hp_re.append(hr.transpose(1, 0, 2))
        hp_im.append(hi.transpose(1, 0, 2))

        uz = _norm_matmul(xs, g, w_in3, tm=512, tn=1024, name="s5_in_sample")
        u_l = _to_scan_layout(uz[0], dbs, dseq, dseq)
        y_l, hr, hi = _s5_scan(u_l, ops_s, state_ssm_re[i].transpose(1, 0, 2), state_ssm_im[i].transpose(1, 0, 2),
                               layer=i, n_chunks=1, bsz=dbs)
        y = _from_scan_layout(y_l, dbs, dseq, dseq)
        a = _s5_act(y, uz, dsk)
        xs = _s5_glu_out(a, uz, xs, w_glu2, b_glu2, w_out, tm=512, tn=512, name="s5_glu_out_sample")
        hs_re.append(hr.transpose(1, 0, 2))
        hs_im.append(hi.transpose(1, 0, 2))

    xp = xp.reshape(PROMPT_CHUNK, n_chunks, bsz, d).transpose(2, 1, 0, 3).reshape(bsz * seq, d)

    cc_p, ss_p = _rope_tables(jnp.arange(seq, dtype=jnp.int32))
    pos_s = past_len + jnp.arange(dseq, dtype=jnp.int32)
    cc_s, ss_s = _rope_tables(jnp.tile(pos_s, dbs))
    w_lat = w_dkv[:, :KV_LORA].astype(BF16)
    w_kr = _pad_lanes(w_dkv[:, KV_LORA:]).astype(BF16)
    w_ks = _pad_lanes(_swap_halves(w_dkv[:, KV_LORA:])).astype(BF16)
    g_kv = norm_kv[None, :]
    g_lat = norm_latent[None, :]
    lat_p, kr_p, kcat_p = _latent(xp, g_kv, w_lat, w_kr, w_ks, g_lat, cc_p, ss_p, tm=512, name="latent_prompt")
    lat_s, kr_s, _ = _latent(xs, g_kv, w_lat, w_kr, w_ks, g_lat, cc_s, ss_s, tm=512, name="latent_sample")
    new_lat = lat_s.reshape(dbs, dseq, KV_LORA)
    new_kr = kr_s.reshape(dbs, dseq, QK_ROPE)

    cache_krope_t = jnp.swapaxes(cache_krope, 1, 2)
    w_ukt = w_uk.astype(BF16).reshape(KV_LORA, N_HEADS, QK_NOPE).transpose(1, 2, 0)
    w_uvh = w_uv.astype(BF16).reshape(KV_LORA, N_HEADS, V_HEAD).transpose(1, 0, 2)
    g_f = norm_f[None, :]
    assert n_b > 0
    for j in range(n_b):
        g = norm_b[j][None, :]
        w_cq = w_in_b[j][:, :Q_LORA].astype(BF16)
        w_gate = w_in_b[j][:, Q_LORA:].astype(BF16)
        g_q = norm_q[j][None, :]
        wq3 = w_uq[j].reshape(Q_LORA, N_HEADS, QK_NOPE + QK_ROPE)
        w_qn = wq3[:, :, :QK_NOPE].reshape(Q_LORA, N_HEADS * QK_NOPE).astype(BF16)
        w_qr = _pad_lanes(wq3[:, :, QK_NOPE:]).reshape(Q_LORA, N_HEADS * LANES).astype(BF16)
        w_qs = _pad_lanes(_swap_halves(wq3[:, :, QK_NOPE:])).reshape(Q_LORA, N_HEADS * LANES).astype(BF16)
        w_o = w_out_b[j].astype(BF16)

        q, sg = _mla_front(xp, g, w_cq, w_gate, g_q, w_qn, w_qr, w_qs, w_ukt, cc_p, ss_p, tm=ATTN_TQ,
                           name="mla_front_prompt")
        o = _attn_prompt(q, kcat_p, bsz=bsz, seq=seq, tq=ATTN_TQ, tk=ATTN_TK)
        last = j == n_b - 1
        xp = _mla_back(o, sg, xp, w_uvh, w_o, g_f, final_norm=last, name="mla_back_prompt")

        q, sg = _mla_front(xs, g, w_cq, w_gate, g_q, w_qn, w_qr, w_qs, w_ukt, cc_s, ss_s, tm=ATTN_TQ,
                           name="mla_front_sample")
        spt = ATTN_TQ // dseq
        q_seq = q.reshape(-1, N_HEADS, spt, dseq, QK_PAD).transpose(0, 2, 1, 3, 4)
        o = _attn_sample(q_seq.reshape(dbs, N_HEADS * dseq, QK_PAD), cache_latent, cache_krope_t, page_table,
                         new_lat, new_kr, n_pg=SAMPLE_PAGES_PER_STEP)
        o = o.reshape(-1, spt, N_HEADS, dseq, KV_LORA).transpose(0, 2, 1, 3, 4)
        xs = _mla_back(o.reshape(-1, N_HEADS, ATTN_TQ, KV_LORA), sg, xs, w_uvh, w_o, g_f, final_norm=last,
                       name="mla_back_sample")

    return (xp.reshape(bsz, seq, d), xs.reshape(dbs, dseq, d),
            lat_p.reshape(bsz, seq, KV_LORA), kr_p.reshape(bsz, seq, QK_ROPE),
            new_lat, new_kr,
            jnp.stack(hp_re), jnp.stack(hp_im), jnp.stack(hs_re), jnp.stack(hs_im))
```

```python
import functools
import math

import jax
import jax.numpy as jnp
from jax import lax
from jax.experimental import pallas as pl
from jax.experimental.pallas import tpu as pltpu

F32 = jnp.float32
BF16 = jnp.bfloat16

GROUP_CH = 16
STATE_DIM = 64
N_HEADS = 8
QK_NOPE = 128
QK_ROPE = 64
V_HEAD = 128
KV_LORA = 256
Q_LORA = 384
PAGE_SIZE = 128
ROPE_THETA = 10000.0
RMS_EPS = 1e-6
SOFTMAX_SCALE = 1.0 / math.sqrt(QK_NOPE + QK_ROPE)
SCALE_LOG2E = SOFTMAX_SCALE * math.log2(math.e)
NEG_INF = -1e30
LANES = 128
QK_PAD = KV_LORA + LANES
PROMPT_CHUNK = 16
ATTN_TQ = 512
ATTN_TK = 512
ATTN_HEADS_PER_CHUNK = 2
SAMPLE_PAGES_PER_STEP = 64
SCAN_ROWS = 512
GLU_ROW_SPLIT = 4
VMEM_LIMIT = 56 * 1024 * 1024


def _dot(a, b):
    return jnp.dot(a, b, preferred_element_type=F32)


def _dot_nt(a, b):
    return lax.dot_general(a, b, (((1,), (1,)), ((), ())), preferred_element_type=F32)


def _dot_f32(a, b):
    return jnp.dot(a, b, preferred_element_type=F32, precision=lax.Precision.HIGHEST)


def _rms(x, g):
    return x * lax.rsqrt(jnp.mean(x * x, axis=-1, keepdims=True) + RMS_EPS) * g


def _params(*sem):
    return pltpu.CompilerParams(dimension_semantics=sem, vmem_limit_bytes=VMEM_LIMIT)


def _norm_matmul_kernel(x_ref, g_ref, w_ref, o_ref, *, tn):
    h = _rms(x_ref[...], g_ref[...]).astype(BF16)
    parts, _, n = w_ref.shape
    for p in range(parts):
        for j in range(n // tn):
            cols = slice(j * tn, (j + 1) * tn)
            o_ref[p, :, cols] = _dot(h, w_ref[p, :, cols])


def _norm_matmul(x, g, w3, *, tm, tn, name):
    m, d = x.shape
    parts, _, n = w3.shape
    tm = min(tm, m)
    return pl.pallas_call(
        functools.partial(_norm_matmul_kernel, tn=tn),
        grid=(m // tm,),
        in_specs=[
            pl.BlockSpec((tm, d), lambda i: (i, 0)),
            pl.BlockSpec((1, d), lambda i: (0, 0)),
            pl.BlockSpec((parts, d, n), lambda i: (0, 0, 0), pipeline_mode=pl.Buffered(1)),
        ],
        out_specs=pl.BlockSpec((parts, tm, n), lambda i: (0, i, 0)),
        out_shape=jax.ShapeDtypeStruct((parts, m, n), F32),
        compiler_params=_params("parallel"),
        name=name,
    )(x, g, w3)


def _discretize(lam_re, lam_im, log_dt):
    dt = jnp.exp(log_dt)
    mag = jnp.exp(lam_re * dt)
    lb_re = mag * jnp.cos(lam_im * dt)
    lb_im = mag * jnp.sin(lam_im * dt)
    den = lam_re * lam_re + lam_im * lam_im
    nr = lb_re - 1.0
    f_re = (nr * lam_re + lb_im * lam_im) / den
    f_im = (lb_im * lam_re - nr * lam_im) / den
    return lb_re, lb_im, f_re, f_im


def _cpow(br, bi, e, nbits, shape):
    br = jnp.broadcast_to(br, shape)
    bi = jnp.broadcast_to(bi, shape)
    rr = jnp.ones(shape, F32)
    ri = jnp.zeros(shape, F32)
    for j in range(nbits):
        bit = ((e >> j) & 1) == 1
        nr = rr * br - ri * bi
        ni = rr * bi + ri * br
        rr = jnp.where(bit, nr, rr)
        ri = jnp.where(bit, ni, ri)
        if j + 1 < nbits:
            br, bi = br * br - bi * bi, 2.0 * br * bi
    return rr, ri


def _s5_prep_kernel(pcol_ref, prow_ref, btr_ref, bti_ref, ctr_ref, cti_ref,
                    toep_ref, wsr_ref, wsi_ref, wor_ref, woi_ref, at_ref, *, chunk, rotate):
    gpb, p, tl = ctr_ref.shape
    tc = chunk * GROUP_CH
    nbits = chunk.bit_length()
    shift = GROUP_CH.bit_length() - 1
    per_half = LANES // GROUP_CH

    pc = pcol_ref[...].reshape(gpb * p, 3)
    lbr_c, lbi_c, _, _ = _discretize(pc[:, 0:1], pc[:, 1:2], pc[:, 2:3])
    cr = ctr_ref[...].reshape(gpb * p, tl)
    ci = cti_ref[...].reshape(gpb * p, tl)
    shape = cr.shape
    lane_t = lax.broadcasted_iota(jnp.int32, shape, 1) >> shift
    p0r, p0i = _cpow(lbr_c, lbi_c, lane_t, nbits, shape)
    p1r = p0r * lbr_c - p0i * lbi_c
    p1i = p0r * lbi_c + p0i * lbr_c
    rr = cr * p0r - ci * p0i
    ri = cr * p0i + ci * p0r
    wor = cr * p1r - ci * p1i
    woi = -(cr * p1i + ci * p1r)

    pr = prow_ref[...]
    lbr_r, lbi_r, f_re, f_im = _discretize(pr[:, 0:1, :], pr[:, 1:2, :], pr[:, 2:3, :])
    btr = btr_ref[...]
    bti = bti_ref[...]
    bbr = f_re * btr - f_im * bti
    bbi = f_re * bti + f_im * btr
    shape_s = btr.shape
    row_e = (chunk - 1) - (lax.broadcasted_iota(jnp.int32, shape_s, 1) >> shift)
    qr, qi = _cpow(lbr_r, lbi_r, row_e, nbits, shape_s)
    wsr = (bbr * qr - bbi * qi).astype(BF16)
    wsi = (bbr * qi + bbi * qr).astype(BF16)

    ar, ai = lbr_r, lbi_r
    for _ in range(chunk.bit_length() - 1):
        ar, ai = ar * ar - ai * ai, 2.0 * ar * ai
    at_ref[:, 0:1, :] = ar
    at_ref[:, 1:2, :] = ai

    lane = lax.broadcasted_iota(jnp.int32, (GROUP_CH, tl), 1)
    for g in range(gpb):
        rows = slice(g * p, (g + 1) * p)

        def rot_cols(x, g=g):
            if not rotate or g == 0:
                return x
            return jnp.concatenate([_rot_blocks(x[:, h * LANES:(h + 1) * LANES], g) for h in range(tl // LANES)],
                                   axis=1)

        def dest_row(s, g=g):
            if not rotate:
                return GROUP_CH * s
            return LANES * (s // per_half) + GROUP_CH * ((s + g) % per_half)

        wor_ref[g] = rot_cols(wor[rows])[:, :tc].astype(BF16)
        woi_ref[g] = rot_cols(woi[rows])[:, :tc].astype(BF16)

        krow = _dot_f32(bbr[g, :GROUP_CH], rr[rows]) - _dot_f32(bbi[g, :GROUP_CH], ri[rows])
        for s in range(chunk):
            if s == 0:
                blk = krow
            else:
                blk = jnp.where(lane >= GROUP_CH * s, pltpu.roll(krow, GROUP_CH * s, 1), 0.0)
            dst = slice(dest_row(s), dest_row(s) + GROUP_CH)
            src = slice(GROUP_CH * s, GROUP_CH * (s + 1))
            toep_ref[g, dst, :] = rot_cols(blk)[:, :tc].astype(BF16)
            wsr_ref[g, dst, :] = wsr[g, src]
            wsi_ref[g, dst, :] = wsi[g, src]


def _s5_prep(a_re, a_im, log_dt, b_re, b_im, c_re, c_im, *, chunk, rotate):
    p = a_re.shape[-1]
    a_re, a_im = a_re.reshape(-1, p), a_im.reshape(-1, p)
    log_dt = log_dt.reshape(-1)
    b_re, b_im = b_re.reshape((-1,) + b_re.shape[-2:]), b_im.reshape((-1,) + b_im.shape[-2:])
    c_re, c_im = c_re.reshape((-1,) + c_re.shape[-2:]), c_im.reshape((-1,) + c_im.shape[-2:])
    g = a_re.shape[0]
    gpb = LANES // GROUP_CH
    tc = chunk * GROUP_CH
    tl = max(tc, LANES)
    ldt = jnp.broadcast_to(log_dt[:, None], (g, p))
    pcol = jnp.stack([a_re, a_im, ldt], axis=-1)
    prow = jnp.stack([a_re, a_im, ldt], axis=1)
    bt_re = jnp.tile(jnp.swapaxes(b_re, 1, 2), (1, chunk, 1))
    bt_im = jnp.tile(jnp.swapaxes(b_im, 1, 2), (1, chunk, 1))
    ct_re = jnp.tile(jnp.swapaxes(c_re, 1, 2), (1, 1, tl // GROUP_CH))
    ct_im = jnp.tile(jnp.swapaxes(c_im, 1, 2), (1, 1, tl // GROUP_CH))

    def spec(a, b):
        return pl.BlockSpec((gpb, a, b), lambda i: (i, 0, 0))

    return pl.pallas_call(
        functools.partial(_s5_prep_kernel, chunk=chunk, rotate=rotate),
        grid=(g // gpb,),
        in_specs=[spec(p, 3), spec(3, p), spec(tc, p), spec(tc, p), spec(p, tl), spec(p, tl)],
        out_specs=[spec(tc, tc), spec(tc, p), spec(tc, p), spec(p, tc), spec(p, tc), spec(2, p)],
        out_shape=[
            jax.ShapeDtypeStruct((g, tc, tc), BF16),
            jax.ShapeDtypeStruct((g, tc, p), BF16),
            jax.ShapeDtypeStruct((g, tc, p), BF16),
            jax.ShapeDtypeStruct((g, p, tc), BF16),
            jax.ShapeDtypeStruct((g, p, tc), BF16),
            jax.ShapeDtypeStruct((g, 2, p), F32),
        ],
        compiler_params=_params("parallel"),
        name=f"s5_prep_t{chunk}",
    )(pcol, prow, bt_re, bt_im, ct_re, ct_im)


def _s5_scan_kernel(u_ref, toep_ref, wsr_ref, wsi_ref, wor_ref, woi_ref, at_ref, h0r_ref, h0i_ref,
                    y_ref, hr_ref, hi_ref, sr_s, si_s, pr_s, pi_s, *, n_chunks, bsz):
    for g in range(u_ref.shape[0]):
        sr_s[g] = _dot(u_ref[g], wsr_ref[g])
        si_s[g] = _dot(u_ref[g], wsi_ref[g])
    ar = at_ref[:, 0:1, :]
    ai = at_ref[:, 1:2, :]

    def body(k, carry):
        hr, hi = carry
        rows = pl.ds(pl.multiple_of(k * bsz, bsz), bsz)
        pr_s[:, rows, :] = hr
        pi_s[:, rows, :] = hi
        return (ar * hr - ai * hi + sr_s[:, rows, :], ar * hi + ai * hr + si_s[:, rows, :])

    hr, hi = lax.fori_loop(0, n_chunks, body, (h0r_ref[...], h0i_ref[...]))
    hr_ref[...] = hr
    hi_ref[...] = hi
    for g in range(u_ref.shape[0]):
        y_ref[g] = (_dot(u_ref[g], toep_ref[g]) + _dot(pr_s[g].astype(BF16), wor_ref[g])
                    + _dot(pi_s[g].astype(BF16), woi_ref[g]))


def _s5_scan(u_l, ops, h0r, h0i, *, layer, n_chunks, bsz):
    toep, wsr, wsi, wor, woi, at = ops
    g, n, tc = u_l.shape
    p = STATE_DIM
    gpb = LANES // GROUP_CH
    nblk = g // gpb

    def spec(a, b):
        return pl.BlockSpec((gpb, a, b), lambda i: (i, 0, 0))

    def ospec(a, b):
        return pl.BlockSpec((gpb, a, b), lambda i: (layer * nblk + i, 0, 0))

    return pl.pallas_call(
        functools.partial(_s5_scan_kernel, n_chunks=n_chunks, bsz=bsz),
        grid=(nblk,),
        in_specs=[spec(n, tc), ospec(tc, tc), ospec(tc, p), ospec(tc, p), ospec(p, tc), ospec(p, tc),
                  ospec(2, p), spec(bsz, p), spec(bsz, p)],
        out_specs=[spec(n, tc), spec(bsz, p), spec(bsz, p)],
        out_shape=[
            jax.ShapeDtypeStruct((g, n, tc), F32),
            jax.ShapeDtypeStruct((g, bsz, p), F32),
            jax.ShapeDtypeStruct((g, bsz, p), F32),
        ],
        scratch_shapes=[pltpu.VMEM((gpb, n, p), F32)] * 4,
        compiler_params=_params("parallel"),
        name=f"s5_scan_n{n}",
    )(u_l, toep, wsr, wsi, wor, woi, at, h0r, h0i)


def _rot_blocks(x, k):
    k %= LANES // GROUP_CH
    return x if k == 0 else pltpu.roll(x, k * GROUP_CH, 1)


def _merge_blocks(xs, offset):
    n = LANES // GROUP_CH
    blk = lax.broadcasted_iota(jnp.int32, xs[0].shape, 1) >> (GROUP_CH.bit_length() - 1)
    out = xs[(-offset) % n]
    for b in range(1, n):
        out = jnp.where(blk == b, xs[(b - offset) % n], out)
    return out


def _s5_scan_fused_kernel(u_ref, toep_ref, wsr_ref, wsi_ref, wor_ref, woi_ref, at_ref, h0r_ref, h0i_ref, d_ref,
                          a_ref, hr_ref, hi_ref, hr_s, hi_s, sr_s, si_s, pr_s, pi_s, y_s, *, bsz):
    r = pl.program_id(1)
    _, n_t, rows, _ = u_ref.shape
    gpb = toep_ref.shape[0]
    n_half = n_t // 8

    @pl.when(r == 0)
    def _():
        hr_s[...] = h0r_ref[...]
        hi_s[...] = h0i_ref[...]

    rot = [_rot_blocks(u_ref[0, t], t).astype(BF16) for t in range(n_t)]
    for g in range(gpb):
        u_g = jnp.concatenate([_merge_blocks(rot[h * 8:(h + 1) * 8], g) for h in range(n_half)], axis=1)
        y_s[g] = _dot(u_g, toep_ref[g])
        sr_s[g] = _dot(u_g, wsr_ref[g])
        si_s[g] = _dot(u_g, wsi_ref[g])

    ar = at_ref[:, 0:1, :]
    ai = at_ref[:, 1:2, :]

    def body(k, carry):
        hr, hi = carry
        rs = pl.ds(pl.multiple_of(k * bsz, bsz), bsz)
        pr_s[:, rs, :] = hr
        pi_s[:, rs, :] = hi
        return (ar * hr - ai * hi + sr_s[:, rs, :], ar * hi + ai * hr + si_s[:, rs, :])

    hr, hi = lax.fori_loop(0, rows // bsz, body, (hr_s[...], hi_s[...]))
    hr_s[...] = hr
    hi_s[...] = hi

    @pl.when(r == pl.num_programs(1) - 1)
    def _():
        hr_ref[...] = hr
        hi_ref[...] = hi

    ys = []
    for g in range(gpb):
        ys.append(y_s[g] + _dot(pr_s[g].astype(BF16), wor_ref[g]) + _dot(pi_s[g].astype(BF16), woi_ref[g]))
    d = d_ref[...]
    for h in range(n_half):
        y_half = [y[:, h * LANES:(h + 1) * LANES] for y in ys]
        for tl in range(8):
            t = h * 8 + tl
            y_t = _rot_blocks(_merge_blocks(y_half, tl), -tl)
            a_ref[t] = jax.nn.gelu(y_t + d * u_ref[0, t]).astype(BF16)


def _s5_scan_fused(uz4, ops, h0r, h0i, d_skip, *, layer, bsz, rows):
    toep, wsr, wsi, wor, woi, at = ops
    _, n_t, n, width = uz4.shape
    g = width // GROUP_CH
    tc = toep.shape[1]
    p = STATE_DIM
    gpb = LANES // GROUP_CH
    nblk = g // gpb
    assert n_t % 8 == 0 and tc == n_t * GROUP_CH and n % rows == 0 and rows % bsz == 0

    def gspec(a, b):
        return pl.BlockSpec((gpb, a, b), lambda i, r: (i, 0, 0))

    def ospec(a, b):
        return pl.BlockSpec((gpb, a, b), lambda i, r: (layer * nblk + i, 0, 0))

    return pl.pallas_call(
        functools.partial(_s5_scan_fused_kernel, bsz=bsz),
        grid=(nblk, n // rows),
        in_specs=[
            pl.BlockSpec((1, n_t, rows, LANES), lambda i, r: (0, 0, r, i)),
            ospec(tc, tc), ospec(tc, p), ospec(tc, p), ospec(p, tc), ospec(p, tc), ospec(2, p),
            gspec(bsz, p), gspec(bsz, p),
            pl.BlockSpec((1, LANES), lambda i, r: (0, i)),
        ],
        out_specs=[
            pl.BlockSpec((n_t, rows, LANES), lambda i, r: (0, r, i)),
            gspec(bsz, p), gspec(bsz, p),
        ],
        out_shape=[
            jax.ShapeDtypeStruct((n_t, n, width), BF16),
            jax.ShapeDtypeStruct((g, bsz, p), F32),
            jax.ShapeDtypeStruct((g, bsz, p), F32),
        ],
        scratch_shapes=[pltpu.VMEM((gpb, bsz, p), F32)] * 2 + [pltpu.VMEM((gpb, rows, p), F32)] * 4
        + [pltpu.VMEM((gpb, rows, tc), F32)],
        compiler_params=_params("parallel", "arbitrary"),
        name="s5_scan_fused",
    )(uz4, toep, wsr, wsi, wor, woi, at, h0r, h0i, d_skip)


def _s5_act_kernel(y_ref, u_ref, d_ref, a_ref):
    a_ref[...] = jax.nn.gelu(y_ref[...] + d_ref[...] * u_ref[0]).astype(BF16)


def _s5_act(y, uz, d_skip):
    m, w = y.shape
    return pl.pallas_call(
        _s5_act_kernel,
        grid=(1,),
        in_specs=[
            pl.BlockSpec((m, w), lambda i: (0, 0)),
            pl.BlockSpec((1, m, w), lambda i: (0, 0, 0)),
            pl.BlockSpec((1, w), lambda i: (0, 0)),
        ],
        out_specs=pl.BlockSpec((m, w), lambda i: (0, 0)),
        out_shape=jax.ShapeDtypeStruct((m, w), BF16),
        compiler_params=_params("arbitrary"),
        name="s5_act_sample",
    )(y, uz, d_skip)


def _s5_glu_out_kernel(a_ref, z_ref, x_ref, wa_ref, wb_ref, ba_ref, bb_ref, wo_ref, o_ref, acc_s):
    j = pl.program_id(1)

    @pl.when(j == 0)
    def _():
        acc_s[...] = jnp.zeros(acc_s.shape, F32)

    rs = a_ref.shape[0] // GLU_ROW_SPLIT
    for h in range(GLU_ROW_SPLIT):
        rows = slice(h * rs, (h + 1) * rs)
        a = a_ref[rows, :]
        ga = _dot(a, wa_ref[0]) + ba_ref[0]
        gb = _dot(a, wb_ref[0]) + bb_ref[0]
        z = z_ref[0, rows, :]
        v = (ga * jax.nn.sigmoid(gb) * (z * jax.nn.sigmoid(z))).astype(BF16)
        acc_s[rows, :] += _dot(v, wo_ref[...])

    @pl.when(j == pl.num_programs(1) - 1)
    def _():
        o_ref[...] = x_ref[...] + acc_s[...]


def _s5_glu_out(a, uz, x, w_glu2, b_glu2, w_out, *, tm, tn, name):
    m, w = a.shape
    d = x.shape[1]
    tm = min(tm, m)
    return pl.pallas_call(
        _s5_glu_out_kernel,
        grid=(m // tm, w // tn),
        in_specs=[
            pl.BlockSpec((tm, w), lambda i, j: (i, 0)),
            pl.BlockSpec((1, tm, tn), lambda i, j: (1, i, j)),
            pl.BlockSpec((tm, d), lambda i, j: (i, 0)),
            pl.BlockSpec((1, w, tn), lambda i, j: (0, 0, j)),
            pl.BlockSpec((1, w, tn), lambda i, j: (1, 0, j)),
            pl.BlockSpec((1, 1, tn), lambda i, j: (0, 0, j)),
            pl.BlockSpec((1, 1, tn), lambda i, j: (1, 0, j)),
            pl.BlockSpec((tn, d), lambda i, j: (j, 0)),
        ],
        out_specs=pl.BlockSpec((tm, d), lambda i, j: (i, 0)),
        out_shape=jax.ShapeDtypeStruct((m, d), F32),
        scratch_shapes=[pltpu.VMEM((tm, d), F32)],
        compiler_params=_params("parallel", "arbitrary"),
        name=name,
    )(a, uz, x, w_glu2, w_glu2, b_glu2, b_glu2, w_out)


def _latent_kernel(x_ref, g_ref, wl_ref, wr_ref, ws_ref, gl_ref, cc_ref, ss_ref, lat_ref, kr_ref, kcat_ref):
    h = _rms(x_ref[...], g_ref[...]).astype(BF16)
    lat = _rms(_dot(h, wl_ref[...]), gl_ref[...])
    kr = _dot(h, wr_ref[...]) * cc_ref[...] + _dot(h, ws_ref[...]) * ss_ref[...]
    lat_ref[...] = lat
    kr_ref[...] = kr[:, :QK_ROPE]
    kcat_ref[:, :KV_LORA] = lat.astype(BF16)
    kcat_ref[:, KV_LORA:] = kr.astype(BF16)


def _latent(x, g, wl, wr, ws, gl, cc, ss, *, tm, name):
    m, d = x.shape
    tm = min(tm, m)
    nt = cc.shape[0] // tm

    def full(a):
        return pl.BlockSpec(a.shape, lambda i: (0,) * a.ndim)

    return pl.pallas_call(
        _latent_kernel,
        grid=(m // tm,),
        in_specs=[
            pl.BlockSpec((tm, d), lambda i: (i, 0)),
            full(g), full(wl), full(wr), full(ws), full(gl),
            pl.BlockSpec((tm, LANES), lambda i: (i % nt, 0)),
            pl.BlockSpec((tm, LANES), lambda i: (i % nt, 0)),
        ],
        out_specs=[
            pl.BlockSpec((tm, KV_LORA), lambda i: (i, 0)),
            pl.BlockSpec((tm, QK_ROPE), lambda i: (i, 0)),
            pl.BlockSpec((tm, QK_PAD), lambda i: (i, 0)),
        ],
        out_shape=[
            jax.ShapeDtypeStruct((m, KV_LORA), F32),
            jax.ShapeDtypeStruct((m, QK_ROPE), F32),
            jax.ShapeDtypeStruct((m, QK_PAD), BF16),
        ],
        compiler_params=_params("parallel"),
        name=name,
    )(x, g, wl, wr, ws, gl, cc, ss)


def _mla_front_kernel(x_ref, g_ref, wcq_ref, wg_ref, gq_ref, wqn_ref, wqr_ref, wqs_ref, wuk_ref,
                      cc_ref, ss_ref, q_ref, sg_ref):
    h = _rms(x_ref[...], g_ref[...]).astype(BF16)
    cq = _rms(_dot(h, wcq_ref[...]), gq_ref[...]).astype(BF16)
    gate = _dot(h, wg_ref[...])
    sg_ref[...] = gate * jax.nn.sigmoid(gate)
    qn = _dot(cq, wqn_ref[...]).astype(BF16)
    qr = _dot(cq, wqr_ref[...])
    qs = _dot(cq, wqs_ref[...])
    cc = cc_ref[...]
    ss = ss_ref[...]
    for hd in range(N_HEADS):
        sl = slice(hd * LANES, (hd + 1) * LANES)
        q_ref[0, hd, :, :KV_LORA] = _dot(qn[:, sl], wuk_ref[hd]).astype(BF16)
        q_ref[0, hd, :, KV_LORA:] = (qr[:, sl] * cc + qs[:, sl] * ss).astype(BF16)


def _mla_front(x, g, wcq, wg, gq, wqn, wqr, wqs, wuk, cc, ss, *, tm, name):
    m, d = x.shape
    tm = min(tm, m)
    nt = cc.shape[0] // tm

    def full(a):
        return pl.BlockSpec(a.shape, lambda i: (0,) * a.ndim)

    return pl.pallas_call(
        _mla_front_kernel,
        grid=(m // tm,),
        in_specs=[
            pl.BlockSpec((tm, d), lambda i: (i, 0)),
            full(g), full(wcq), full(wg), full(gq), full(wqn), full(wqr), full(wqs), full(wuk),
            pl.BlockSpec((tm, LANES), lambda i: (i % nt, 0)),
            pl.BlockSpec((tm, LANES), lambda i: (i % nt, 0)),
        ],
        out_specs=[
            pl.BlockSpec((1, N_HEADS, tm, QK_PAD), lambda i: (i, 0, 0, 0)),
            pl.BlockSpec((tm, N_HEADS * V_HEAD), lambda i: (i, 0)),
        ],
        out_shape=[
            jax.ShapeDtypeStruct((m // tm, N_HEADS, tm, QK_PAD), BF16),
            jax.ShapeDtypeStruct((m, N_HEADS * V_HEAD), F32),
        ],
        compiler_params=_params("parallel"),
        name=name,
    )(x, g, wcq, wg, gq, wqn, wqr, wqs, wuk, cc, ss)


def _lane_tile(x, width):
    return x if width == LANES else jnp.concatenate([x] * (width // LANES), axis=1)


def _softmax_update(s, kl, m_ref, l_ref, acc_ref, rows):
    m_prev = m_ref[rows, :]
    m_new = jnp.maximum(m_prev, jnp.max(s, axis=1, keepdims=True))
    alpha = jnp.exp2(m_prev - m_new)
    p = jnp.exp2(s - _lane_tile(m_new, s.shape[1]))
    l_ref[rows, :] = alpha * l_ref[rows, :] + jnp.sum(p, axis=1, keepdims=True)
    acc_ref[rows, :] = _lane_tile(alpha, kl.shape[1]) * acc_ref[rows, :] + _dot(p.astype(BF16), kl)
    m_ref[rows, :] = m_new


def _attn_prompt_kernel(qi_ref, ki_ref, q_ref, k_ref, o_ref, m_s, l_s, acc_s, *, tq, tk, heads_per_chunk):
    step = pl.program_id(1)
    qi = qi_ref[step]
    ki = ki_ref[step]
    diag = (qi * tq + (tq - 1)) // tk
    rc = heads_per_chunk * tq

    @pl.when(ki == 0)
    def _():
        m_s[...] = jnp.full(m_s.shape, NEG_INF, F32)
        l_s[...] = jnp.zeros(l_s.shape, F32)
        acc_s[...] = jnp.zeros(acc_s.shape, F32)

    def update(masked):
        k = k_ref[...]
        kl = k[:, :KV_LORA]
        for c in range(N_HEADS // heads_per_chunk):
            q = q_ref[0, c * heads_per_chunk:(c + 1) * heads_per_chunk].reshape(rc, QK_PAD)
            s = _dot_nt(q, k) * SCALE_LOG2E
            if masked:
                tok = qi * tq + (lax.broadcasted_iota(jnp.int32, (rc, tk), 0) & (tq - 1))
                col = ki * tk + lax.broadcasted_iota(jnp.int32, (rc, tk), 1)
                s = jnp.where(col <= tok, s, NEG_INF)
            _softmax_update(s, kl, m_s, l_s, acc_s, pl.ds(c * rc, rc))

    @pl.when(ki < diag)
    def _():
        update(False)

    @pl.when(ki == diag)
    def _():
        update(True)
        for hd in range(N_HEADS):
            rows = pl.ds(hd * tq, tq)
            o_ref[0, hd] = (acc_s[rows, :] / _lane_tile(l_s[rows, :], KV_LORA)).astype(BF16)


def _attn_prompt(q, kcat, *, bsz, seq, tq, tk):
    assert tq & (tq - 1) == 0 and seq % tq == 0 and seq % tk == 0
    nq = seq // tq
    nk = seq // tk
    rows = N_HEADS * tq
    pairs = [(i, j) for i in range(nq) for j in range((i * tq + tq - 1) // tk + 1)]
    qi_arr = jnp.asarray([p[0] for p in pairs], jnp.int32)
    ki_arr = jnp.asarray([p[1] for p in pairs], jnp.int32)
    grid_spec = pltpu.PrefetchScalarGridSpec(
        num_scalar_prefetch=2,
        grid=(bsz, len(pairs)),
        in_specs=[
            pl.BlockSpec((1, N_HEADS, tq, QK_PAD), lambda b, s, qi, ki: (b * nq + qi[s], 0, 0, 0)),
            pl.BlockSpec((tk, QK_PAD), lambda b, s, qi, ki: (b * nk + ki[s], 0)),
        ],
        out_specs=pl.BlockSpec((1, N_HEADS, tq, KV_LORA), lambda b, s, qi, ki: (b * nq + qi[s], 0, 0, 0)),
        scratch_shapes=[
            pltpu.VMEM((rows, LANES), F32),
            pltpu.VMEM((rows, LANES), F32),
            pltpu.VMEM((rows, KV_LORA), F32),
        ],
    )
    return pl.pallas_call(
        functools.partial(_attn_prompt_kernel, tq=tq, tk=tk, heads_per_chunk=ATTN_HEADS_PER_CHUNK),
        grid_spec=grid_spec,
        out_shape=jax.ShapeDtypeStruct((bsz * nq, N_HEADS, tq, KV_LORA), BF16),
        compiler_params=_params("parallel", "arbitrary"),
        name="attn_prompt",
    )(qi_arr, ki_arr, q, kcat)


def _attn_sample_kernel(pt_ref, q_ref, nl_ref, nk_ref, lat_hbm, krt_hbm, o_ref,
                        lat_buf, krt_buf, sems, kl_s, kr_s, m_s, l_s, acc_s, *, n_pg, dec_seq):
    step = pl.program_id(1)
    steps = pl.num_programs(1)
    gstep = pl.program_id(0) * steps + step
    total = pl.num_programs(0) * steps
    slot = gstep & 1

    def page_copies(at, buf_slot):
        out = []
        for i in range(n_pg):
            page = pt_ref[at * n_pg + i]
            out.append(pltpu.make_async_copy(lat_hbm.at[page], lat_buf.at[buf_slot, i], sems.at[0, buf_slot]))
            out.append(pltpu.make_async_copy(krt_hbm.at[page], krt_buf.at[buf_slot, i], sems.at[1, buf_slot]))
        return out

    @pl.when(gstep == 0)
    def _():
        for c in page_copies(0, 0):
            c.start()

    @pl.when(gstep + 1 < total)
    def _():
        for c in page_copies(gstep + 1, 1 - slot):
            c.start()

    @pl.when(step == 0)
    def _():
        m_s[...] = jnp.full(m_s.shape, NEG_INF, F32)
        l_s[...] = jnp.zeros(l_s.shape, F32)
        acc_s[...] = jnp.zeros(acc_s.shape, F32)

    for c in page_copies(gstep, slot):
        c.wait()

    for i in range(n_pg):
        kl_s[i * PAGE_SIZE:(i + 1) * PAGE_SIZE, :] = lat_buf[slot, i].astype(BF16)
        kr_s[:, i * PAGE_SIZE:(i + 1) * PAGE_SIZE] = krt_buf[slot, i].astype(BF16)

    q = q_ref[0]
    ql = q[:, :KV_LORA]
    qr = q[:, KV_LORA:KV_LORA + QK_ROPE]
    kl = kl_s[...]
    s = (_dot_nt(ql, kl) + _dot(qr, kr_s[...])) * SCALE_LOG2E
    _softmax_update(s, kl, m_s, l_s, acc_s, slice(None))

    @pl.when(step == pl.num_programs(1) - 1)
    def _():
        nl = nl_ref[0].astype(BF16).astype(F32)
        nk = nk_ref[0].astype(BF16).astype(F32)
        qlf = ql.astype(F32)
        qrf = qr.astype(F32)
        tok = lax.broadcasted_iota(jnp.int32, (q.shape[0], 1), 0) & (dec_seq - 1)
        m = m_s[...]
        l = l_s[...]
        acc = acc_s[...]
        for j in range(dec_seq):
            sj = (jnp.sum(qlf * nl[j:j + 1], axis=-1, keepdims=True)
                  + jnp.sum(qrf * nk[j:j + 1], axis=-1, keepdims=True)) * SCALE_LOG2E
            sj = jnp.where(tok >= j, sj, NEG_INF)
            m_new = jnp.maximum(m, sj)
            alpha = jnp.exp2(m - m_new)
            pj = jnp.exp2(sj - m_new)
            l = alpha * l + pj
            pv = _lane_tile(pj, KV_LORA).astype(BF16).astype(F32) * nl[j:j + 1]
            acc = _lane_tile(alpha, KV_LORA) * acc + pv
            m = m_new
        o_ref[0] = (acc / _lane_tile(l, KV_LORA)).astype(BF16)


def _attn_sample(q, cache_latent, cache_krope_t, page_table, new_lat, new_kr, *, n_pg):
    n_seq, n_pages = page_table.shape
    rows = q.shape[1]
    dec_seq = new_lat.shape[1]
    assert dec_seq & (dec_seq - 1) == 0 and n_pages % n_pg == 0
    steps = n_pages // n_pg
    pt = page_table.reshape(-1)

    def seq_spec(a, b):
        return pl.BlockSpec((1, a, b), lambda n, s, pt_ref: (n, 0, 0))

    grid_spec = pltpu.PrefetchScalarGridSpec(
        num_scalar_prefetch=1,
        grid=(n_seq, steps),
        in_specs=[seq_spec(rows, QK_PAD), seq_spec(dec_seq, KV_LORA), seq_spec(dec_seq, QK_ROPE),
                  pl.BlockSpec(memory_space=pl.ANY), pl.BlockSpec(memory_space=pl.ANY)],
        out_specs=seq_spec(rows, KV_LORA),
        scratch_shapes=[
            pltpu.VMEM((2, n_pg, PAGE_SIZE, KV_LORA), F32),
            pltpu.VMEM((2, n_pg, QK_ROPE, PAGE_SIZE), F32),
            pltpu.SemaphoreType.DMA((2, 2)),
            pltpu.VMEM((n_pg * PAGE_SIZE, KV_LORA), BF16),
            pltpu.VMEM((QK_ROPE, n_pg * PAGE_SIZE), BF16),
            pltpu.VMEM((rows, LANES), F32),
            pltpu.VMEM((rows, LANES), F32),
            pltpu.VMEM((rows, KV_LORA), F32),
        ],
    )
    return pl.pallas_call(
        functools.partial(_attn_sample_kernel, n_pg=n_pg, dec_seq=dec_seq),
        grid_spec=grid_spec,
        out_shape=jax.ShapeDtypeStruct((n_seq, rows, KV_LORA), BF16),
        compiler_params=_params("arbitrary", "arbitrary"),
        name="attn_sample",
    )(pt, q, new_lat, new_kr, cache_latent, cache_krope_t)


def _mla_back_kernel(o_ref, sg_ref, x_ref, wuv_ref, wo_ref, gf_ref, out_ref, *, final_norm):
    parts = []
    for hd in range(N_HEADS):
        oh = _dot(o_ref[0, hd], wuv_ref[hd])
        parts.append((oh * sg_ref[:, hd * V_HEAD:(hd + 1) * V_HEAD]).astype(BF16))
    y = x_ref[...] + _dot(jnp.concatenate(parts, axis=1), wo_ref[...])
    out_ref[...] = _rms(y, gf_ref[...]) if final_norm else y


def _mla_back(o, sg, x, wuv, wo, gf, *, final_norm, name):
    m, d = x.shape
    tm = o.shape[2]

    def full(a):
        return pl.BlockSpec(a.shape, lambda i: (0,) * a.ndim)

    def rows(a):
        return pl.BlockSpec((tm, a.shape[1]), lambda i: (i, 0))

    return pl.pallas_call(
        functools.partial(_mla_back_kernel, final_norm=final_norm),
        grid=(m // tm,),
        in_specs=[pl.BlockSpec((1,) + o.shape[1:], lambda i: (i, 0, 0, 0)), rows(sg), rows(x), full(wuv), full(wo),
                  full(gf)],
        out_specs=pl.BlockSpec((tm, d), lambda i: (i, 0)),
        out_shape=jax.ShapeDtypeStruct((m, d), F32),
        compiler_params=_params("parallel"),
        name=name,
    )(o, sg, x, wuv, wo, gf)


def _to_scan_layout(u, bsz, seq, chunk):
    k = seq // chunk
    g = u.shape[1] // GROUP_CH
    u5 = u.astype(BF16).reshape(bsz, k, chunk, g, GROUP_CH)
    return u5.transpose(3, 1, 0, 2, 4).reshape(g, k * bsz, chunk * GROUP_CH)


def _from_scan_layout(y, bsz, seq, chunk):
    k = seq // chunk
    g = y.shape[0]
    y5 = y.reshape(g, k, bsz, chunk, GROUP_CH)
    return y5.transpose(2, 1, 3, 0, 4).reshape(bsz * seq, g * GROUP_CH)


def _rope_tables(pos):
    half = QK_ROPE // 2
    inv = ROPE_THETA ** (-jnp.arange(half, dtype=F32) / half)
    ang = pos.astype(F32)[:, None] * inv[None, :]
    cos = jnp.cos(ang)
    sin = jnp.sin(ang)
    pad = jnp.zeros((pos.shape[0], LANES - QK_ROPE), F32)
    cc = jnp.concatenate([cos, cos, pad], axis=1)
    ss = jnp.concatenate([-sin, sin, pad], axis=1)
    return cc, ss


def _swap_halves(w):
    half = w.shape[-1] // 2
    return jnp.concatenate([w[..., half:], w[..., :half]], axis=-1)


def _pad_lanes(w):
    pad = [(0, 0)] * (w.ndim - 1) + [(0, LANES - w.shape[-1])]
    return jnp.pad(w, pad)


def kernel(x_prompt, x_sample, cache_latent, cache_krope, page_table, state_ssm_re, state_ssm_im, norm_a, w_in_a, a_re, a_im, log_dt, b_re, b_im, c_re, c_im, d_skip, w_glu, b_glu, w_out_a, norm_kv, w_dkv, norm_latent, w_uk, w_uv, norm_b, w_in_b, norm_q, w_uq, w_out_b, norm_f):
    bsz, seq, d = x_prompt.shape
    dbs, dseq, _ = x_sample.shape
    n_a = norm_a.shape[0]
    n_b = norm_b.shape[0]
    width = d_skip.shape[1]
    n_groups = width // GROUP_CH
    past_len = page_table.shape[1] * PAGE_SIZE

    n_chunks = seq // PROMPT_CHUNK
    xp = x_prompt.reshape(bsz, n_chunks, PROMPT_CHUNK, d).transpose(2, 1, 0, 3).reshape(bsz * seq, d)
    xs = x_sample.reshape(dbs * dseq, d)

    hp_re, hp_im, hs_re, hs_im = [], [], [], []
    zeros_p = jnp.zeros((n_groups, bsz, STATE_DIM), F32)
    ops_p = _s5_prep(a_re, a_im, log_dt, b_re, b_im, c_re, c_im, chunk=PROMPT_CHUNK, rotate=True)
    ops_s = _s5_prep(a_re, a_im, log_dt, b_re, b_im, c_re, c_im, chunk=dseq, rotate=False)
    for i in range(n_a):
        w_in3 = w_in_a[i].astype(BF16).reshape(d, 2, width).transpose(1, 0, 2)
        w_glu2 = w_glu[i].astype(BF16).reshape(width, 2, width).transpose(1, 0, 2)
        b_glu2 = b_glu[i].reshape(2, 1, width)
        w_out = w_out_a[i].astype(BF16)
        g = norm_a[i][None, :]
        dsk = d_skip[i][None, :]

        uz = _norm_matmul(xp, g, w_in3, tm=512, tn=1024, name="s5_in_prompt")
        uz4 = uz.reshape(2, PROMPT_CHUNK, n_chunks * bsz, width)
        a, hr, hi = _s5_scan_fused(uz4, ops_p, zeros_p, zeros_p, dsk, layer=i, bsz=bsz, rows=SCAN_ROWS)
        xp = _s5_glu_out(a.reshape(bsz * seq, width), uz, xp, w_glu2, b_glu2, w_out, tm=1024, tn=512,
                         name="s5_glu_out_prompt")
        hp_re.append(hr.transpose(1, 0, 2))
        hp_im.append(hi.transpose(1, 0, 2))

        uz = _norm_matmul(xs, g, w_in3, tm=512, tn=1024, name="s5_in_sample")
        u_l = _to_scan_layout(uz[0], dbs, dseq, dseq)
        y_l, hr, hi = _s5_scan(u_l, ops_s, state_ssm_re[i].transpose(1, 0, 2), state_ssm_im[i].transpose(1, 0, 2),
                               layer=i, n_chunks=1, bsz=dbs)
        y = _from_scan_layout(y_l, dbs, dseq, dseq)
        a = _s5_act(y, uz, dsk)
        xs = _s5_glu_out(a, uz, xs, w_glu2, b_glu2, w_out, tm=512, tn=512, name="s5_glu_out_sample")
        hs_re.append(hr.transpose(1, 0, 2))
        hs_im.append(hi.transpose(1, 0, 2))

    xp = xp.reshape(PROMPT_CHUNK, n_chunks, bsz, d).transpose(2, 1, 0, 3).reshape(bsz * seq, d)

    cc_p, ss_p = _rope_tables(jnp.arange(seq, dtype=jnp.int32))
    pos_s = past_len + jnp.arange(dseq, dtype=jnp.int32)
    cc_s, ss_s = _rope_tables(jnp.tile(pos_s, dbs))
    w_lat = w_dkv[:, :KV_LORA].astype(BF16)
    w_kr = _pad_lanes(w_dkv[:, KV_LORA:]).astype(BF16)
    w_ks = _pad_lanes(_swap_halves(w_dkv[:, KV_LORA:])).astype(BF16)
    g_kv = norm_kv[None, :]
    g_lat = norm_latent[None, :]
    lat_p, kr_p, kcat_p = _latent(xp, g_kv, w_lat, w_kr, w_ks, g_lat, cc_p, ss_p, tm=512, name="latent_prompt")
    lat_s, kr_s, _ = _latent(xs, g_kv, w_lat, w_kr, w_ks, g_lat, cc_s, ss_s, tm=512, name="latent_sample")
    new_lat = lat_s.reshape(dbs, dseq, KV_LORA)
    new_kr = kr_s.reshape(dbs, dseq, QK_ROPE)

    cache_krope_t = jnp.swapaxes(cache_krope, 1, 2)
    w_ukt = w_uk.astype(BF16).reshape(KV_LORA, N_HEADS, QK_NOPE).transpose(1, 2, 0)
    w_uvh = w_uv.astype(BF16).reshape(KV_LORA, N_HEADS, V_HEAD).transpose(1, 0, 2)
    g_f = norm_f[None, :]
    assert n_b > 0
    for j in range(n_b):
        g = norm_b[j][None, :]
        w_cq = w_in_b[j][:, :Q_LORA].astype(BF16)
        w_gate = w_in_b[j][:, Q_LORA:].astype(BF16)
        g_q = norm_q[j][None, :]
        wq3 = w_uq[j].reshape(Q_LORA, N_HEADS, QK_NOPE + QK_ROPE)
        w_qn = wq3[:, :, :QK_NOPE].reshape(Q_LORA, N_HEADS * QK_NOPE).astype(BF16)
        w_qr = _pad_lanes(wq3[:, :, QK_NOPE:]).reshape(Q_LORA, N_HEADS * LANES).astype(BF16)
        w_qs = _pad_lanes(_swap_halves(wq3[:, :, QK_NOPE:])).reshape(Q_LORA, N_HEADS * LANES).astype(BF16)
        w_o = w_out_b[j].astype(BF16)

        q, sg = _mla_front(xp, g, w_cq, w_gate, g_q, w_qn, w_qr, w_qs, w_ukt, cc_p, ss_p, tm=ATTN_TQ,
                           name="mla_front_prompt")
        o = _attn_prompt(q, kcat_p, bsz=bsz, seq=seq, tq=ATTN_TQ, tk=ATTN_TK)
        last = j == n_b - 1
        xp = _mla_back(o, sg, xp, w_uvh, w_o, g_f, final_norm=last, name="mla_back_prompt")

        q, sg = _mla_front(xs, g, w_cq, w_gate, g_q, w_qn, w_qr, w_qs, w_ukt, cc_s, ss_s, tm=ATTN_TQ,
                           name="mla_front_sample")
        spt = ATTN_TQ // dseq
        q_seq = q.reshape(-1, N_HEADS, spt, dseq, QK_PAD).transpose(0, 2, 1, 3, 4)
        o = _attn_sample(q_seq.reshape(dbs, N_HEADS * dseq, QK_PAD), cache_latent, cache_krope_t, page_table,
                         new_lat, new_kr, n_pg=SAMPLE_PAGES_PER_STEP)
        o = o.reshape(-1, spt, N_HEADS, dseq, KV_LORA).transpose(0, 2, 1, 3, 4)
        xs = _mla_back(o.reshape(-1, N_HEADS, ATTN_TQ, KV_LORA), sg, xs, w_uvh, w_o, g_f, final_norm=last,
                       name="mla_back_sample")

    return (xp.reshape(bsz, seq, d), xs.reshape(dbs, dseq, d),
            lat_p.reshape(bsz, seq, KV_LORA), kr_p.reshape(bsz, seq, QK_ROPE),
            new_lat, new_kr,
            jnp.stack(hp_re), jnp.stack(hp_im), jnp.stack(hs_re), jnp.stack(hs_im))
```

```python
import functools
import math

import jax
import jax.numpy as jnp
from jax import lax
from jax.experimental import pallas as pl
from jax.experimental.pallas import tpu as pltpu

F32 = jnp.float32
BF16 = jnp.bfloat16

GROUP_CH = 16
STATE_DIM = 64
N_HEADS = 8
QK_NOPE = 128
QK_ROPE = 64
V_HEAD = 128
KV_LORA = 256
Q_LORA = 384
PAGE_SIZE = 128
ROPE_THETA = 10000.0
RMS_EPS = 1e-6
SOFTMAX_SCALE = 1.0 / math.sqrt(QK_NOPE + QK_ROPE)
SCALE_LOG2E = SOFTMAX_SCALE * math.log2(math.e)
NEG_INF = -1e30
LANES = 128
QK_PAD = KV_LORA + LANES
PROMPT_CHUNK = 16
SAMPLE_CHUNK = 8
ATTN_TQ = 512
ATTN_TK = 512
ATTN_HEADS_PER_CHUNK = 2
SAMPLE_PAGES_PER_STEP = 64
SCAN_ROWS = 512
GLU_ROW_SPLIT = 4
VMEM_LIMIT = 56 * 1024 * 1024


def _dot(a, b):
    return jnp.dot(a, b, preferred_element_type=F32)


def _dot_nt(a, b):
    return lax.dot_general(a, b, (((1,), (1,)), ((), ())), preferred_element_type=F32)


def _dot_f32(a, b):
    return jnp.dot(a, b, preferred_element_type=F32, precision=lax.Precision.HIGHEST)


def _rms(x, g):
    return x * lax.rsqrt(jnp.mean(x * x, axis=-1, keepdims=True) + RMS_EPS) * g


def _params(*sem):
    return pltpu.CompilerParams(dimension_semantics=sem, vmem_limit_bytes=VMEM_LIMIT)


def _norm_matmul_kernel(x_ref, g_ref, w_ref, o_ref, *, tn):
    h = _rms(x_ref[...], g_ref[...]).astype(BF16)
    parts, _, n = o_ref.shape
    for p in range(parts):
        for j in range(n // tn):
            o_ref[p, :, j * tn:(j + 1) * tn] = _dot(h, w_ref[:, p * n + j * tn:p * n + (j + 1) * tn])


def _norm_matmul(x, g, w, *, parts, tm, tn, name):
    m, d = x.shape
    n = w.shape[1] // parts
    tm = min(tm, m)
    return pl.pallas_call(
        functools.partial(_norm_matmul_kernel, tn=tn),
        grid=(m // tm,),
        in_specs=[
            pl.BlockSpec((tm, d), lambda i: (i, 0)),
            pl.BlockSpec((1, d), lambda i: (0, 0)),
            pl.BlockSpec(w.shape, lambda i: (0, 0), pipeline_mode=pl.Buffered(1)),
        ],
        out_specs=pl.BlockSpec((parts, tm, n), lambda i: (0, i, 0)),
        out_shape=jax.ShapeDtypeStruct((parts, m, n), F32),
        compiler_params=_params("parallel"),
        name=name,
    )(x, g, w)


def _discretize(lam_re, lam_im, log_dt):
    dt = jnp.exp(log_dt)
    mag = jnp.exp(lam_re * dt)
    lb_re = mag * jnp.cos(lam_im * dt)
    lb_im = mag * jnp.sin(lam_im * dt)
    den = lam_re * lam_re + lam_im * lam_im
    nr = lb_re - 1.0
    f_re = (nr * lam_re + lb_im * lam_im) / den
    f_im = (lb_im * lam_re - nr * lam_im) / den
    return lb_re, lb_im, f_re, f_im


def _cpow(br, bi, e, nbits, shape):
    br = jnp.broadcast_to(br, shape)
    bi = jnp.broadcast_to(bi, shape)
    rr = jnp.ones(shape, F32)
    ri = jnp.zeros(shape, F32)
    for j in range(nbits):
        bit = ((e >> j) & 1) == 1
        nr = rr * br - ri * bi
        ni = rr * bi + ri * br
        rr = jnp.where(bit, nr, rr)
        ri = jnp.where(bit, ni, ri)
        if j + 1 < nbits:
            br, bi = br * br - bi * bi, 2.0 * br * bi
    return rr, ri


def _s5_prep_kernel(pcol_ref, prow_ref, btr_ref, bti_ref, ctr_ref, cti_ref,
                    toep_ref, wsr_ref, wsi_ref, wor_ref, woi_ref, at_ref, *, chunk, offset):
    gpb, p, tl = ctr_ref.shape
    nbits = chunk.bit_length()
    shift = GROUP_CH.bit_length() - 1
    per_half = LANES // GROUP_CH

    pc = pcol_ref[...].reshape(gpb * p, 3)
    lbr_c, lbi_c, _, _ = _discretize(pc[:, 0:1], pc[:, 1:2], pc[:, 2:3])
    cr = ctr_ref[...].reshape(gpb * p, tl)
    ci = cti_ref[...].reshape(gpb * p, tl)
    shape = cr.shape
    lane_t = lax.broadcasted_iota(jnp.int32, shape, 1) >> shift
    p0r, p0i = _cpow(lbr_c, lbi_c, lane_t, nbits, shape)
    if offset == 0:
        p1r = p0r * lbr_c - p0i * lbi_c
        p1i = p0r * lbi_c + p0i * lbr_c
    else:
        p1r, p1i = _cpow(lbr_c, lbi_c, jnp.maximum(lane_t + (1 - offset), 0), nbits, shape)
    rr = cr * p0r - ci * p0i
    ri = cr * p0i + ci * p0r
    wor = cr * p1r - ci * p1i
    woi = -(cr * p1i + ci * p1r)

    pr = prow_ref[...]
    lbr_r, lbi_r, f_re, f_im = _discretize(pr[:, 0:1, :], pr[:, 1:2, :], pr[:, 2:3, :])
    btr = btr_ref[...]
    bti = bti_ref[...]
    bbr = f_re * btr - f_im * bti
    bbi = f_re * bti + f_im * btr
    shape_s = btr.shape
    row_e = (chunk - 1) - (lax.broadcasted_iota(jnp.int32, shape_s, 1) >> shift)
    qr, qi = _cpow(lbr_r, lbi_r, row_e, nbits, shape_s)
    wsr = (bbr * qr - bbi * qi).astype(BF16)
    wsi = (bbr * qi + bbi * qr).astype(BF16)

    ar, ai = lbr_r, lbi_r
    for _ in range((chunk - offset).bit_length() - 1):
        ar, ai = ar * ar - ai * ai, 2.0 * ar * ai
    at_ref[:, 0:1, :] = ar
    at_ref[:, 1:2, :] = ai

    lane = lax.broadcasted_iota(jnp.int32, (GROUP_CH, tl), 1)
    for g in range(gpb):
        rows = slice(g * p, (g + 1) * p)

        def rot_cols(x, g=g):
            return jnp.concatenate([_rot_blocks(x[:, h * LANES:(h + 1) * LANES], g) for h in range(tl // LANES)],
                                   axis=1)

        def dest_row(s, g=g):
            return LANES * (s // per_half) + GROUP_CH * ((s + g) % per_half)

        wor_ref[g] = rot_cols(wor[rows]).astype(BF16)
        woi_ref[g] = rot_cols(woi[rows]).astype(BF16)

        krow = _dot_f32(bbr[g, :GROUP_CH], rr[rows]) - _dot_f32(bbi[g, :GROUP_CH], ri[rows])
        for s in range(chunk):
            if s == 0:
                blk = krow
            else:
                blk = jnp.where(lane >= GROUP_CH * s, pltpu.roll(krow, GROUP_CH * s, 1), 0.0)
            dst = slice(dest_row(s), dest_row(s) + GROUP_CH)
            src = slice(GROUP_CH * s, GROUP_CH * (s + 1))
            toep_ref[g, dst, :] = rot_cols(blk).astype(BF16)
            wsr_ref[g, dst, :] = wsr[g, src]
            wsi_ref[g, dst, :] = wsi[g, src]


def _s5_prep(a_re, a_im, log_dt, b_re, b_im, c_re, c_im, *, chunk, offset):
    steps = chunk - offset
    assert steps > 0 and steps & (steps - 1) == 0 and chunk & (chunk - 1) == 0
    assert (chunk * GROUP_CH) % LANES == 0
    p = a_re.shape[-1]
    a_re, a_im = a_re.reshape(-1, p), a_im.reshape(-1, p)
    log_dt = log_dt.reshape(-1)
    b_re, b_im = b_re.reshape((-1,) + b_re.shape[-2:]), b_im.reshape((-1,) + b_im.shape[-2:])
    c_re, c_im = c_re.reshape((-1,) + c_re.shape[-2:]), c_im.reshape((-1,) + c_im.shape[-2:])
    g = a_re.shape[0]
    gpb = LANES // GROUP_CH
    tc = chunk * GROUP_CH
    ldt = jnp.broadcast_to(log_dt[:, None], (g, p))
    pcol = jnp.stack([a_re, a_im, ldt], axis=-1)
    prow = jnp.stack([a_re, a_im, ldt], axis=1)
    bt_re = jnp.tile(jnp.swapaxes(b_re, 1, 2), (1, chunk, 1))
    bt_im = jnp.tile(jnp.swapaxes(b_im, 1, 2), (1, chunk, 1))
    ct_re = jnp.tile(jnp.swapaxes(c_re, 1, 2), (1, 1, chunk))
    ct_im = jnp.tile(jnp.swapaxes(c_im, 1, 2), (1, 1, chunk))

    def spec(a, b):
        return pl.BlockSpec((gpb, a, b), lambda i: (i, 0, 0))

    return pl.pallas_call(
        functools.partial(_s5_prep_kernel, chunk=chunk, offset=offset),
        grid=(g // gpb,),
        in_specs=[spec(p, 3), spec(3, p), spec(tc, p), spec(tc, p), spec(p, tc), spec(p, tc)],
        out_specs=[spec(tc, tc), spec(tc, p), spec(tc, p), spec(p, tc), spec(p, tc), spec(2, p)],
        out_shape=[
            jax.ShapeDtypeStruct((g, tc, tc), BF16),
            jax.ShapeDtypeStruct((g, tc, p), BF16),
            jax.ShapeDtypeStruct((g, tc, p), BF16),
            jax.ShapeDtypeStruct((g, p, tc), BF16),
            jax.ShapeDtypeStruct((g, p, tc), BF16),
            jax.ShapeDtypeStruct((g, 2, p), F32),
        ],
        compiler_params=_params("parallel"),
        name=f"s5_prep_t{chunk}",
    )(pcol, prow, bt_re, bt_im, ct_re, ct_im)


def _rot_blocks(x, k):
    k %= LANES // GROUP_CH
    return x if k == 0 else pltpu.roll(x, k * GROUP_CH, 1)


def _merge_blocks(xs, offset):
    n = LANES // GROUP_CH
    blk = lax.broadcasted_iota(jnp.int32, xs[0].shape, 1) >> (GROUP_CH.bit_length() - 1)
    out = xs[(-offset) % n]
    for b in range(1, n):
        out = jnp.where(blk == b, xs[(b - offset) % n], out)
    return out


def _s5_scan_fused_kernel(u_ref, toep_ref, wsr_ref, wsi_ref, wor_ref, woi_ref, at_ref, h0r_ref, h0i_ref, d_ref,
                          a_ref, hr_ref, hi_ref, hr_s, hi_s, sr_s, si_s, pr_s, pi_s, y_s, *, bsz):
    r = pl.program_id(1)
    _, n_t, rows, _ = u_ref.shape
    gpb = toep_ref.shape[0]
    n_half = n_t // 8

    @pl.when(r == 0)
    def _():
        hr_s[...] = h0r_ref[...]
        hi_s[...] = h0i_ref[...]

    rot = [_rot_blocks(u_ref[0, t], t).astype(BF16) for t in range(n_t)]
    for g in range(gpb):
        u_g = jnp.concatenate([_merge_blocks(rot[h * 8:(h + 1) * 8], g) for h in range(n_half)], axis=1)
        y_s[g] = _dot(u_g, toep_ref[g])
        sr_s[g] = _dot(u_g, wsr_ref[g])
        si_s[g] = _dot(u_g, wsi_ref[g])

    ar = at_ref[:, 0:1, :]
    ai = at_ref[:, 1:2, :]

    def body(k, carry):
        hr, hi = carry
        rs = pl.ds(pl.multiple_of(k * bsz, bsz), bsz)
        pr_s[:, rs, :] = hr
        pi_s[:, rs, :] = hi
        return (ar * hr - ai * hi + sr_s[:, rs, :], ar * hi + ai * hr + si_s[:, rs, :])

    hr, hi = lax.fori_loop(0, rows // bsz, body, (hr_s[...], hi_s[...]))
    hr_s[...] = hr
    hi_s[...] = hi

    @pl.when(r == pl.num_programs(1) - 1)
    def _():
        hr_ref[...] = hr
        hi_ref[...] = hi

    ys = []
    for g in range(gpb):
        ys.append(y_s[g] + _dot(pr_s[g].astype(BF16), wor_ref[g]) + _dot(pi_s[g].astype(BF16), woi_ref[g]))
    d = d_ref[...]
    for h in range(n_half):
        y_half = [y[:, h * LANES:(h + 1) * LANES] for y in ys]
        for tl in range(8):
            t = h * 8 + tl
            y_t = _rot_blocks(_merge_blocks(y_half, tl), -tl)
            a_ref[t] = jax.nn.gelu(y_t + d * u_ref[0, t]).astype(BF16)


def _s5_scan_fused(uz4, ops, h0r, h0i, d_skip, *, layer, bsz, rows):
    toep, wsr, wsi, wor, woi, at = ops
    _, n_t, n, width = uz4.shape
    g = width // GROUP_CH
    tc = toep.shape[1]
    p = STATE_DIM
    gpb = LANES // GROUP_CH
    nblk = g // gpb
    assert n_t % 8 == 0 and tc == n_t * GROUP_CH and n % rows == 0 and rows % bsz == 0

    def gspec(a, b):
        return pl.BlockSpec((gpb, a, b), lambda i, r: (i, 0, 0))

    def ospec(a, b):
        return pl.BlockSpec((gpb, a, b), lambda i, r: (layer * nblk + i, 0, 0))

    return pl.pallas_call(
        functools.partial(_s5_scan_fused_kernel, bsz=bsz),
        grid=(nblk, n // rows),
        in_specs=[
            pl.BlockSpec((1, n_t, rows, LANES), lambda i, r: (0, 0, r, i)),
            ospec(tc, tc), ospec(tc, p), ospec(tc, p), ospec(p, tc), ospec(p, tc), ospec(2, p),
            gspec(bsz, p), gspec(bsz, p),
            pl.BlockSpec((1, LANES), lambda i, r: (0, i)),
        ],
        out_specs=[
            pl.BlockSpec((n_t, rows, LANES), lambda i, r: (0, r, i)),
            gspec(bsz, p), gspec(bsz, p),
        ],
        out_shape=[
            jax.ShapeDtypeStruct((n_t, n, width), BF16),
            jax.ShapeDtypeStruct((g, bsz, p), F32),
            jax.ShapeDtypeStruct((g, bsz, p), F32),
        ],
        scratch_shapes=[pltpu.VMEM((gpb, bsz, p), F32)] * 2 + [pltpu.VMEM((gpb, rows, p), F32)] * 4
        + [pltpu.VMEM((gpb, rows, tc), F32)],
        compiler_params=_params("parallel", "arbitrary"),
        name=f"s5_scan_fused_t{n_t}",
    )(uz4, toep, wsr, wsi, wor, woi, at, h0r, h0i, d_skip)


def _s5_glu_out_kernel(a_ref, z_ref, x_ref, wa_ref, wb_ref, ba_ref, bb_ref, wo_ref, o_ref, acc_s):
    j = pl.program_id(1)

    @pl.when(j == 0)
    def _():
        acc_s[...] = jnp.zeros(acc_s.shape, F32)

    rs = a_ref.shape[0] // GLU_ROW_SPLIT
    for h in range(GLU_ROW_SPLIT):
        rows = slice(h * rs, (h + 1) * rs)
        a = a_ref[rows, :]
        ga = _dot(a, wa_ref[...]) + ba_ref[...]
        gb = _dot(a, wb_ref[...]) + bb_ref[...]
        z = z_ref[0, rows, :]
        v = (ga * jax.nn.sigmoid(gb) * (z * jax.nn.sigmoid(z))).astype(BF16)
        acc_s[rows, :] += _dot(v, wo_ref[...])

    @pl.when(j == pl.num_programs(1) - 1)
    def _():
        o_ref[...] = x_ref[...] + acc_s[...]


def _s5_glu_out(a, uz, x, w_glu, b_glu, w_out, *, tm, tn, name):
    m, w = a.shape
    d = x.shape[1]
    tm = min(tm, m)
    nj = w // tn
    return pl.pallas_call(
        _s5_glu_out_kernel,
        grid=(m // tm, nj),
        in_specs=[
            pl.BlockSpec((tm, w), lambda i, j: (i, 0)),
            pl.BlockSpec((1, tm, tn), lambda i, j: (1, i, j)),
            pl.BlockSpec((tm, d), lambda i, j: (i, 0)),
            pl.BlockSpec((w, tn), lambda i, j: (0, j)),
            pl.BlockSpec((w, tn), lambda i, j: (0, nj + j)),
            pl.BlockSpec((1, tn), lambda i, j: (0, j)),
            pl.BlockSpec((1, tn), lambda i, j: (0, nj + j)),
            pl.BlockSpec((tn, d), lambda i, j: (j, 0)),
        ],
        out_specs=pl.BlockSpec((tm, d), lambda i, j: (i, 0)),
        out_shape=jax.ShapeDtypeStruct((m, d), F32),
        scratch_shapes=[pltpu.VMEM((tm, d), F32)],
        compiler_params=_params("parallel", "arbitrary"),
        name=name,
    )(a, uz, x, w_glu, w_glu, b_glu, b_glu, w_out)


def _latent_kernel(x_ref, g_ref, wl_ref, wr_ref, ws_ref, gl_ref, cc_ref, ss_ref, lat_ref, kr_ref, kcat_ref):
    h = _rms(x_ref[...], g_ref[...]).astype(BF16)
    lat = _rms(_dot(h, wl_ref[...]), gl_ref[...])
    kr = _dot(h, wr_ref[...]) * cc_ref[...] + _dot(h, ws_ref[...]) * ss_ref[...]
    lat_ref[...] = lat
    kr_ref[...] = kr[:, :QK_ROPE]
    kcat_ref[:, :KV_LORA] = lat.astype(BF16)
    kcat_ref[:, KV_LORA:] = kr.astype(BF16)


def _latent(x, g, wl, wr, ws, gl, cc, ss, *, tm, name):
    m, d = x.shape
    tm = min(tm, m)
    nt = cc.shape[0] // tm

    def full(a):
        return pl.BlockSpec(a.shape, lambda i: (0,) * a.ndim)

    return pl.pallas_call(
        _latent_kernel,
        grid=(m // tm,),
        in_specs=[
            pl.BlockSpec((tm, d), lambda i: (i, 0)),
            full(g), full(wl), full(wr), full(ws), full(gl),
            pl.BlockSpec((tm, LANES), lambda i: (i % nt, 0)),
            pl.BlockSpec((tm, LANES), lambda i: (i % nt, 0)),
        ],
        out_specs=[
            pl.BlockSpec((tm, KV_LORA), lambda i: (i, 0)),
            pl.BlockSpec((tm, QK_ROPE), lambda i: (i, 0)),
            pl.BlockSpec((tm, QK_PAD), lambda i: (i, 0)),
        ],
        out_shape=[
            jax.ShapeDtypeStruct((m, KV_LORA), F32),
            jax.ShapeDtypeStruct((m, QK_ROPE), F32),
            jax.ShapeDtypeStruct((m, QK_PAD), BF16),
        ],
        compiler_params=_params("parallel"),
        name=name,
    )(x, g, wl, wr, ws, gl, cc, ss)


def _mla_front_kernel(x_ref, g_ref, wcq_ref, wg_ref, gq_ref, wqn_ref, wqr_ref, wqs_ref, wuk_ref,
                      cc_ref, ss_ref, q_ref, sg_ref):
    h = _rms(x_ref[...], g_ref[...]).astype(BF16)
    cq = _rms(_dot(h, wcq_ref[...]), gq_ref[...]).astype(BF16)
    gate = _dot(h, wg_ref[...])
    sg_ref[...] = gate * jax.nn.sigmoid(gate)
    qn = _dot(cq, wqn_ref[...]).astype(BF16)
    qr = _dot(cq, wqr_ref[...])
    qs = _dot(cq, wqs_ref[...])
    cc = cc_ref[...]
    ss = ss_ref[...]
    for hd in range(N_HEADS):
        sl = slice(hd * LANES, (hd + 1) * LANES)
        q_ref[0, hd, :, :KV_LORA] = _dot(qn[:, sl], wuk_ref[hd]).astype(BF16)
        q_ref[0, hd, :, KV_LORA:] = (qr[:, sl] * cc + qs[:, sl] * ss).astype(BF16)


def _mla_front(x, g, wcq, wg, gq, wqn, wqr, wqs, wuk, cc, ss, *, tm, name):
    m, d = x.shape
    tm = min(tm, m)
    nt = cc.shape[0] // tm

    def full(a):
        return pl.BlockSpec(a.shape, lambda i: (0,) * a.ndim)

    return pl.pallas_call(
        _mla_front_kernel,
        grid=(m // tm,),
        in_specs=[
            pl.BlockSpec((tm, d), lambda i: (i, 0)),
            full(g), full(wcq), full(wg), full(gq), full(wqn), full(wqr), full(wqs), full(wuk),
            pl.BlockSpec((tm, LANES), lambda i: (i % nt, 0)),
            pl.BlockSpec((tm, LANES), lambda i: (i % nt, 0)),
        ],
        out_specs=[
            pl.BlockSpec((1, N_HEADS, tm, QK_PAD), lambda i: (i, 0, 0, 0)),
            pl.BlockSpec((tm, N_HEADS * V_HEAD), lambda i: (i, 0)),
        ],
        out_shape=[
            jax.ShapeDtypeStruct((m // tm, N_HEADS, tm, QK_PAD), BF16),
            jax.ShapeDtypeStruct((m, N_HEADS * V_HEAD), F32),
        ],
        compiler_params=_params("parallel"),
        name=name,
    )(x, g, wcq, wg, gq, wqn, wqr, wqs, wuk, cc, ss)


def _lane_tile(x, width):
    return x if width == LANES else jnp.concatenate([x] * (width // LANES), axis=1)


def _softmax_update(s, kl, m_ref, l_ref, acc_ref, rows):
    m_prev = m_ref[rows, :]
    m_new = jnp.maximum(m_prev, jnp.max(s, axis=1, keepdims=True))
    alpha = jnp.exp2(m_prev - m_new)
    p = jnp.exp2(s - _lane_tile(m_new, s.shape[1]))
    l_ref[rows, :] = alpha * l_ref[rows, :] + jnp.sum(p, axis=1, keepdims=True)
    acc_ref[rows, :] = _lane_tile(alpha, kl.shape[1]) * acc_ref[rows, :] + _dot(p.astype(BF16), kl)
    m_ref[rows, :] = m_new


def _attn_prompt_kernel(qi_ref, ki_ref, q_ref, k_ref, o_ref, m_s, l_s, acc_s, *, tq, tk, heads_per_chunk):
    step = pl.program_id(1)
    qi = qi_ref[step]
    ki = ki_ref[step]
    diag = (qi * tq + (tq - 1)) // tk
    rc = heads_per_chunk * tq

    @pl.when(ki == 0)
    def _():
        m_s[...] = jnp.full(m_s.shape, NEG_INF, F32)
        l_s[...] = jnp.zeros(l_s.shape, F32)
        acc_s[...] = jnp.zeros(acc_s.shape, F32)

    def update(masked):
        k = k_ref[...]
        kl = k[:, :KV_LORA]
        for c in range(N_HEADS // heads_per_chunk):
            q = q_ref[0, c * heads_per_chunk:(c + 1) * heads_per_chunk].reshape(rc, QK_PAD)
            s = _dot_nt(q, k) * SCALE_LOG2E
            if masked:
                tok = qi * tq + (lax.broadcasted_iota(jnp.int32, (rc, tk), 0) & (tq - 1))
                col = ki * tk + lax.broadcasted_iota(jnp.int32, (rc, tk), 1)
                s = jnp.where(col <= tok, s, NEG_INF)
            _softmax_update(s, kl, m_s, l_s, acc_s, pl.ds(c * rc, rc))

    @pl.when(ki < diag)
    def _():
        update(False)

    @pl.when(ki == diag)
    def _():
        update(True)
        for hd in range(N_HEADS):
            rows = pl.ds(hd * tq, tq)
            o_ref[0, hd] = (acc_s[rows, :] / _lane_tile(l_s[rows, :], KV_LORA)).astype(BF16)


def _attn_prompt(q, kcat, *, bsz, seq, tq, tk):
    assert tq & (tq - 1) == 0 and seq % tq == 0 and seq % tk == 0
    nq = seq // tq
    nk = seq // tk
    rows = N_HEADS * tq
    pairs = [(i, j) for i in range(nq) for j in range((i * tq + tq - 1) // tk + 1)]
    qi_arr = jnp.asarray([p[0] for p in pairs], jnp.int32)
    ki_arr = jnp.asarray([p[1] for p in pairs], jnp.int32)
    grid_spec = pltpu.PrefetchScalarGridSpec(
        num_scalar_prefetch=2,
        grid=(bsz, len(pairs)),
        in_specs=[
            pl.BlockSpec((1, N_HEADS, tq, QK_PAD), lambda b, s, qi, ki: (b * nq + qi[s], 0, 0, 0)),
            pl.BlockSpec((tk, QK_PAD), lambda b, s, qi, ki: (b * nk + ki[s], 0)),
        ],
        out_specs=pl.BlockSpec((1, N_HEADS, tq, KV_LORA), lambda b, s, qi, ki: (b * nq + qi[s], 0, 0, 0)),
        scratch_shapes=[
            pltpu.VMEM((rows, LANES), F32),
            pltpu.VMEM((rows, LANES), F32),
            pltpu.VMEM((rows, KV_LORA), F32),
        ],
    )
    return pl.pallas_call(
        functools.partial(_attn_prompt_kernel, tq=tq, tk=tk, heads_per_chunk=ATTN_HEADS_PER_CHUNK),
        grid_spec=grid_spec,
        out_shape=jax.ShapeDtypeStruct((bsz * nq, N_HEADS, tq, KV_LORA), BF16),
        compiler_params=_params("parallel", "arbitrary"),
        name="attn_prompt",
    )(qi_arr, ki_arr, q, kcat)


def _attn_sample_kernel(pt_ref, q_ref, nl_ref, nk_ref, lat_hbm, krt_hbm, o_ref,
                        lat_buf, krt_buf, sems, kl_s, kr_s, m_s, l_s, acc_s, *, n_pg, dec_seq):
    step = pl.program_id(1)
    steps = pl.num_programs(1)
    gstep = pl.program_id(0) * steps + step
    total = pl.num_programs(0) * steps
    slot = gstep & 1

    def page_copies(at, buf_slot):
        out = []
        for i in range(n_pg):
            page = pt_ref[at * n_pg + i]
            out.append(pltpu.make_async_copy(lat_hbm.at[page], lat_buf.at[buf_slot, i], sems.at[0, buf_slot]))
            out.append(pltpu.make_async_copy(krt_hbm.at[page], krt_buf.at[buf_slot, i], sems.at[1, buf_slot]))
        return out

    @pl.when(gstep == 0)
    def _():
        for c in page_copies(0, 0):
            c.start()

    @pl.when(gstep + 1 < total)
    def _():
        for c in page_copies(gstep + 1, 1 - slot):
            c.start()

    @pl.when(step == 0)
    def _():
        m_s[...] = jnp.full(m_s.shape, NEG_INF, F32)
        l_s[...] = jnp.zeros(l_s.shape, F32)
        acc_s[...] = jnp.zeros(acc_s.shape, F32)

    for c in page_copies(gstep, slot):
        c.wait()

    for i in range(n_pg):
        kl_s[i * PAGE_SIZE:(i + 1) * PAGE_SIZE, :] = lat_buf[slot, i].astype(BF16)
        kr_s[:, i * PAGE_SIZE:(i + 1) * PAGE_SIZE] = krt_buf[slot, i].astype(BF16)

    q = q_ref[0]
    ql = q[:, :KV_LORA]
    qr = q[:, KV_LORA:KV_LORA + QK_ROPE]
    kl = kl_s[...]
    s = (_dot_nt(ql, kl) + _dot(qr, kr_s[...])) * SCALE_LOG2E
    _softmax_update(s, kl, m_s, l_s, acc_s, slice(None))

    @pl.when(step == pl.num_programs(1) - 1)
    def _():
        nl = nl_ref[0].astype(BF16).astype(F32)
        nk = nk_ref[0].astype(BF16).astype(F32)
        qlf = ql.astype(F32)
        qrf = qr.astype(F32)
        tok = lax.broadcasted_iota(jnp.int32, (q.shape[0], 1), 0) & (dec_seq - 1)
        m = m_s[...]
        l = l_s[...]
        acc = acc_s[...]
        for j in range(dec_seq):
            sj = (jnp.sum(qlf * nl[j:j + 1], axis=-1, keepdims=True)
                  + jnp.sum(qrf * nk[j:j + 1], axis=-1, keepdims=True)) * SCALE_LOG2E
            sj = jnp.where(tok >= j, sj, NEG_INF)
            m_new = jnp.maximum(m, sj)
            alpha = jnp.exp2(m - m_new)
            pj = jnp.exp2(sj - m_new)
            l = alpha * l + pj
            pv = _lane_tile(pj, KV_LORA).astype(BF16).astype(F32) * nl[j:j + 1]
            acc = _lane_tile(alpha, KV_LORA) * acc + pv
            m = m_new
        o_ref[0] = (acc / _lane_tile(l, KV_LORA)).astype(BF16)


def _attn_sample(q, cache_latent, cache_krope_t, page_table, new_lat, new_kr, *, n_pg):
    n_seq, n_pages = page_table.shape
    rows = q.shape[1]
    dec_seq = new_lat.shape[1]
    assert dec_seq & (dec_seq - 1) == 0 and n_pages % n_pg == 0
    steps = n_pages // n_pg
    pt = page_table.reshape(-1)

    def seq_spec(a, b):
        return pl.BlockSpec((1, a, b), lambda n, s, pt_ref: (n, 0, 0))

    grid_spec = pltpu.PrefetchScalarGridSpec(
        num_scalar_prefetch=1,
        grid=(n_seq, steps),
        in_specs=[seq_spec(rows, QK_PAD), seq_spec(dec_seq, KV_LORA), seq_spec(dec_seq, QK_ROPE),
                  pl.BlockSpec(memory_space=pl.ANY), pl.BlockSpec(memory_space=pl.ANY)],
        out_specs=seq_spec(rows, KV_LORA),
        scratch_shapes=[
            pltpu.VMEM((2, n_pg, PAGE_SIZE, KV_LORA), F32),
            pltpu.VMEM((2, n_pg, QK_ROPE, PAGE_SIZE), F32),
            pltpu.SemaphoreType.DMA((2, 2)),
            pltpu.VMEM((n_pg * PAGE_SIZE, KV_LORA), BF16),
            pltpu.VMEM((QK_ROPE, n_pg * PAGE_SIZE), BF16),
            pltpu.VMEM((rows, LANES), F32),
            pltpu.VMEM((rows, LANES), F32),
            pltpu.VMEM((rows, KV_LORA), F32),
        ],
    )
    return pl.pallas_call(
        functools.partial(_attn_sample_kernel, n_pg=n_pg, dec_seq=dec_seq),
        grid_spec=grid_spec,
        out_shape=jax.ShapeDtypeStruct((n_seq, rows, KV_LORA), BF16),
        compiler_params=_params("arbitrary", "arbitrary"),
        name="attn_sample",
    )(pt, q, new_lat, new_kr, cache_latent, cache_krope_t)


def _mla_back_kernel(o_ref, sg_ref, x_ref, wuv_ref, wo_ref, gf_ref, out_ref, *, final_norm):
    parts = []
    for hd in range(N_HEADS):
        oh = _dot(o_ref[0, hd], wuv_ref[hd])
        parts.append((oh * sg_ref[:, hd * V_HEAD:(hd + 1) * V_HEAD]).astype(BF16))
    y = x_ref[...] + _dot(jnp.concatenate(parts, axis=1), wo_ref[...])
    out_ref[...] = _rms(y, gf_ref[...]) if final_norm else y


def _mla_back(o, sg, x, wuv, wo, gf, *, final_norm, name):
    m, d = x.shape
    tm = o.shape[2]

    def full(a):
        return pl.BlockSpec(a.shape, lambda i: (0,) * a.ndim)

    def rows(a):
        return pl.BlockSpec((tm, a.shape[1]), lambda i: (i, 0))

    return pl.pallas_call(
        functools.partial(_mla_back_kernel, final_norm=final_norm),
        grid=(m // tm,),
        in_specs=[pl.BlockSpec((1,) + o.shape[1:], lambda i: (i, 0, 0, 0)), rows(sg), rows(x), full(wuv), full(wo),
                  full(gf)],
        out_specs=pl.BlockSpec((tm, d), lambda i: (i, 0)),
        out_shape=jax.ShapeDtypeStruct((m, d), F32),
        compiler_params=_params("parallel"),
        name=name,
    )(o, sg, x, wuv, wo, gf)


def _rope_tables(pos):
    half = QK_ROPE // 2
    inv = ROPE_THETA ** (-jnp.arange(half, dtype=F32) / half)
    ang = pos.astype(F32)[:, None] * inv[None, :]
    cos = jnp.cos(ang)
    sin = jnp.sin(ang)
    pad = jnp.zeros((pos.shape[0], LANES - QK_ROPE), F32)
    cc = jnp.concatenate([cos, cos, pad], axis=1)
    ss = jnp.concatenate([-sin, sin, pad], axis=1)
    return cc, ss


def _swap_halves(w):
    half = w.shape[-1] // 2
    return jnp.concatenate([w[..., half:], w[..., :half]], axis=-1)


def _pad_lanes(w):
    pad = [(0, 0)] * (w.ndim - 1) + [(0, LANES - w.shape[-1])]
    return jnp.pad(w, pad)


def kernel(x_prompt, x_sample, cache_latent, cache_krope, page_table, state_ssm_re, state_ssm_im, norm_a, w_in_a, a_re, a_im, log_dt, b_re, b_im, c_re, c_im, d_skip, w_glu, b_glu, w_out_a, norm_kv, w_dkv, norm_latent, w_uk, w_uv, norm_b, w_in_b, norm_q, w_uq, w_out_b, norm_f):
    bsz, seq, d = x_prompt.shape
    dbs, dseq, _ = x_sample.shape
    n_a = norm_a.shape[0]
    n_b = norm_b.shape[0]
    width = d_skip.shape[1]
    n_groups = width // GROUP_CH
    past_len = page_table.shape[1] * PAGE_SIZE

    n_chunks = seq // PROMPT_CHUNK
    xp = x_prompt.reshape(bsz, n_chunks, PROMPT_CHUNK, d).transpose(2, 1, 0, 3).reshape(bsz * seq, d)
    assert dseq <= SAMPLE_CHUNK
    lead = SAMPLE_CHUNK - dseq
    xs = x_sample.transpose(1, 0, 2).reshape(dseq * dbs, d)

    hp_re, hp_im, hs_re, hs_im = [], [], [], []
    zeros_p = jnp.zeros((n_groups, bsz, STATE_DIM), F32)
    ops_p = _s5_prep(a_re, a_im, log_dt, b_re, b_im, c_re, c_im, chunk=PROMPT_CHUNK, offset=0)
    ops_s = _s5_prep(a_re, a_im, log_dt, b_re, b_im, c_re, c_im, chunk=SAMPLE_CHUNK, offset=lead)
    for i in range(n_a):
        w_in = w_in_a[i].astype(BF16)
        w_gl = w_glu[i].astype(BF16)
        b_gl = b_glu[i][None, :]
        w_out = w_out_a[i].astype(BF16)
        g = norm_a[i][None, :]
        dsk = d_skip[i][None, :]

        uz = _norm_matmul(xp, g, w_in, parts=2, tm=512, tn=1024, name="s5_in_prompt")
        uz4 = uz.reshape(2, PROMPT_CHUNK, n_chunks * bsz, width)
        a, hr, hi = _s5_scan_fused(uz4, ops_p, zeros_p, zeros_p, dsk, layer=i, bsz=bsz, rows=SCAN_ROWS)
        xp = _s5_glu_out(a.reshape(bsz * seq, width), uz, xp, w_gl, b_gl, w_out, tm=1024, tn=512,
                         name="s5_glu_out_prompt")
        hp_re.append(hr.transpose(1, 0, 2))
        hp_im.append(hi.transpose(1, 0, 2))

        uz = _norm_matmul(xs, g, w_in, parts=2, tm=512, tn=1024, name="s5_in_sample")
        uz4 = jnp.pad(uz.reshape(2, dseq, dbs, width), ((0, 0), (lead, 0), (0, 0), (0, 0)))
        a, hr, hi = _s5_scan_fused(uz4, ops_s, state_ssm_re[i].transpose(1, 0, 2),
                                   state_ssm_im[i].transpose(1, 0, 2), dsk, layer=i, bsz=dbs, rows=dbs)
        xs = _s5_glu_out(a[lead:].reshape(dseq * dbs, width), uz, xs, w_gl, b_gl, w_out, tm=512, tn=512,
                         name="s5_glu_out_sample")
        hs_re.append(hr.transpose(1, 0, 2))
        hs_im.append(hi.transpose(1, 0, 2))

    xp = xp.reshape(PROMPT_CHUNK, n_chunks, bsz, d).transpose(2, 1, 0, 3).reshape(bsz * seq, d)
    xs = xs.reshape(dseq, dbs, d).transpose(1, 0, 2).reshape(dbs * dseq, d)

    cc_p, ss_p = _rope_tables(jnp.arange(seq, dtype=jnp.int32))
    pos_s = past_len + jnp.arange(dseq, dtype=jnp.int32)
    cc_s, ss_s = _rope_tables(jnp.tile(pos_s, dbs))
    w_lat = w_dkv[:, :KV_LORA].astype(BF16)
    w_kr = _pad_lanes(w_dkv[:, KV_LORA:]).astype(BF16)
    w_ks = _pad_lanes(_swap_halves(w_dkv[:, KV_LORA:])).astype(BF16)
    g_kv = norm_kv[None, :]
    g_lat = norm_latent[None, :]
    lat_p, kr_p, kcat_p = _latent(xp, g_kv, w_lat, w_kr, w_ks, g_lat, cc_p, ss_p, tm=512, name="latent_prompt")
    lat_s, kr_s, _ = _latent(xs, g_kv, w_lat, w_kr, w_ks, g_lat, cc_s, ss_s, tm=512, name="latent_sample")
    new_lat = lat_s.reshape(dbs, dseq, KV_LORA)
    new_kr = kr_s.reshape(dbs, dseq, QK_ROPE)

    cache_krope_t = jnp.swapaxes(cache_krope, 1, 2)
    w_ukt = w_uk.astype(BF16).reshape(KV_LORA, N_HEADS, QK_NOPE).transpose(1, 2, 0)
    w_uvh = w_uv.astype(BF16).reshape(KV_LORA, N_HEADS, V_HEAD).transpose(1, 0, 2)
    g_f = norm_f[None, :]
    assert n_b > 0
    for j in range(n_b):
        g = norm_b[j][None, :]
        w_cq = w_in_b[j][:, :Q_LORA].astype(BF16)
        w_gate = w_in_b[j][:, Q_LORA:].astype(BF16)
        g_q = norm_q[j][None, :]
        wq3 = w_uq[j].reshape(Q_LORA, N_HEADS, QK_NOPE + QK_ROPE)
        w_qn = wq3[:, :, :QK_NOPE].reshape(Q_LORA, N_HEADS * QK_NOPE).astype(BF16)
        w_qr = _pad_lanes(wq3[:, :, QK_NOPE:]).reshape(Q_LORA, N_HEADS * LANES).astype(BF16)
        w_qs = _pad_lanes(_swap_halves(wq3[:, :, QK_NOPE:])).reshape(Q_LORA, N_HEADS * LANES).astype(BF16)
        w_o = w_out_b[j].astype(BF16)

        q, sg = _mla_front(xp, g, w_cq, w_gate, g_q, w_qn, w_qr, w_qs, w_ukt, cc_p, ss_p, tm=ATTN_TQ,
                           name="mla_front_prompt")
        o = _attn_prompt(q, kcat_p, bsz=bsz, seq=seq, tq=ATTN_TQ, tk=ATTN_TK)
        last = j == n_b - 1
        xp = _mla_back(o, sg, xp, w_uvh, w_o, g_f, final_norm=last, name="mla_back_prompt")

        q, sg = _mla_front(xs, g, w_cq, w_gate, g_q, w_qn, w_qr, w_qs, w_ukt, cc_s, ss_s, tm=ATTN_TQ,
                           name="mla_front_sample")
        spt = ATTN_TQ // dseq
        q_seq = q.reshape(-1, N_HEADS, spt, dseq, QK_PAD).transpose(0, 2, 1, 3, 4)
        o = _attn_sample(q_seq.reshape(dbs, N_HEADS * dseq, QK_PAD), cache_latent, cache_krope_t, page_table,
                         new_lat, new_kr, n_pg=SAMPLE_PAGES_PER_STEP)
        o = o.reshape(-1, spt, N_HEADS, dseq, KV_LORA).transpose(0, 2, 1, 3, 4)
        xs = _mla_back(o.reshape(-1, N_HEADS, ATTN_TQ, KV_LORA), sg, xs, w_uvh, w_o, g_f, final_norm=last,
                       name="mla_back_sample")

    return (xp.reshape(bsz, seq, d), xs.reshape(dbs, dseq, d),
            lat_p.reshape(bsz, seq, KV_LORA), kr_p.reshape(bsz, seq, QK_ROPE),
            new_lat, new_kr,
            jnp.stack(hp_re), jnp.stack(hp_im), jnp.stack(hs_re), jnp.stack(hs_im))
```

```python
import functools
import math

import jax
import jax.numpy as jnp
from jax import lax
from jax.experimental import pallas as pl
from jax.experimental.pallas import tpu as pltpu

F32 = jnp.float32
BF16 = jnp.bfloat16

GROUP_CH = 16
STATE_DIM = 64
N_HEADS = 8
QK_NOPE = 128
QK_ROPE = 64
V_HEAD = 128
KV_LORA = 256
Q_LORA = 384
PAGE_SIZE = 128
ROPE_THETA = 10000.0
RMS_EPS = 1e-6
SOFTMAX_SCALE = 1.0 / math.sqrt(QK_NOPE + QK_ROPE)
SCALE_LOG2E = SOFTMAX_SCALE * math.log2(math.e)
NEG_INF = -1e30
LANES = 128
QK_PAD = KV_LORA + LANES
PROMPT_CHUNK = 16
SAMPLE_CHUNK = 8
ATTN_TQ = 512
ATTN_TK = 512
ATTN_HEADS_PER_CHUNK = 2
SAMPLE_PAGES_PER_STEP = 64
SCAN_ROWS = 512
GLU_ROW_SPLIT = 2
VMEM_LIMIT = 56 * 1024 * 1024


def _dot(a, b):
    return jnp.dot(a, b, preferred_element_type=F32)


def _dot_nt(a, b):
    return lax.dot_general(a, b, (((1,), (1,)), ((), ())), preferred_element_type=F32)


def _dot_f32(a, b):
    return jnp.dot(a, b, preferred_element_type=F32, precision=lax.Precision.HIGHEST)


def _rms(x, g):
    return x * lax.rsqrt(jnp.mean(x * x, axis=-1, keepdims=True) + RMS_EPS) * g


def _params(*sem):
    return pltpu.CompilerParams(dimension_semantics=sem, vmem_limit_bytes=VMEM_LIMIT)


def _norm_matmul_kernel(x_ref, g_ref, w_ref, o_ref, *, tn):
    h = _rms(x_ref[...], g_ref[...]).astype(BF16)
    parts, _, n = o_ref.shape
    for p in range(parts):
        for j in range(n // tn):
            o_ref[p, :, j * tn:(j + 1) * tn] = _dot(h, w_ref[:, p * n + j * tn:p * n + (j + 1) * tn])


def _norm_matmul(x, g, w, *, parts, tm, tn, name):
    m, d = x.shape
    n = w.shape[1] // parts
    tm = min(tm, m)
    return pl.pallas_call(
        functools.partial(_norm_matmul_kernel, tn=tn),
        grid=(m // tm,),
        in_specs=[
            pl.BlockSpec((tm, d), lambda i: (i, 0)),
            pl.BlockSpec((1, d), lambda i: (0, 0)),
            pl.BlockSpec(w.shape, lambda i: (0, 0), pipeline_mode=pl.Buffered(1)),
        ],
        out_specs=pl.BlockSpec((parts, tm, n), lambda i: (0, i, 0)),
        out_shape=jax.ShapeDtypeStruct((parts, m, n), F32),
        compiler_params=_params("parallel"),
        name=name,
    )(x, g, w)


def _discretize(lam_re, lam_im, log_dt):
    dt = jnp.exp(log_dt)
    mag = jnp.exp(lam_re * dt)
    lb_re = mag * jnp.cos(lam_im * dt)
    lb_im = mag * jnp.sin(lam_im * dt)
    den = lam_re * lam_re + lam_im * lam_im
    nr = lb_re - 1.0
    f_re = (nr * lam_re + lb_im * lam_im) / den
    f_im = (lb_im * lam_re - nr * lam_im) / den
    return lb_re, lb_im, f_re, f_im


def _cpow(br, bi, e, nbits, shape):
    br = jnp.broadcast_to(br, shape)
    bi = jnp.broadcast_to(bi, shape)
    rr = jnp.ones(shape, F32)
    ri = jnp.zeros(shape, F32)
    for j in range(nbits):
        bit = ((e >> j) & 1) == 1
        nr = rr * br - ri * bi
        ni = rr * bi + ri * br
        rr = jnp.where(bit, nr, rr)
        ri = jnp.where(bit, ni, ri)
        if j + 1 < nbits:
            br, bi = br * br - bi * bi, 2.0 * br * bi
    return rr, ri


def _s5_prep_kernel(pcol_ref, prow_ref, btr_ref, bti_ref, ctr_ref, cti_ref,
                    toep_ref, wsr_ref, wsi_ref, wor_ref, woi_ref, at_ref, *, chunk, offset):
    gpb, p, tl = ctr_ref.shape
    nbits = chunk.bit_length()
    shift = GROUP_CH.bit_length() - 1
    per_half = LANES // GROUP_CH

    pc = pcol_ref[...].reshape(gpb * p, 3)
    lbr_c, lbi_c, _, _ = _discretize(pc[:, 0:1], pc[:, 1:2], pc[:, 2:3])
    cr = ctr_ref[...].reshape(gpb * p, tl)
    ci = cti_ref[...].reshape(gpb * p, tl)
    shape = cr.shape
    lane_t = lax.broadcasted_iota(jnp.int32, shape, 1) >> shift
    p0r, p0i = _cpow(lbr_c, lbi_c, lane_t, nbits, shape)
    if offset == 0:
        p1r = p0r * lbr_c - p0i * lbi_c
        p1i = p0r * lbi_c + p0i * lbr_c
    else:
        p1r, p1i = _cpow(lbr_c, lbi_c, jnp.maximum(lane_t + (1 - offset), 0), nbits, shape)
    rr = cr * p0r - ci * p0i
    ri = cr * p0i + ci * p0r
    wor = cr * p1r - ci * p1i
    woi = -(cr * p1i + ci * p1r)

    pr = prow_ref[...]
    lbr_r, lbi_r, f_re, f_im = _discretize(pr[:, 0:1, :], pr[:, 1:2, :], pr[:, 2:3, :])
    btr = btr_ref[...]
    bti = bti_ref[...]
    bbr = f_re * btr - f_im * bti
    bbi = f_re * bti + f_im * btr
    pwr, pwi = [jnp.ones_like(lbr_r)], [jnp.zeros_like(lbr_r)]
    for _ in range(chunk):
        pwr, pwi = (pwr + [pwr[-1] * lbr_r - pwi[-1] * lbi_r], pwi + [pwr[-1] * lbi_r + pwi[-1] * lbr_r])
    wsr = [(bbr * pwr[chunk - 1 - s] - bbi * pwi[chunk - 1 - s]).astype(BF16) for s in range(chunk)]
    wsi = [(bbr * pwi[chunk - 1 - s] + bbi * pwr[chunk - 1 - s]).astype(BF16) for s in range(chunk)]
    at_ref[:, 0:1, :] = pwr[chunk - offset]
    at_ref[:, 1:2, :] = pwi[chunk - offset]

    lane = lax.broadcasted_iota(jnp.int32, (GROUP_CH, tl), 1)
    for g in range(gpb):
        rows = slice(g * p, (g + 1) * p)

        def rot_cols(x, g=g):
            return jnp.concatenate([_rot_blocks(x[:, h * LANES:(h + 1) * LANES], g) for h in range(tl // LANES)],
                                   axis=1)

        def dest_row(s, g=g):
            return LANES * (s // per_half) + GROUP_CH * ((s + g) % per_half)

        wor_ref[g] = rot_cols(wor[rows]).astype(BF16)
        woi_ref[g] = rot_cols(woi[rows]).astype(BF16)

        krow = _dot_f32(bbr[g], rr[rows]) - _dot_f32(bbi[g], ri[rows])
        for s in range(chunk):
            if s == 0:
                blk = krow
            else:
                blk = jnp.where(lane >= GROUP_CH * s, pltpu.roll(krow, GROUP_CH * s, 1), 0.0)
            dst = slice(dest_row(s), dest_row(s) + GROUP_CH)
            toep_ref[g, dst, :] = rot_cols(blk).astype(BF16)
            wsr_ref[g, dst, :] = wsr[s][g]
            wsi_ref[g, dst, :] = wsi[s][g]


def _s5_prep(a_re, a_im, log_dt, b_re, b_im, c_re, c_im, *, chunk, offset):
    steps = chunk - offset
    assert steps > 0 and steps & (steps - 1) == 0 and chunk & (chunk - 1) == 0
    assert (chunk * GROUP_CH) % LANES == 0
    p = a_re.shape[-1]
    a_re, a_im = a_re.reshape(-1, p), a_im.reshape(-1, p)
    log_dt = log_dt.reshape(-1)
    b_re, b_im = b_re.reshape((-1,) + b_re.shape[-2:]), b_im.reshape((-1,) + b_im.shape[-2:])
    c_re, c_im = c_re.reshape((-1,) + c_re.shape[-2:]), c_im.reshape((-1,) + c_im.shape[-2:])
    g = a_re.shape[0]
    gpb = LANES // GROUP_CH
    tc = chunk * GROUP_CH
    ldt = jnp.broadcast_to(log_dt[:, None], (g, p))
    pcol = jnp.stack([a_re, a_im, ldt], axis=-1)
    prow = jnp.stack([a_re, a_im, ldt], axis=1)
    bt_re = jnp.swapaxes(b_re, 1, 2)
    bt_im = jnp.swapaxes(b_im, 1, 2)
    ct_re = jnp.tile(jnp.swapaxes(c_re, 1, 2), (1, 1, chunk))
    ct_im = jnp.tile(jnp.swapaxes(c_im, 1, 2), (1, 1, chunk))

    def spec(a, b):
        return pl.BlockSpec((gpb, a, b), lambda i: (i, 0, 0))

    return pl.pallas_call(
        functools.partial(_s5_prep_kernel, chunk=chunk, offset=offset),
        grid=(g // gpb,),
        in_specs=[spec(p, 3), spec(3, p), spec(GROUP_CH, p), spec(GROUP_CH, p), spec(p, tc), spec(p, tc)],
        out_specs=[spec(tc, tc), spec(tc, p), spec(tc, p), spec(p, tc), spec(p, tc), spec(2, p)],
        out_shape=[
            jax.ShapeDtypeStruct((g, tc, tc), BF16),
            jax.ShapeDtypeStruct((g, tc, p), BF16),
            jax.ShapeDtypeStruct((g, tc, p), BF16),
            jax.ShapeDtypeStruct((g, p, tc), BF16),
            jax.ShapeDtypeStruct((g, p, tc), BF16),
            jax.ShapeDtypeStruct((g, 2, p), F32),
        ],
        compiler_params=_params("parallel"),
        name=f"s5_prep_t{chunk}",
    )(pcol, prow, bt_re, bt_im, ct_re, ct_im)


def _rot_blocks(x, k):
    k %= LANES // GROUP_CH
    return x if k == 0 else pltpu.roll(x, k * GROUP_CH, 1)


def _merge_blocks(xs, offset):
    n = LANES // GROUP_CH
    blk = lax.broadcasted_iota(jnp.int32, xs[0].shape, 1) >> (GROUP_CH.bit_length() - 1)
    out = xs[(-offset) % n]
    for b in range(1, n):
        out = jnp.where(blk == b, xs[(b - offset) % n], out)
    return out


def _s5_scan_fused_kernel(u_ref, toep_ref, wsr_ref, wsi_ref, wor_ref, woi_ref, at_ref, h0r_ref, h0i_ref, d_ref,
                          a_ref, hr_ref, hi_ref, hr_s, hi_s, sr_s, si_s, pr_s, pi_s, y_s, *, bsz):
    r = pl.program_id(1)
    _, n_t, rows, _ = u_ref.shape
    gpb = toep_ref.shape[0]
    n_half = n_t // 8

    @pl.when(r == 0)
    def _():
        hr_s[...] = h0r_ref[...]
        hi_s[...] = h0i_ref[...]

    rot = [_rot_blocks(u_ref[0, t], t).astype(BF16) for t in range(n_t)]
    for g in range(gpb):
        u_g = jnp.concatenate([_merge_blocks(rot[h * 8:(h + 1) * 8], g) for h in range(n_half)], axis=1)
        y_s[g] = _dot(u_g, toep_ref[g])
        sr_s[g] = _dot(u_g, wsr_ref[g])
        si_s[g] = _dot(u_g, wsi_ref[g])

    ar = at_ref[:, 0:1, :]
    ai = at_ref[:, 1:2, :]

    def body(k, carry):
        hr, hi = carry
        rs = pl.ds(pl.multiple_of(k * bsz, bsz), bsz)
        pr_s[:, rs, :] = hr
        pi_s[:, rs, :] = hi
        return (ar * hr - ai * hi + sr_s[:, rs, :], ar * hi + ai * hr + si_s[:, rs, :])

    hr, hi = lax.fori_loop(0, rows // bsz, body, (hr_s[...], hi_s[...]))
    hr_s[...] = hr
    hi_s[...] = hi

    @pl.when(r == pl.num_programs(1) - 1)
    def _():
        hr_ref[...] = hr
        hi_ref[...] = hi

    ys = []
    for g in range(gpb):
        ys.append(y_s[g] + _dot(pr_s[g].astype(BF16), wor_ref[g]) + _dot(pi_s[g].astype(BF16), woi_ref[g]))
    d = d_ref[...]
    for h in range(n_half):
        y_half = [y[:, h * LANES:(h + 1) * LANES] for y in ys]
        for tl in range(8):
            t = h * 8 + tl
            y_t = _rot_blocks(_merge_blocks(y_half, tl), -tl)
            a_ref[t] = jax.nn.gelu(y_t + d * u_ref[0, t]).astype(BF16)


def _s5_scan_fused(uz4, ops, h0r, h0i, d_skip, *, layer, bsz, rows):
    toep, wsr, wsi, wor, woi, at = ops
    _, n_t, n, width = uz4.shape
    g = width // GROUP_CH
    tc = toep.shape[1]
    p = STATE_DIM
    gpb = LANES // GROUP_CH
    nblk = g // gpb
    assert n_t % 8 == 0 and tc == n_t * GROUP_CH and n % rows == 0 and rows % bsz == 0

    def gspec(a, b):
        return pl.BlockSpec((gpb, a, b), lambda i, r: (i, 0, 0))

    def ospec(a, b):
        return pl.BlockSpec((gpb, a, b), lambda i, r: (layer * nblk + i, 0, 0))

    return pl.pallas_call(
        functools.partial(_s5_scan_fused_kernel, bsz=bsz),
        grid=(nblk, n // rows),
        in_specs=[
            pl.BlockSpec((1, n_t, rows, LANES), lambda i, r: (0, 0, r, i)),
            ospec(tc, tc), ospec(tc, p), ospec(tc, p), ospec(p, tc), ospec(p, tc), ospec(2, p),
            gspec(bsz, p), gspec(bsz, p),
            pl.BlockSpec((1, LANES), lambda i, r: (0, i)),
        ],
        out_specs=[
            pl.BlockSpec((n_t, rows, LANES), lambda i, r: (0, r, i)),
            gspec(bsz, p), gspec(bsz, p),
        ],
        out_shape=[
            jax.ShapeDtypeStruct((n_t, n, width), BF16),
            jax.ShapeDtypeStruct((g, bsz, p), F32),
            jax.ShapeDtypeStruct((g, bsz, p), F32),
        ],
        scratch_shapes=[pltpu.VMEM((gpb, bsz, p), F32)] * 2 + [pltpu.VMEM((gpb, rows, p), F32)] * 4
        + [pltpu.VMEM((gpb, rows, tc), F32)],
        compiler_params=_params("parallel", "arbitrary"),
        name=f"s5_scan_fused_t{n_t}",
    )(uz4, toep, wsr, wsi, wor, woi, at, h0r, h0i, d_skip)


def _s5_glu_out_kernel(a_ref, z_ref, x_ref, wa_ref, wb_ref, ba_ref, bb_ref, wo_ref, o_ref, acc_s):
    j = pl.program_id(1)

    @pl.when(j == 0)
    def _():
        acc_s[...] = jnp.zeros(acc_s.shape, F32)

    rs = a_ref.shape[0] // GLU_ROW_SPLIT
    for h in range(GLU_ROW_SPLIT):
        rows = slice(h * rs, (h + 1) * rs)
        a = a_ref[rows, :]
        ga = _dot(a, wa_ref[...]) + ba_ref[...]
        gb = _dot(a, wb_ref[...]) + bb_ref[...]
        z = z_ref[0, rows, :]
        v = (ga * jax.nn.sigmoid(gb) * (z * jax.nn.sigmoid(z))).astype(BF16)
        acc_s[rows, :] += _dot(v, wo_ref[...])

    @pl.when(j == pl.num_programs(1) - 1)
    def _():
        o_ref[...] = x_ref[...] + acc_s[...]


def _s5_glu_out(a, uz, x, w_glu, b_glu, w_out, *, tm, tn, name):
    m, w = a.shape
    d = x.shape[1]
    tm = min(tm, m)
    nj = w // tn
    return pl.pallas_call(
        _s5_glu_out_kernel,
        grid=(m // tm, nj),
        in_specs=[
            pl.BlockSpec((tm, w), lambda i, j: (i, 0)),
            pl.BlockSpec((1, tm, tn), lambda i, j: (1, i, j)),
            pl.BlockSpec((tm, d), lambda i, j: (i, 0)),
            pl.BlockSpec((w, tn), lambda i, j: (0, j)),
            pl.BlockSpec((w, tn), lambda i, j: (0, nj + j)),
            pl.BlockSpec((1, tn), lambda i, j: (0, j)),
            pl.BlockSpec((1, tn), lambda i, j: (0, nj + j)),
            pl.BlockSpec((tn, d), lambda i, j: (j, 0)),
        ],
        out_specs=pl.BlockSpec((tm, d), lambda i, j: (i, 0)),
        out_shape=jax.ShapeDtypeStruct((m, d), F32),
        scratch_shapes=[pltpu.VMEM((tm, d), F32)],
        compiler_params=_params("parallel", "arbitrary"),
        name=name,
    )(a, uz, x, w_glu, w_glu, b_glu, b_glu, w_out)


def _latent_kernel(x_ref, g_ref, wl_ref, wr_ref, ws_ref, gl_ref, cc_ref, ss_ref, lat_ref, kr_ref, kcat_ref):
    h = _rms(x_ref[...], g_ref[...]).astype(BF16)
    lat = _rms(_dot(h, wl_ref[...]), gl_ref[...])
    kr = _dot(h, wr_ref[...]) * cc_ref[...] + _dot(h, ws_ref[...]) * ss_ref[...]
    lat_ref[...] = lat
    kr_ref[...] = kr[:, :QK_ROPE]
    kcat_ref[:, :KV_LORA] = lat.astype(BF16)
    kcat_ref[:, KV_LORA:] = kr.astype(BF16)


def _latent(x, g, wl, wr, ws, gl, cc, ss, *, tm, name):
    m, d = x.shape
    tm = min(tm, m)
    nt = cc.shape[0] // tm

    def full(a):
        return pl.BlockSpec(a.shape, lambda i: (0,) * a.ndim)

    return pl.pallas_call(
        _latent_kernel,
        grid=(m // tm,),
        in_specs=[
            pl.BlockSpec((tm, d), lambda i: (i, 0)),
            full(g), full(wl), full(wr), full(ws), full(gl),
            pl.BlockSpec((tm, LANES), lambda i: (i % nt, 0)),
            pl.BlockSpec((tm, LANES), lambda i: (i % nt, 0)),
        ],
        out_specs=[
            pl.BlockSpec((tm, KV_LORA), lambda i: (i, 0)),
            pl.BlockSpec((tm, QK_ROPE), lambda i: (i, 0)),
            pl.BlockSpec((tm, QK_PAD), lambda i: (i, 0)),
        ],
        out_shape=[
            jax.ShapeDtypeStruct((m, KV_LORA), F32),
            jax.ShapeDtypeStruct((m, QK_ROPE), F32),
            jax.ShapeDtypeStruct((m, QK_PAD), BF16),
        ],
        compiler_params=_params("parallel"),
        name=name,
    )(x, g, wl, wr, ws, gl, cc, ss)


def _mla_front_kernel(x_ref, g_ref, wcq_ref, wg_ref, gq_ref, wqn_ref, wqr_ref, wqs_ref, wuk_ref,
                      cc_ref, ss_ref, q_ref, sg_ref):
    h = _rms(x_ref[...], g_ref[...]).astype(BF16)
    cq = _rms(_dot(h, wcq_ref[...]), gq_ref[...]).astype(BF16)
    gate = _dot(h, wg_ref[...])
    sg_ref[...] = gate * jax.nn.sigmoid(gate)
    qn = _dot(cq, wqn_ref[...]).astype(BF16)
    qr = _dot(cq, wqr_ref[...])
    qs = _dot(cq, wqs_ref[...])
    cc = cc_ref[...]
    ss = ss_ref[...]
    for hd in range(N_HEADS):
        sl = slice(hd * LANES, (hd + 1) * LANES)
        q_ref[0, hd, :, :KV_LORA] = _dot(qn[:, sl], wuk_ref[hd]).astype(BF16)
        q_ref[0, hd, :, KV_LORA:] = (qr[:, sl] * cc + qs[:, sl] * ss).astype(BF16)


def _mla_front(x, g, wcq, wg, gq, wqn, wqr, wqs, wuk, cc, ss, *, tm, name):
    m, d = x.shape
    tm = min(tm, m)
    nt = cc.shape[0] // tm

    def full(a):
        return pl.BlockSpec(a.shape, lambda i: (0,) * a.ndim)

    return pl.pallas_call(
        _mla_front_kernel,
        grid=(m // tm,),
        in_specs=[
            pl.BlockSpec((tm, d), lambda i: (i, 0)),
            full(g), full(wcq), full(wg), full(gq), full(wqn), full(wqr), full(wqs), full(wuk),
            pl.BlockSpec((tm, LANES), lambda i: (i % nt, 0)),
            pl.BlockSpec((tm, LANES), lambda i: (i % nt, 0)),
        ],
        out_specs=[
            pl.BlockSpec((1, N_HEADS, tm, QK_PAD), lambda i: (i, 0, 0, 0)),
            pl.BlockSpec((tm, N_HEADS * V_HEAD), lambda i: (i, 0)),
        ],
        out_shape=[
            jax.ShapeDtypeStruct((m // tm, N_HEADS, tm, QK_PAD), BF16),
            jax.ShapeDtypeStruct((m, N_HEADS * V_HEAD), F32),
        ],
        compiler_params=_params("parallel"),
        name=name,
    )(x, g, wcq, wg, gq, wqn, wqr, wqs, wuk, cc, ss)


def _lane_tile(x, width):
    return x if width == LANES else jnp.concatenate([x] * (width // LANES), axis=1)


def _softmax_update(s, kl, m_ref, l_ref, acc_ref, rows):
    m_prev = m_ref[rows, :]
    m_new = jnp.maximum(m_prev, jnp.max(s, axis=1, keepdims=True))
    alpha = jnp.exp2(m_prev - m_new)
    p = jnp.exp2(s - _lane_tile(m_new, s.shape[1]))
    l_ref[rows, :] = alpha * l_ref[rows, :] + jnp.sum(p, axis=1, keepdims=True)
    acc_ref[rows, :] = _lane_tile(alpha, kl.shape[1]) * acc_ref[rows, :] + _dot(p.astype(BF16), kl)
    m_ref[rows, :] = m_new


def _attn_prompt_kernel(qi_ref, ki_ref, q_ref, k_ref, o_ref, m_s, l_s, acc_s, *, tq, tk, heads_per_chunk):
    step = pl.program_id(1)
    qi = qi_ref[step]
    ki = ki_ref[step]
    diag = (qi * tq + (tq - 1)) // tk
    rc = heads_per_chunk * tq

    @pl.when(ki == 0)
    def _():
        m_s[...] = jnp.full(m_s.shape, NEG_INF, F32)
        l_s[...] = jnp.zeros(l_s.shape, F32)
        acc_s[...] = jnp.zeros(acc_s.shape, F32)

    def update(masked):
        k = k_ref[...]
        kl = k[:, :KV_LORA]
        for c in range(N_HEADS // heads_per_chunk):
            q = q_ref[0, c * heads_per_chunk:(c + 1) * heads_per_chunk].reshape(rc, QK_PAD)
            s = _dot_nt(q, k) * SCALE_LOG2E
            if masked:
                tok = qi * tq + (lax.broadcasted_iota(jnp.int32, (rc, tk), 0) & (tq - 1))
                col = ki * tk + lax.broadcasted_iota(jnp.int32, (rc, tk), 1)
                s = jnp.where(col <= tok, s, NEG_INF)
            _softmax_update(s, kl, m_s, l_s, acc_s, pl.ds(c * rc, rc))

    @pl.when(ki < diag)
    def _():
        update(False)

    @pl.when(ki == diag)
    def _():
        update(True)
        for hd in range(N_HEADS):
            rows = pl.ds(hd * tq, tq)
            o_ref[0, hd] = (acc_s[rows, :] / _lane_tile(l_s[rows, :], KV_LORA)).astype(BF16)


def _attn_prompt(q, kcat, *, bsz, seq, tq, tk):
    assert tq & (tq - 1) == 0 and seq % tq == 0 and seq % tk == 0
    nq = seq // tq
    nk = seq // tk
    rows = N_HEADS * tq
    pairs = [(i, j) for i in range(nq) for j in range((i * tq + tq - 1) // tk + 1)]
    qi_arr = jnp.asarray([p[0] for p in pairs], jnp.int32)
    ki_arr = jnp.asarray([p[1] for p in pairs], jnp.int32)
    grid_spec = pltpu.PrefetchScalarGridSpec(
        num_scalar_prefetch=2,
        grid=(bsz, len(pairs)),
        in_specs=[
            pl.BlockSpec((1, N_HEADS, tq, QK_PAD), lambda b, s, qi, ki: (b * nq + qi[s], 0, 0, 0)),
            pl.BlockSpec((tk, QK_PAD), lambda b, s, qi, ki: (b * nk + ki[s], 0)),
        ],
        out_specs=pl.BlockSpec((1, N_HEADS, tq, KV_LORA), lambda b, s, qi, ki: (b * nq + qi[s], 0, 0, 0)),
        scratch_shapes=[
            pltpu.VMEM((rows, LANES), F32),
            pltpu.VMEM((rows, LANES), F32),
            pltpu.VMEM((rows, KV_LORA), F32),
        ],
    )
    return pl.pallas_call(
        functools.partial(_attn_prompt_kernel, tq=tq, tk=tk, heads_per_chunk=ATTN_HEADS_PER_CHUNK),
        grid_spec=grid_spec,
        out_shape=jax.ShapeDtypeStruct((bsz * nq, N_HEADS, tq, KV_LORA), BF16),
        compiler_params=_params("parallel", "arbitrary"),
        name="attn_prompt",
    )(qi_arr, ki_arr, q, kcat)


def _attn_sample_kernel(pt_ref, q_ref, nl_ref, nk_ref, lat_hbm, krt_hbm, o_ref,
                        lat_buf, krt_buf, sems, kl_s, kr_s, m_s, l_s, acc_s, *, n_pg, dec_seq):
    step = pl.program_id(1)
    steps = pl.num_programs(1)
    gstep = pl.program_id(0) * steps + step
    total = pl.num_programs(0) * steps
    slot = gstep & 1

    def page_copies(at, buf_slot):
        out = []
        for i in range(n_pg):
            page = pt_ref[at * n_pg + i]
            out.append(pltpu.make_async_copy(lat_hbm.at[page], lat_buf.at[buf_slot, i], sems.at[0, buf_slot]))
            out.append(pltpu.make_async_copy(krt_hbm.at[page], krt_buf.at[buf_slot, i], sems.at[1, buf_slot]))
        return out

    @pl.when(gstep == 0)
    def _():
        for c in page_copies(0, 0):
            c.start()

    @pl.when(gstep + 1 < total)
    def _():
        for c in page_copies(gstep + 1, 1 - slot):
            c.start()

    @pl.when(step == 0)
    def _():
        m_s[...] = jnp.full(m_s.shape, NEG_INF, F32)
        l_s[...] = jnp.zeros(l_s.shape, F32)
        acc_s[...] = jnp.zeros(acc_s.shape, F32)

    for c in page_copies(gstep, slot):
        c.wait()

    for i in range(n_pg):
        kl_s[i * PAGE_SIZE:(i + 1) * PAGE_SIZE, :] = lat_buf[slot, i].astype(BF16)
        kr_s[:, i * PAGE_SIZE:(i + 1) * PAGE_SIZE] = krt_buf[slot, i].astype(BF16)

    q = q_ref[0]
    ql = q[:, :KV_LORA]
    qr = q[:, KV_LORA:KV_LORA + QK_ROPE]
    kl = kl_s[...]
    s = (_dot_nt(ql, kl) + _dot(qr, kr_s[...])) * SCALE_LOG2E
    _softmax_update(s, kl, m_s, l_s, acc_s, slice(None))

    @pl.when(step == pl.num_programs(1) - 1)
    def _():
        nl = nl_ref[0].astype(BF16).astype(F32)
        nk = nk_ref[0].astype(BF16).astype(F32)
        qlf = ql.astype(F32)
        qrf = qr.astype(F32)
        tok = lax.broadcasted_iota(jnp.int32, (q.shape[0], 1), 0) & (dec_seq - 1)
        m = m_s[...]
        l = l_s[...]
        acc = acc_s[...]
        for j in range(dec_seq):
            sj = (jnp.sum(qlf * nl[j:j + 1], axis=-1, keepdims=True)
                  + jnp.sum(qrf * nk[j:j + 1], axis=-1, keepdims=True)) * SCALE_LOG2E
            sj = jnp.where(tok >= j, sj, NEG_INF)
            m_new = jnp.maximum(m, sj)
            alpha = jnp.exp2(m - m_new)
            pj = jnp.exp2(sj - m_new)
            l = alpha * l + pj
            pv = _lane_tile(pj, KV_LORA).astype(BF16).astype(F32) * nl[j:j + 1]
            acc = _lane_tile(alpha, KV_LORA) * acc + pv
            m = m_new
        o_ref[0] = (acc / _lane_tile(l, KV_LORA)).astype(BF16)


def _attn_sample(q, cache_latent, cache_krope_t, page_table, new_lat, new_kr, *, n_pg):
    n_seq, n_pages = page_table.shape
    rows = q.shape[1]
    dec_seq = new_lat.shape[1]
    assert dec_seq & (dec_seq - 1) == 0 and n_pages % n_pg == 0
    steps = n_pages // n_pg
    pt = page_table.reshape(-1)

    def seq_spec(a, b):
        return pl.BlockSpec((1, a, b), lambda n, s, pt_ref: (n, 0, 0))

    grid_spec = pltpu.PrefetchScalarGridSpec(
        num_scalar_prefetch=1,
        grid=(n_seq, steps),
        in_specs=[seq_spec(rows, QK_PAD), seq_spec(dec_seq, KV_LORA), seq_spec(dec_seq, QK_ROPE),
                  pl.BlockSpec(memory_space=pl.ANY), pl.BlockSpec(memory_space=pl.ANY)],
        out_specs=seq_spec(rows, KV_LORA),
        scratch_shapes=[
            pltpu.VMEM((2, n_pg, PAGE_SIZE, KV_LORA), F32),
            pltpu.VMEM((2, n_pg, QK_ROPE, PAGE_SIZE), F32),
            pltpu.SemaphoreType.DMA((2, 2)),
            pltpu.VMEM((n_pg * PAGE_SIZE, KV_LORA), BF16),
            pltpu.VMEM((QK_ROPE, n_pg * PAGE_SIZE), BF16),
            pltpu.VMEM((rows, LANES), F32),
            pltpu.VMEM((rows, LANES), F32),
            pltpu.VMEM((rows, KV_LORA), F32),
        ],
    )
    return pl.pallas_call(
        functools.partial(_attn_sample_kernel, n_pg=n_pg, dec_seq=dec_seq),
        grid_spec=grid_spec,
        out_shape=jax.ShapeDtypeStruct((n_seq, rows, KV_LORA), BF16),
        compiler_params=_params("arbitrary", "arbitrary"),
        name="attn_sample",
    )(pt, q, new_lat, new_kr, cache_latent, cache_krope_t)


def _mla_back_kernel(o_ref, sg_ref, x_ref, wuv_ref, wo_ref, gf_ref, out_ref, *, final_norm):
    parts = []
    for hd in range(N_HEADS):
        oh = _dot(o_ref[0, hd], wuv_ref[hd])
        parts.append((oh * sg_ref[:, hd * V_HEAD:(hd + 1) * V_HEAD]).astype(BF16))
    y = x_ref[...] + _dot(jnp.concatenate(parts, axis=1), wo_ref[...])
    out_ref[...] = _rms(y, gf_ref[...]) if final_norm else y


def _mla_back(o, sg, x, wuv, wo, gf, *, final_norm, name):
    m, d = x.shape
    tm = o.shape[2]

    def full(a):
        return pl.BlockSpec(a.shape, lambda i: (0,) * a.ndim)

    def rows(a):
        return pl.BlockSpec((tm, a.shape[1]), lambda i: (i, 0))

    return pl.pallas_call(
        functools.partial(_mla_back_kernel, final_norm=final_norm),
        grid=(m // tm,),
        in_specs=[pl.BlockSpec((1,) + o.shape[1:], lambda i: (i, 0, 0, 0)), rows(sg), rows(x), full(wuv), full(wo),
                  full(gf)],
        out_specs=pl.BlockSpec((tm, d), lambda i: (i, 0)),
        out_shape=jax.ShapeDtypeStruct((m, d), F32),
        compiler_params=_params("parallel"),
        name=name,
    )(o, sg, x, wuv, wo, gf)


def _rope_tables(pos):
    half = QK_ROPE // 2
    inv = ROPE_THETA ** (-jnp.arange(half, dtype=F32) / half)
    ang = pos.astype(F32)[:, None] * inv[None, :]
    cos = jnp.cos(ang)
    sin = jnp.sin(ang)
    pad = jnp.zeros((pos.shape[0], LANES - QK_ROPE), F32)
    cc = jnp.concatenate([cos, cos, pad], axis=1)
    ss = jnp.concatenate([-sin, sin, pad], axis=1)
    return cc, ss


def _swap_halves(w):
    half = w.shape[-1] // 2
    return jnp.concatenate([w[..., half:], w[..., :half]], axis=-1)


def _pad_lanes(w):
    pad = [(0, 0)] * (w.ndim - 1) + [(0, LANES - w.shape[-1])]
    return jnp.pad(w, pad)


def kernel(x_prompt, x_sample, cache_latent, cache_krope, page_table, state_ssm_re, state_ssm_im, norm_a, w_in_a, a_re, a_im, log_dt, b_re, b_im, c_re, c_im, d_skip, w_glu, b_glu, w_out_a, norm_kv, w_dkv, norm_latent, w_uk, w_uv, norm_b, w_in_b, norm_q, w_uq, w_out_b, norm_f):
    bsz, seq, d = x_prompt.shape
    dbs, dseq, _ = x_sample.shape
    n_a = norm_a.shape[0]
    n_b = norm_b.shape[0]
    width = d_skip.shape[1]
    n_groups = width // GROUP_CH
    past_len = page_table.shape[1] * PAGE_SIZE

    n_chunks = seq // PROMPT_CHUNK
    xp = x_prompt.reshape(bsz, n_chunks, PROMPT_CHUNK, d).transpose(2, 1, 0, 3).reshape(bsz * seq, d)
    assert dseq <= SAMPLE_CHUNK
    lead = SAMPLE_CHUNK - dseq
    xs = x_sample.transpose(1, 0, 2).reshape(dseq * dbs, d)

    hp_re, hp_im, hs_re, hs_im = [], [], [], []
    zeros_p = jnp.zeros((n_groups, bsz, STATE_DIM), F32)
    ops_p = _s5_prep(a_re, a_im, log_dt, b_re, b_im, c_re, c_im, chunk=PROMPT_CHUNK, offset=0)
    ops_s = _s5_prep(a_re, a_im, log_dt, b_re, b_im, c_re, c_im, chunk=SAMPLE_CHUNK, offset=lead)
    for i in range(n_a):
        w_in = w_in_a[i].astype(BF16)
        w_gl = w_glu[i].astype(BF16)
        b_gl = b_glu[i][None, :]
        w_out = w_out_a[i].astype(BF16)
        g = norm_a[i][None, :]
        dsk = d_skip[i][None, :]

        uz = _norm_matmul(xp, g, w_in, parts=2, tm=512, tn=1024, name="s5_in_prompt")
        uz4 = uz.reshape(2, PROMPT_CHUNK, n_chunks * bsz, width)
        a, hr, hi = _s5_scan_fused(uz4, ops_p, zeros_p, zeros_p, dsk, layer=i, bsz=bsz, rows=SCAN_ROWS)
        xp = _s5_glu_out(a.reshape(bsz * seq, width), uz, xp, w_gl, b_gl, w_out, tm=1024, tn=512,
                         name="s5_glu_out_prompt")
        hp_re.append(hr.transpose(1, 0, 2))
        hp_im.append(hi.transpose(1, 0, 2))

        uz = _norm_matmul(xs, g, w_in, parts=2, tm=512, tn=1024, name="s5_in_sample")
        uz4 = jnp.pad(uz.reshape(2, dseq, dbs, width), ((0, 0), (lead, 0), (0, 0), (0, 0)))
        a, hr, hi = _s5_scan_fused(uz4, ops_s, state_ssm_re[i].transpose(1, 0, 2),
                                   state_ssm_im[i].transpose(1, 0, 2), dsk, layer=i, bsz=dbs, rows=dbs)
        xs = _s5_glu_out(a[lead:].reshape(dseq * dbs, width), uz, xs, w_gl, b_gl, w_out, tm=512, tn=512,
                         name="s5_glu_out_sample")
        hs_re.append(hr.transpose(1, 0, 2))
        hs_im.append(hi.transpose(1, 0, 2))

    xp = xp.reshape(PROMPT_CHUNK, n_chunks, bsz, d).transpose(2, 1, 0, 3).reshape(bsz * seq, d)
    xs = xs.reshape(dseq, dbs, d).transpose(1, 0, 2).reshape(dbs * dseq, d)

    cc_p, ss_p = _rope_tables(jnp.arange(seq, dtype=jnp.int32))
    pos_s = past_len + jnp.arange(dseq, dtype=jnp.int32)
    cc_s, ss_s = _rope_tables(jnp.tile(pos_s, dbs))
    w_lat = w_dkv[:, :KV_LORA].astype(BF16)
    w_kr = _pad_lanes(w_dkv[:, KV_LORA:]).astype(BF16)
    w_ks = _pad_lanes(_swap_halves(w_dkv[:, KV_LORA:])).astype(BF16)
    g_kv = norm_kv[None, :]
    g_lat = norm_latent[None, :]
    lat_p, kr_p, kcat_p = _latent(xp, g_kv, w_lat, w_kr, w_ks, g_lat, cc_p, ss_p, tm=512, name="latent_prompt")
    lat_s, kr_s, _ = _latent(xs, g_kv, w_lat, w_kr, w_ks, g_lat, cc_s, ss_s, tm=512, name="latent_sample")
    new_lat = lat_s.reshape(dbs, dseq, KV_LORA)
    new_kr = kr_s.reshape(dbs, dseq, QK_ROPE)

    cache_krope_t = jnp.swapaxes(cache_krope, 1, 2)
    w_ukt = w_uk.astype(BF16).reshape(KV_LORA, N_HEADS, QK_NOPE).transpose(1, 2, 0)
    w_uvh = w_uv.astype(BF16).reshape(KV_LORA, N_HEADS, V_HEAD).transpose(1, 0, 2)
    g_f = norm_f[None, :]
    assert n_b > 0
    for j in range(n_b):
        g = norm_b[j][None, :]
        w_cq = w_in_b[j][:, :Q_LORA].astype(BF16)
        w_gate = w_in_b[j][:, Q_LORA:].astype(BF16)
        g_q = norm_q[j][None, :]
        wq3 = w_uq[j].reshape(Q_LORA, N_HEADS, QK_NOPE + QK_ROPE)
        w_qn = wq3[:, :, :QK_NOPE].reshape(Q_LORA, N_HEADS * QK_NOPE).astype(BF16)
        w_qr = _pad_lanes(wq3[:, :, QK_NOPE:]).reshape(Q_LORA, N_HEADS * LANES).astype(BF16)
        w_qs = _pad_lanes(_swap_halves(wq3[:, :, QK_NOPE:])).reshape(Q_LORA, N_HEADS * LANES).astype(BF16)
        w_o = w_out_b[j].astype(BF16)

        q, sg = _mla_front(xp, g, w_cq, w_gate, g_q, w_qn, w_qr, w_qs, w_ukt, cc_p, ss_p, tm=ATTN_TQ,
                           name="mla_front_prompt")
        o = _attn_prompt(q, kcat_p, bsz=bsz, seq=seq, tq=ATTN_TQ, tk=ATTN_TK)
        last = j == n_b - 1
        xp = _mla_back(o, sg, xp, w_uvh, w_o, g_f, final_norm=last, name="mla_back_prompt")

        q, sg = _mla_front(xs, g, w_cq, w_gate, g_q, w_qn, w_qr, w_qs, w_ukt, cc_s, ss_s, tm=ATTN_TQ,
                           name="mla_front_sample")
        spt = ATTN_TQ // dseq
        q_seq = q.reshape(-1, N_HEADS, spt, dseq, QK_PAD).transpose(0, 2, 1, 3, 4)
        o = _attn_sample(q_seq.reshape(dbs, N_HEADS * dseq, QK_PAD), cache_latent, cache_krope_t, page_table,
                         new_lat, new_kr, n_pg=SAMPLE_PAGES_PER_STEP)
        o = o.reshape(-1, spt, N_HEADS, dseq, KV_LORA).transpose(0, 2, 1, 3, 4)
        xs = _mla_back(o.reshape(-1, N_HEADS, ATTN_TQ, KV_LORA), sg, xs, w_uvh, w_o, g_f, final_norm=last,
                       name="mla_back_sample")

    return (xp.reshape(bsz, seq, d), xs.reshape(dbs, dseq, d),
            lat_p.reshape(bsz, seq, KV_LORA), kr_p.reshape(bsz, seq, QK_ROPE),
            new_lat, new_kr,
            jnp.stack(hp_re), jnp.stack(hp_im), jnp.stack(hs_re), jnp.stack(hs_im))
```

```python
import functools
import math

import jax
import jax.numpy as jnp
from jax import lax
from jax.experimental import pallas as pl
from jax.experimental.pallas import tpu as pltpu

F32 = jnp.float32
BF16 = jnp.bfloat16

GROUP_CH = 16
STATE_DIM = 64
N_HEADS = 8
QK_NOPE = 128
QK_ROPE = 64
V_HEAD = 128
KV_LORA = 256
Q_LORA = 384
PAGE_SIZE = 128
ROPE_THETA = 10000.0
RMS_EPS = 1e-6
SOFTMAX_SCALE = 1.0 / math.sqrt(QK_NOPE + QK_ROPE)
SCALE_LOG2E = SOFTMAX_SCALE * math.log2(math.e)
NEG_INF = -1e30
LANES = 128
QK_PAD = KV_LORA + LANES
PROMPT_CHUNK = 16
SAMPLE_CHUNK = 8
ATTN_TQ = 512
ATTN_TK = 512
ATTN_HEADS_PER_CHUNK = 2
SAMPLE_PAGES_PER_STEP = 64
SCAN_ROWS = 512
VMEM_LIMIT = 56 * 1024 * 1024


def _dot(a, b):
    return jnp.dot(a, b, preferred_element_type=F32)


def _dot_nt(a, b):
    return lax.dot_general(a, b, (((1,), (1,)), ((), ())), preferred_element_type=F32)


def _dot_f32(a, b):
    return jnp.dot(a, b, preferred_element_type=F32, precision=lax.Precision.HIGHEST)


def _rms(x, g):
    return x * lax.rsqrt(jnp.mean(x * x, axis=-1, keepdims=True) + RMS_EPS) * g


def _params(*sem):
    return pltpu.CompilerParams(dimension_semantics=sem, vmem_limit_bytes=VMEM_LIMIT)


def _norm_matmul_kernel(x_ref, g_ref, w_ref, o_ref, *, tn):
    h = _rms(x_ref[...], g_ref[...]).astype(BF16)
    parts, _, n = o_ref.shape
    for p in range(parts):
        for j in range(n // tn):
            o_ref[p, :, j * tn:(j + 1) * tn] = _dot(h, w_ref[:, p * n + j * tn:p * n + (j + 1) * tn])


def _norm_matmul(x, g, w, *, parts, tm, tn, name):
    m, d = x.shape
    n = w.shape[1] // parts
    tm = min(tm, m)
    return pl.pallas_call(
        functools.partial(_norm_matmul_kernel, tn=tn),
        grid=(m // tm,),
        in_specs=[
            pl.BlockSpec((tm, d), lambda i: (i, 0)),
            pl.BlockSpec((1, d), lambda i: (0, 0)),
            pl.BlockSpec(w.shape, lambda i: (0, 0), pipeline_mode=pl.Buffered(1)),
        ],
        out_specs=pl.BlockSpec((parts, tm, n), lambda i: (0, i, 0)),
        out_shape=jax.ShapeDtypeStruct((parts, m, n), F32),
        compiler_params=_params("parallel"),
        name=name,
    )(x, g, w)


def _discretize(lam_re, lam_im, log_dt):
    dt = jnp.exp(log_dt)
    mag = jnp.exp(lam_re * dt)
    lb_re = mag * jnp.cos(lam_im * dt)
    lb_im = mag * jnp.sin(lam_im * dt)
    den = lam_re * lam_re + lam_im * lam_im
    nr = lb_re - 1.0
    f_re = (nr * lam_re + lb_im * lam_im) / den
    f_im = (lb_im * lam_re - nr * lam_im) / den
    return lb_re, lb_im, f_re, f_im


def _cpow(br, bi, e, nbits, shape):
    br = jnp.broadcast_to(br, shape)
    bi = jnp.broadcast_to(bi, shape)
    rr = jnp.ones(shape, F32)
    ri = jnp.zeros(shape, F32)
    for j in range(nbits):
        bit = ((e >> j) & 1) == 1
        nr = rr * br - ri * bi
        ni = rr * bi + ri * br
        rr = jnp.where(bit, nr, rr)
        ri = jnp.where(bit, ni, ri)
        if j + 1 < nbits:
            br, bi = br * br - bi * bi, 2.0 * br * bi
    return rr, ri


def _s5_prep_kernel(pcol_ref, prow_ref, btr_ref, bti_ref, ctr_ref, cti_ref,
                    toep_ref, wsr_ref, wsi_ref, wor_ref, woi_ref, at_ref, *, chunk, offset):
    gpb, p, tl = ctr_ref.shape
    nbits = chunk.bit_length()
    shift = GROUP_CH.bit_length() - 1
    per_half = LANES // GROUP_CH

    pc = pcol_ref[...].reshape(gpb * p, 3)
    lbr_c, lbi_c, _, _ = _discretize(pc[:, 0:1], pc[:, 1:2], pc[:, 2:3])
    cr = ctr_ref[...].reshape(gpb * p, tl)
    ci = cti_ref[...].reshape(gpb * p, tl)
    shape = cr.shape
    lane_t = lax.broadcasted_iota(jnp.int32, shape, 1) >> shift
    p0r, p0i = _cpow(lbr_c, lbi_c, lane_t, nbits, shape)
    if offset == 0:
        p1r = p0r * lbr_c - p0i * lbi_c
        p1i = p0r * lbi_c + p0i * lbr_c
    else:
        p1r, p1i = _cpow(lbr_c, lbi_c, jnp.maximum(lane_t + (1 - offset), 0), nbits, shape)
    rr = cr * p0r - ci * p0i
    ri = cr * p0i + ci * p0r
    wor = cr * p1r - ci * p1i
    woi = -(cr * p1i + ci * p1r)

    pr = prow_ref[...]
    lbr_r, lbi_r, f_re, f_im = _discretize(pr[:, 0:1, :], pr[:, 1:2, :], pr[:, 2:3, :])
    btr = btr_ref[...]
    bti = bti_ref[...]
    bbr = f_re * btr - f_im * bti
    bbi = f_re * bti + f_im * btr
    pwr, pwi = [jnp.ones_like(lbr_r)], [jnp.zeros_like(lbr_r)]
    for _ in range(chunk):
        pwr, pwi = (pwr + [pwr[-1] * lbr_r - pwi[-1] * lbi_r], pwi + [pwr[-1] * lbi_r + pwi[-1] * lbr_r])
    wsr = [(bbr * pwr[chunk - 1 - s] - bbi * pwi[chunk - 1 - s]).astype(BF16) for s in range(chunk)]
    wsi = [(bbr * pwi[chunk - 1 - s] + bbi * pwr[chunk - 1 - s]).astype(BF16) for s in range(chunk)]
    at_ref[:, 0:1, :] = pwr[chunk - offset]
    at_ref[:, 1:2, :] = pwi[chunk - offset]

    lane = lax.broadcasted_iota(jnp.int32, (GROUP_CH, tl), 1)
    for g in range(gpb):
        rows = slice(g * p, (g + 1) * p)

        def rot_cols(x, g=g):
            return jnp.concatenate([_rot_blocks(x[:, h * LANES:(h + 1) * LANES], g) for h in range(tl // LANES)],
                                   axis=1)

        def dest_row(s, g=g):
            return LANES * (s // per_half) + GROUP_CH * ((s + g) % per_half)

        wor_ref[g] = rot_cols(wor[rows]).astype(BF16)
        woi_ref[g] = rot_cols(woi[rows]).astype(BF16)

        krow = _dot_f32(bbr[g], rr[rows]) - _dot_f32(bbi[g], ri[rows])
        for s in range(chunk):
            if s == 0:
                blk = krow
            else:
                blk = jnp.where(lane >= GROUP_CH * s, pltpu.roll(krow, GROUP_CH * s, 1), 0.0)
            dst = slice(dest_row(s), dest_row(s) + GROUP_CH)
            toep_ref[g, dst, :] = rot_cols(blk).astype(BF16)
            wsr_ref[g, dst, :] = wsr[s][g]
            wsi_ref[g, dst, :] = wsi[s][g]


def _s5_prep(a_re, a_im, log_dt, b_re, b_im, c_re, c_im, *, chunk, offset):
    steps = chunk - offset
    assert steps > 0 and steps & (steps - 1) == 0 and chunk & (chunk - 1) == 0
    assert (chunk * GROUP_CH) % LANES == 0
    p = a_re.shape[-1]
    a_re, a_im = a_re.reshape(-1, p), a_im.reshape(-1, p)
    log_dt = log_dt.reshape(-1)
    b_re, b_im = b_re.reshape((-1,) + b_re.shape[-2:]), b_im.reshape((-1,) + b_im.shape[-2:])
    c_re, c_im = c_re.reshape((-1,) + c_re.shape[-2:]), c_im.reshape((-1,) + c_im.shape[-2:])
    g = a_re.shape[0]
    gpb = LANES // GROUP_CH
    tc = chunk * GROUP_CH
    ldt = jnp.broadcast_to(log_dt[:, None], (g, p))
    pcol = jnp.stack([a_re, a_im, ldt], axis=-1)
    prow = jnp.stack([a_re, a_im, ldt], axis=1)
    bt_re = jnp.swapaxes(b_re, 1, 2)
    bt_im = jnp.swapaxes(b_im, 1, 2)
    ct_re = jnp.tile(jnp.swapaxes(c_re, 1, 2), (1, 1, chunk))
    ct_im = jnp.tile(jnp.swapaxes(c_im, 1, 2), (1, 1, chunk))

    def spec(a, b):
        return pl.BlockSpec((gpb, a, b), lambda i: (i, 0, 0))

    return pl.pallas_call(
        functools.partial(_s5_prep_kernel, chunk=chunk, offset=offset),
        grid=(g // gpb,),
        in_specs=[spec(p, 3), spec(3, p), spec(GROUP_CH, p), spec(GROUP_CH, p), spec(p, tc), spec(p, tc)],
        out_specs=[spec(tc, tc), spec(tc, p), spec(tc, p), spec(p, tc), spec(p, tc), spec(2, p)],
        out_shape=[
            jax.ShapeDtypeStruct((g, tc, tc), BF16),
            jax.ShapeDtypeStruct((g, tc, p), BF16),
            jax.ShapeDtypeStruct((g, tc, p), BF16),
            jax.ShapeDtypeStruct((g, p, tc), BF16),
            jax.ShapeDtypeStruct((g, p, tc), BF16),
            jax.ShapeDtypeStruct((g, 2, p), F32),
        ],
        compiler_params=_params("parallel"),
        name=f"s5_prep_t{chunk}",
    )(pcol, prow, bt_re, bt_im, ct_re, ct_im)


def _rot_blocks(x, k):
    k %= LANES // GROUP_CH
    return x if k == 0 else pltpu.roll(x, k * GROUP_CH, 1)


def _merge_blocks(xs, offset):
    n = LANES // GROUP_CH
    blk = lax.broadcasted_iota(jnp.int32, xs[0].shape, 1) >> (GROUP_CH.bit_length() - 1)
    out = xs[(-offset) % n]
    for b in range(1, n):
        out = jnp.where(blk == b, xs[(b - offset) % n], out)
    return out


def _s5_scan_fused_kernel(u_ref, toep_ref, wsr_ref, wsi_ref, wor_ref, woi_ref, at_ref, h0r_ref, h0i_ref, d_ref,
                          a_ref, hr_ref, hi_ref, hr_s, hi_s, sr_s, si_s, pr_s, pi_s, y_s, *, bsz):
    r = pl.program_id(1)
    _, n_t, rows, _ = u_ref.shape
    gpb = toep_ref.shape[0]
    n_half = n_t // 8

    @pl.when(r == 0)
    def _():
        hr_s[...] = h0r_ref[...]
        hi_s[...] = h0i_ref[...]

    rot = [_rot_blocks(u_ref[0, t], t).astype(BF16) for t in range(n_t)]
    for g in range(gpb):
        u_g = jnp.concatenate([_merge_blocks(rot[h * 8:(h + 1) * 8], g) for h in range(n_half)], axis=1)
        y_s[g] = _dot(u_g, toep_ref[g])
        sr_s[g] = _dot(u_g, wsr_ref[g])
        si_s[g] = _dot(u_g, wsi_ref[g])

    ar = at_ref[:, 0:1, :]
    ai = at_ref[:, 1:2, :]

    def body(k, carry):
        hr, hi = carry
        rs = pl.ds(pl.multiple_of(k * bsz, bsz), bsz)
        pr_s[:, rs, :] = hr
        pi_s[:, rs, :] = hi
        return (ar * hr - ai * hi + sr_s[:, rs, :], ar * hi + ai * hr + si_s[:, rs, :])

    hr, hi = lax.fori_loop(0, rows // bsz, body, (hr_s[...], hi_s[...]))
    hr_s[...] = hr
    hi_s[...] = hi

    @pl.when(r == pl.num_programs(1) - 1)
    def _():
        hr_ref[...] = hr
        hi_ref[...] = hi

    ys = []
    for g in range(gpb):
        ys.append(y_s[g] + _dot(pr_s[g].astype(BF16), wor_ref[g]) + _dot(pi_s[g].astype(BF16), woi_ref[g]))
    d = d_ref[...]
    for h in range(n_half):
        y_half = [y[:, h * LANES:(h + 1) * LANES] for y in ys]
        for tl in range(8):
            t = h * 8 + tl
            y_t = _rot_blocks(_merge_blocks(y_half, tl), -tl)
            a_ref[t] = jax.nn.gelu(y_t + d * u_ref[0, t]).astype(BF16)


def _s5_scan_fused(uz4, ops, h0r, h0i, d_skip, *, layer, bsz, rows):
    toep, wsr, wsi, wor, woi, at = ops
    _, n_t, n, width = uz4.shape
    g = width // GROUP_CH
    tc = toep.shape[1]
    p = STATE_DIM
    gpb = LANES // GROUP_CH
    nblk = g // gpb
    assert n_t % 8 == 0 and tc == n_t * GROUP_CH and n % rows == 0 and rows % bsz == 0

    def gspec(a, b):
        return pl.BlockSpec((gpb, a, b), lambda i, r: (i, 0, 0))

    def ospec(a, b):
        return pl.BlockSpec((gpb, a, b), lambda i, r: (layer * nblk + i, 0, 0))

    return pl.pallas_call(
        functools.partial(_s5_scan_fused_kernel, bsz=bsz),
        grid=(nblk, n // rows),
        in_specs=[
            pl.BlockSpec((1, n_t, rows, LANES), lambda i, r: (0, 0, r, i)),
            ospec(tc, tc), ospec(tc, p), ospec(tc, p), ospec(p, tc), ospec(p, tc), ospec(2, p),
            gspec(bsz, p), gspec(bsz, p),
            pl.BlockSpec((1, LANES), lambda i, r: (0, i)),
        ],
        out_specs=[
            pl.BlockSpec((n_t, rows, LANES), lambda i, r: (0, r, i)),
            gspec(bsz, p), gspec(bsz, p),
        ],
        out_shape=[
            jax.ShapeDtypeStruct((n_t, n, width), BF16),
            jax.ShapeDtypeStruct((g, bsz, p), F32),
            jax.ShapeDtypeStruct((g, bsz, p), F32),
        ],
        scratch_shapes=[pltpu.VMEM((gpb, bsz, p), F32)] * 2 + [pltpu.VMEM((gpb, rows, p), F32)] * 4
        + [pltpu.VMEM((gpb, rows, tc), F32)],
        compiler_params=_params("parallel", "arbitrary"),
        name=f"s5_scan_fused_t{n_t}",
    )(uz4, toep, wsr, wsi, wor, woi, at, h0r, h0i, d_skip)


def _s5_glu_out_kernel(a_ref, z_ref, x_ref, wa_ref, wb_ref, ba_ref, bb_ref, wo_ref, o_ref, acc_s):
    j = pl.program_id(1)

    @pl.when(j == 0)
    def _():
        acc_s[...] = jnp.zeros(acc_s.shape, F32)

    a = a_ref[...]
    ga = _dot(a, wa_ref[...]) + ba_ref[...]
    gb = _dot(a, wb_ref[...]) + bb_ref[...]
    z = z_ref[0]
    v = (ga * jax.nn.sigmoid(gb) * (z * jax.nn.sigmoid(z))).astype(BF16)
    acc_s[...] += _dot(v, wo_ref[...])

    @pl.when(j == pl.num_programs(1) - 1)
    def _():
        o_ref[...] = x_ref[...] + acc_s[...]


def _s5_glu_out(a, uz, x, w_glu, b_glu, w_out, *, tm, tn, name):
    m, w = a.shape
    d = x.shape[1]
    tm = min(tm, m)
    nj = w // tn
    return pl.pallas_call(
        _s5_glu_out_kernel,
        grid=(m // tm, nj),
        in_specs=[
            pl.BlockSpec((tm, w), lambda i, j: (i, 0)),
            pl.BlockSpec((1, tm, tn), lambda i, j: (1, i, j)),
            pl.BlockSpec((tm, d), lambda i, j: (i, 0)),
            pl.BlockSpec((w, tn), lambda i, j: (0, j)),
            pl.BlockSpec((w, tn), lambda i, j: (0, nj + j)),
            pl.BlockSpec((1, tn), lambda i, j: (0, j)),
            pl.BlockSpec((1, tn), lambda i, j: (0, nj + j)),
            pl.BlockSpec((tn, d), lambda i, j: (j, 0)),
        ],
        out_specs=pl.BlockSpec((tm, d), lambda i, j: (i, 0)),
        out_shape=jax.ShapeDtypeStruct((m, d), F32),
        scratch_shapes=[pltpu.VMEM((tm, d), F32)],
        compiler_params=_params("parallel", "arbitrary"),
        name=name,
    )(a, uz, x, w_glu, w_glu, b_glu, b_glu, w_out)


def _latent_kernel(x_ref, g_ref, wl_ref, wr_ref, ws_ref, gl_ref, cc_ref, ss_ref, lat_ref, kr_ref, kcat_ref):
    h = _rms(x_ref[...], g_ref[...]).astype(BF16)
    lat = _rms(_dot(h, wl_ref[...]), gl_ref[...])
    kr = _dot(h, wr_ref[...]) * cc_ref[...] + _dot(h, ws_ref[...]) * ss_ref[...]
    lat_ref[...] = lat
    kr_ref[...] = kr[:, :QK_ROPE]
    kcat_ref[:, :KV_LORA] = lat.astype(BF16)
    kcat_ref[:, KV_LORA:] = kr.astype(BF16)


def _latent(x, g, wl, wr, ws, gl, cc, ss, *, tm, name):
    m, d = x.shape
    tm = min(tm, m)
    nt = cc.shape[0] // tm

    def full(a):
        return pl.BlockSpec(a.shape, lambda i: (0,) * a.ndim)

    return pl.pallas_call(
        _latent_kernel,
        grid=(m // tm,),
        in_specs=[
            pl.BlockSpec((tm, d), lambda i: (i, 0)),
            full(g), full(wl), full(wr), full(ws), full(gl),
            pl.BlockSpec((tm, LANES), lambda i: (i % nt, 0)),
            pl.BlockSpec((tm, LANES), lambda i: (i % nt, 0)),
        ],
        out_specs=[
            pl.BlockSpec((tm, KV_LORA), lambda i: (i, 0)),
            pl.BlockSpec((tm, QK_ROPE), lambda i: (i, 0)),
            pl.BlockSpec((tm, QK_PAD), lambda i: (i, 0)),
        ],
        out_shape=[
            jax.ShapeDtypeStruct((m, KV_LORA), F32),
            jax.ShapeDtypeStruct((m, QK_ROPE), F32),
            jax.ShapeDtypeStruct((m, QK_PAD), BF16),
        ],
        compiler_params=_params("parallel"),
        name=name,
    )(x, g, wl, wr, ws, gl, cc, ss)


def _mla_front_kernel(x_ref, g_ref, wcq_ref, wg_ref, gq_ref, wqn_ref, wqr_ref, wqs_ref, wuk_ref,
                      cc_ref, ss_ref, q_ref, sg_ref):
    h = _rms(x_ref[...], g_ref[...]).astype(BF16)
    cq = _rms(_dot(h, wcq_ref[...]), gq_ref[...]).astype(BF16)
    gate = _dot(h, wg_ref[...])
    sg_ref[...] = gate * jax.nn.sigmoid(gate)
    qn = _dot(cq, wqn_ref[...]).astype(BF16)
    qr = _dot(cq, wqr_ref[...])
    qs = _dot(cq, wqs_ref[...])
    cc = cc_ref[...]
    ss = ss_ref[...]
    for hd in range(N_HEADS):
        sl = slice(hd * LANES, (hd + 1) * LANES)
        q_ref[0, hd, :, :KV_LORA] = _dot(qn[:, sl], wuk_ref[hd]).astype(BF16)
        q_ref[0, hd, :, KV_LORA:] = (qr[:, sl] * cc + qs[:, sl] * ss).astype(BF16)


def _mla_front(x, g, wcq, wg, gq, wqn, wqr, wqs, wuk, cc, ss, *, tm, name):
    m, d = x.shape
    tm = min(tm, m)
    nt = cc.shape[0] // tm

    def full(a):
        return pl.BlockSpec(a.shape, lambda i: (0,) * a.ndim)

    return pl.pallas_call(
        _mla_front_kernel,
        grid=(m // tm,),
        in_specs=[
            pl.BlockSpec((tm, d), lambda i: (i, 0)),
            full(g), full(wcq), full(wg), full(gq), full(wqn), full(wqr), full(wqs), full(wuk),
            pl.BlockSpec((tm, LANES), lambda i: (i % nt, 0)),
            pl.BlockSpec((tm, LANES), lambda i: (i % nt, 0)),
        ],
        out_specs=[
            pl.BlockSpec((1, N_HEADS, tm, QK_PAD), lambda i: (i, 0, 0, 0)),
            pl.BlockSpec((tm, N_HEADS * V_HEAD), lambda i: (i, 0)),
        ],
        out_shape=[
            jax.ShapeDtypeStruct((m // tm, N_HEADS, tm, QK_PAD), BF16),
            jax.ShapeDtypeStruct((m, N_HEADS * V_HEAD), F32),
        ],
        compiler_params=_params("parallel"),
        name=name,
    )(x, g, wcq, wg, gq, wqn, wqr, wqs, wuk, cc, ss)


def _lane_tile(x, width):
    return x if width == LANES else jnp.concatenate([x] * (width // LANES), axis=1)


def _softmax_update(s, kl, m_ref, l_ref, acc_ref, rows):
    m_prev = m_ref[rows, :]
    m_new = jnp.maximum(m_prev, jnp.max(s, axis=1, keepdims=True))
    alpha = jnp.exp2(m_prev - m_new)
    p = jnp.exp2(s - _lane_tile(m_new, s.shape[1]))
    l_ref[rows, :] = alpha * l_ref[rows, :] + jnp.sum(p, axis=1, keepdims=True)
    acc_ref[rows, :] = _lane_tile(alpha, kl.shape[1]) * acc_ref[rows, :] + _dot(p.astype(BF16), kl)
    m_ref[rows, :] = m_new


def _attn_prompt_kernel(qi_ref, ki_ref, q_ref, k_ref, o_ref, m_s, l_s, acc_s, *, tq, tk, heads_per_chunk):
    step = pl.program_id(1)
    qi = qi_ref[step]
    ki = ki_ref[step]
    diag = (qi * tq + (tq - 1)) // tk
    rc = heads_per_chunk * tq

    @pl.when(ki == 0)
    def _():
        m_s[...] = jnp.full(m_s.shape, NEG_INF, F32)
        l_s[...] = jnp.zeros(l_s.shape, F32)
        acc_s[...] = jnp.zeros(acc_s.shape, F32)

    def update(masked):
        k = k_ref[...]
        kl = k[:, :KV_LORA]
        for c in range(N_HEADS // heads_per_chunk):
            q = q_ref[0, c * heads_per_chunk:(c + 1) * heads_per_chunk].reshape(rc, QK_PAD)
            s = _dot_nt(q, k) * SCALE_LOG2E
            if masked:
                tok = qi * tq + (lax.broadcasted_iota(jnp.int32, (rc, tk), 0) & (tq - 1))
                col = ki * tk + lax.broadcasted_iota(jnp.int32, (rc, tk), 1)
                s = jnp.where(col <= tok, s, NEG_INF)
            _softmax_update(s, kl, m_s, l_s, acc_s, pl.ds(c * rc, rc))

    @pl.when(ki < diag)
    def _():
        update(False)

    @pl.when(ki == diag)
    def _():
        update(True)
        for hd in range(N_HEADS):
            rows = pl.ds(hd * tq, tq)
            o_ref[0, hd] = (acc_s[rows, :] / _lane_tile(l_s[rows, :], KV_LORA)).astype(BF16)


def _attn_prompt(q, kcat, *, bsz, seq, tq, tk):
    assert tq & (tq - 1) == 0 and seq % tq == 0 and seq % tk == 0
    nq = seq // tq
    nk = seq // tk
    rows = N_HEADS * tq
    pairs = [(i, j) for i in range(nq) for j in range((i * tq + tq - 1) // tk + 1)]
    qi_arr = jnp.asarray([p[0] for p in pairs], jnp.int32)
    ki_arr = jnp.asarray([p[1] for p in pairs], jnp.int32)
    grid_spec = pltpu.PrefetchScalarGridSpec(
        num_scalar_prefetch=2,
        grid=(bsz, len(pairs)),
        in_specs=[
            pl.BlockSpec((1, N_HEADS, tq, QK_PAD), lambda b, s, qi, ki: (b * nq + qi[s], 0, 0, 0)),
            pl.BlockSpec((tk, QK_PAD), lambda b, s, qi, ki: (b * nk + ki[s], 0)),
        ],
        out_specs=pl.BlockSpec((1, N_HEADS, tq, KV_LORA), lambda b, s, qi, ki: (b * nq + qi[s], 0, 0, 0)),
        scratch_shapes=[
            pltpu.VMEM((rows, LANES), F32),
            pltpu.VMEM((rows, LANES), F32),
            pltpu.VMEM((rows, KV_LORA), F32),
        ],
    )
    return pl.pallas_call(
        functools.partial(_attn_prompt_kernel, tq=tq, tk=tk, heads_per_chunk=ATTN_HEADS_PER_CHUNK),
        grid_spec=grid_spec,
        out_shape=jax.ShapeDtypeStruct((bsz * nq, N_HEADS, tq, KV_LORA), BF16),
        compiler_params=_params("parallel", "arbitrary"),
        name="attn_prompt",
    )(qi_arr, ki_arr, q, kcat)


def _attn_sample_kernel(pt_ref, q_ref, nl_ref, nk_ref, lat_hbm, krt_hbm, o_ref,
                        lat_buf, krt_buf, sems, kl_s, kr_s, m_s, l_s, acc_s, *, n_pg, dec_seq):
    step = pl.program_id(1)
    steps = pl.num_programs(1)
    gstep = pl.program_id(0) * steps + step
    total = pl.num_programs(0) * steps
    slot = gstep & 1

    def page_copies(at, buf_slot):
        out = []
        for i in range(n_pg):
            page = pt_ref[at * n_pg + i]
            out.append(pltpu.make_async_copy(lat_hbm.at[page], lat_buf.at[buf_slot, i], sems.at[0, buf_slot]))
            out.append(pltpu.make_async_copy(krt_hbm.at[page], krt_buf.at[buf_slot, i], sems.at[1, buf_slot]))
        return out

    @pl.when(gstep == 0)
    def _():
        for c in page_copies(0, 0):
            c.start()

    @pl.when(gstep + 1 < total)
    def _():
        for c in page_copies(gstep + 1, 1 - slot):
            c.start()

    @pl.when(step == 0)
    def _():
        m_s[...] = jnp.full(m_s.shape, NEG_INF, F32)
        l_s[...] = jnp.zeros(l_s.shape, F32)
        acc_s[...] = jnp.zeros(acc_s.shape, F32)

    for c in page_copies(gstep, slot):
        c.wait()

    for i in range(n_pg):
        kl_s[i * PAGE_SIZE:(i + 1) * PAGE_SIZE, :] = lat_buf[slot, i].astype(BF16)
        kr_s[:, i * PAGE_SIZE:(i + 1) * PAGE_SIZE] = krt_buf[slot, i].astype(BF16)

    q = q_ref[0]
    ql = q[:, :KV_LORA]
    qr = q[:, KV_LORA:KV_LORA + QK_ROPE]
    kl = kl_s[...]
    s = (_dot_nt(ql, kl) + _dot(qr, kr_s[...])) * SCALE_LOG2E
    _softmax_update(s, kl, m_s, l_s, acc_s, slice(None))

    @pl.when(step == pl.num_programs(1) - 1)
    def _():
        nl = nl_ref[0].astype(BF16).astype(F32)
        nk = nk_ref[0].astype(BF16).astype(F32)
        qlf = ql.astype(F32)
        qrf = qr.astype(F32)
        tok = lax.broadcasted_iota(jnp.int32, (q.shape[0], 1), 0) & (dec_seq - 1)
        m = m_s[...]
        l = l_s[...]
        acc = acc_s[...]
        for j in range(dec_seq):
            sj = (jnp.sum(qlf * nl[j:j + 1], axis=-1, keepdims=True)
                  + jnp.sum(qrf * nk[j:j + 1], axis=-1, keepdims=True)) * SCALE_LOG2E
            sj = jnp.where(tok >= j, sj, NEG_INF)
            m_new = jnp.maximum(m, sj)
            alpha = jnp.exp2(m - m_new)
            pj = jnp.exp2(sj - m_new)
            l = alpha * l + pj
            pv = _lane_tile(pj, KV_LORA).astype(BF16).astype(F32) * nl[j:j + 1]
            acc = _lane_tile(alpha, KV_LORA) * acc + pv
            m = m_new
        o_ref[0] = (acc / _lane_tile(l, KV_LORA)).astype(BF16)


def _attn_sample(q, cache_latent, cache_krope_t, page_table, new_lat, new_kr, *, n_pg):
    n_seq, n_pages = page_table.shape
    rows = q.shape[1]
    dec_seq = new_lat.shape[1]
    assert dec_seq & (dec_seq - 1) == 0 and n_pages % n_pg == 0
    steps = n_pages // n_pg
    pt = page_table.reshape(-1)

    def seq_spec(a, b):
        return pl.BlockSpec((1, a, b), lambda n, s, pt_ref: (n, 0, 0))

    grid_spec = pltpu.PrefetchScalarGridSpec(
        num_scalar_prefetch=1,
        grid=(n_seq, steps),
        in_specs=[seq_spec(rows, QK_PAD), seq_spec(dec_seq, KV_LORA), seq_spec(dec_seq, QK_ROPE),
                  pl.BlockSpec(memory_space=pl.ANY), pl.BlockSpec(memory_space=pl.ANY)],
        out_specs=seq_spec(rows, KV_LORA),
        scratch_shapes=[
            pltpu.VMEM((2, n_pg, PAGE_SIZE, KV_LORA), F32),
            pltpu.VMEM((2, n_pg, QK_ROPE, PAGE_SIZE), F32),
            pltpu.SemaphoreType.DMA((2, 2)),
            pltpu.VMEM((n_pg * PAGE_SIZE, KV_LORA), BF16),
            pltpu.VMEM((QK_ROPE, n_pg * PAGE_SIZE), BF16),
            pltpu.VMEM((rows, LANES), F32),
            pltpu.VMEM((rows, LANES), F32),
            pltpu.VMEM((rows, KV_LORA), F32),
        ],
    )
    return pl.pallas_call(
        functools.partial(_attn_sample_kernel, n_pg=n_pg, dec_seq=dec_seq),
        grid_spec=grid_spec,
        out_shape=jax.ShapeDtypeStruct((n_seq, rows, KV_LORA), BF16),
        compiler_params=_params("arbitrary", "arbitrary"),
        name="attn_sample",
    )(pt, q, new_lat, new_kr, cache_latent, cache_krope_t)


def _mla_back_kernel(o_ref, sg_ref, x_ref, wuv_ref, wo_ref, gf_ref, out_ref, *, final_norm):
    parts = []
    for hd in range(N_HEADS):
        oh = _dot(o_ref[0, hd], wuv_ref[hd])
        parts.append((oh * sg_ref[:, hd * V_HEAD:(hd + 1) * V_HEAD]).astype(BF16))
    y = x_ref[...] + _dot(jnp.concatenate(parts, axis=1), wo_ref[...])
    out_ref[...] = _rms(y, gf_ref[...]) if final_norm else y


def _mla_back(o, sg, x, wuv, wo, gf, *, final_norm, name):
    m, d = x.shape
    tm = o.shape[2]

    def full(a):
        return pl.BlockSpec(a.shape, lambda i: (0,) * a.ndim)

    def rows(a):
        return pl.BlockSpec((tm, a.shape[1]), lambda i: (i, 0))

    return pl.pallas_call(
        functools.partial(_mla_back_kernel, final_norm=final_norm),
        grid=(m // tm,),
        in_specs=[pl.BlockSpec((1,) + o.shape[1:], lambda i: (i, 0, 0, 0)), rows(sg), rows(x), full(wuv), full(wo),
                  full(gf)],
        out_specs=pl.BlockSpec((tm, d), lambda i: (i, 0)),
        out_shape=jax.ShapeDtypeStruct((m, d), F32),
        compiler_params=_params("parallel"),
        name=name,
    )(o, sg, x, wuv, wo, gf)


def _rope_tables(pos):
    half = QK_ROPE // 2
    inv = ROPE_THETA ** (-jnp.arange(half, dtype=F32) / half)
    ang = pos.astype(F32)[:, None] * inv[None, :]
    cos = jnp.cos(ang)
    sin = jnp.sin(ang)
    pad = jnp.zeros((pos.shape[0], LANES - QK_ROPE), F32)
    cc = jnp.concatenate([cos, cos, pad], axis=1)
    ss = jnp.concatenate([-sin, sin, pad], axis=1)
    return cc, ss


def _swap_halves(w):
    half = w.shape[-1] // 2
    return jnp.concatenate([w[..., half:], w[..., :half]], axis=-1)


def _pad_lanes(w):
    pad = [(0, 0)] * (w.ndim - 1) + [(0, LANES - w.shape[-1])]
    return jnp.pad(w, pad)


def kernel(x_prompt, x_sample, cache_latent, cache_krope, page_table, state_ssm_re, state_ssm_im, norm_a, w_in_a, a_re, a_im, log_dt, b_re, b_im, c_re, c_im, d_skip, w_glu, b_glu, w_out_a, norm_kv, w_dkv, norm_latent, w_uk, w_uv, norm_b, w_in_b, norm_q, w_uq, w_out_b, norm_f):
    bsz, seq, d = x_prompt.shape
    dbs, dseq, _ = x_sample.shape
    n_a = norm_a.shape[0]
    n_b = norm_b.shape[0]
    width = d_skip.shape[1]
    n_groups = width // GROUP_CH
    past_len = page_table.shape[1] * PAGE_SIZE

    n_chunks = seq // PROMPT_CHUNK
    xp = x_prompt.reshape(bsz, n_chunks, PROMPT_CHUNK, d).transpose(2, 1, 0, 3).reshape(bsz * seq, d)
    assert dseq <= SAMPLE_CHUNK
    lead = SAMPLE_CHUNK - dseq
    xs = x_sample.transpose(1, 0, 2).reshape(dseq * dbs, d)

    hp_re, hp_im, hs_re, hs_im = [], [], [], []
    zeros_p = jnp.zeros((n_groups, bsz, STATE_DIM), F32)
    ops_p = _s5_prep(a_re, a_im, log_dt, b_re, b_im, c_re, c_im, chunk=PROMPT_CHUNK, offset=0)
    ops_s = _s5_prep(a_re, a_im, log_dt, b_re, b_im, c_re, c_im, chunk=SAMPLE_CHUNK, offset=lead)
    for i in range(n_a):
        w_in = w_in_a[i].astype(BF16)
        w_gl = w_glu[i].astype(BF16)
        b_gl = b_glu[i][None, :]
        w_out = w_out_a[i].astype(BF16)
        g = norm_a[i][None, :]
        dsk = d_skip[i][None, :]

        uz = _norm_matmul(xp, g, w_in, parts=2, tm=512, tn=1024, name="s5_in_prompt")
        uz4 = uz.reshape(2, PROMPT_CHUNK, n_chunks * bsz, width)
        a, hr, hi = _s5_scan_fused(uz4, ops_p, zeros_p, zeros_p, dsk, layer=i, bsz=bsz, rows=SCAN_ROWS)
        xp = _s5_glu_out(a.reshape(bsz * seq, width), uz, xp, w_gl, b_gl, w_out, tm=512, tn=1024,
                         name="s5_glu_out_prompt")
        hp_re.append(hr.transpose(1, 0, 2))
        hp_im.append(hi.transpose(1, 0, 2))

        uz = _norm_matmul(xs, g, w_in, parts=2, tm=512, tn=1024, name="s5_in_sample")
        uz4 = jnp.pad(uz.reshape(2, dseq, dbs, width), ((0, 0), (lead, 0), (0, 0), (0, 0)))
        a, hr, hi = _s5_scan_fused(uz4, ops_s, state_ssm_re[i].transpose(1, 0, 2),
                                   state_ssm_im[i].transpose(1, 0, 2), dsk, layer=i, bsz=dbs, rows=dbs)
        xs = _s5_glu_out(a[lead:].reshape(dseq * dbs, width), uz, xs, w_gl, b_gl, w_out, tm=512, tn=512,
                         name="s5_glu_out_sample")
        hs_re.append(hr.transpose(1, 0, 2))
        hs_im.append(hi.transpose(1, 0, 2))

    xp = xp.reshape(PROMPT_CHUNK, n_chunks, bsz, d).transpose(2, 1, 0, 3).reshape(bsz * seq, d)
    xs = xs.reshape(dseq, dbs, d).transpose(1, 0, 2).reshape(dbs * dseq, d)

    cc_p, ss_p = _rope_tables(jnp.arange(seq, dtype=jnp.int32))
    pos_s = past_len + jnp.arange(dseq, dtype=jnp.int32)
    cc_s, ss_s = _rope_tables(jnp.tile(pos_s, dbs))
    w_lat = w_dkv[:, :KV_LORA].astype(BF16)
    w_kr = _pad_lanes(w_dkv[:, KV_LORA:]).astype(BF16)
    w_ks = _pad_lanes(_swap_halves(w_dkv[:, KV_LORA:])).astype(BF16)
    g_kv = norm_kv[None, :]
    g_lat = norm_latent[None, :]
    lat_p, kr_p, kcat_p = _latent(xp, g_kv, w_lat, w_kr, w_ks, g_lat, cc_p, ss_p, tm=512, name="latent_prompt")
    lat_s, kr_s, _ = _latent(xs, g_kv, w_lat, w_kr, w_ks, g_lat, cc_s, ss_s, tm=512, name="latent_sample")
    new_lat = lat_s.reshape(dbs, dseq, KV_LORA)
    new_kr = kr_s.reshape(dbs, dseq, QK_ROPE)

    cache_krope_t = jnp.swapaxes(cache_krope, 1, 2)
    w_ukt = w_uk.astype(BF16).reshape(KV_LORA, N_HEADS, QK_NOPE).transpose(1, 2, 0)
    w_uvh = w_uv.astype(BF16).reshape(KV_LORA, N_HEADS, V_HEAD).transpose(1, 0, 2)
    g_f = norm_f[None, :]
    assert n_b > 0
    for j in range(n_b):
        g = norm_b[j][None, :]
        w_cq = w_in_b[j][:, :Q_LORA].astype(BF16)
        w_gate = w_in_b[j][:, Q_LORA:].astype(BF16)
        g_q = norm_q[j][None, :]
        wq3 = w_uq[j].reshape(Q_LORA, N_HEADS, QK_NOPE + QK_ROPE)
        w_qn = wq3[:, :, :QK_NOPE].reshape(Q_LORA, N_HEADS * QK_NOPE).astype(BF16)
        w_qr = _pad_lanes(wq3[:, :, QK_NOPE:]).reshape(Q_LORA, N_HEADS * LANES).astype(BF16)
        w_qs = _pad_lanes(_swap_halves(wq3[:, :, QK_NOPE:])).reshape(Q_LORA, N_HEADS * LANES).astype(BF16)
        w_o = w_out_b[j].astype(BF16)

        q, sg = _mla_front(xp, g, w_cq, w_gate, g_q, w_qn, w_qr, w_qs, w_ukt, cc_p, ss_p, tm=ATTN_TQ,
                           name="mla_front_prompt")
        o = _attn_prompt(q, kcat_p, bsz=bsz, seq=seq, tq=ATTN_TQ, tk=ATTN_TK)
        last = j == n_b - 1
        xp = _mla_back(o, sg, xp, w_uvh, w_o, g_f, final_norm=last, name="mla_back_prompt")

        q, sg = _mla_front(xs, g, w_cq, w_gate, g_q, w_qn, w_qr, w_qs, w_ukt, cc_s, ss_s, tm=ATTN_TQ,
                           name="mla_front_sample")
        spt = ATTN_TQ // dseq
        q_seq = q.reshape(-1, N_HEADS, spt, dseq, QK_PAD).transpose(0, 2, 1, 3, 4)
        o = _attn_sample(q_seq.reshape(dbs, N_HEADS * dseq, QK_PAD), cache_latent, cache_krope_t, page_table,
                         new_lat, new_kr, n_pg=SAMPLE_PAGES_PER_STEP)
        o = o.reshape(-1, spt, N_HEADS, dseq, KV_LORA).transpose(0, 2, 1, 3, 4)
        xs = _mla_back(o.reshape(-1, N_HEADS, ATTN_TQ, KV_LORA), sg, xs, w_uvh, w_o, g_f, final_norm=last,
                       name="mla_back_sample")

    return (xp.reshape(bsz, seq, d), xs.reshape(dbs, dseq, d),
            lat_p.reshape(bsz, seq, KV_LORA), kr_p.reshape(bsz, seq, QK_ROPE),
            new_lat, new_kr,
            jnp.stack(hp_re), jnp.stack(hp_im), jnp.stack(hs_re), jnp.stack(hs_im))
```
